```python
import math
import jax, jax.numpy as jnp
from jax import lax
import numpy as np

D_MODEL = 1024
BATCH = 16
SEQ = 256
DEPTH = 4
DEC_BATCH = 2
DEC_SEQ = 2048
PAST_LEN = 256

GRID_W = 64
N_MIXERS = 2
N_HYENA = (DEPTH + 1) // 2
N_ATTN = DEPTH // 2
N_HEADS = 8
HEAD_DIM = D_MODEL // (2 * N_HEADS)
V_DIM = 2 * HEAD_DIM
ROPE_THETA = 10000.0
ROT_FREQS = HEAD_DIM // 4
Q_BLOCK = 128
HYENA_ORDER = 2
SHORT_W = 3
EMB_BANDS = 16
EMB_DIM = 1 + 2 * EMB_BANDS
FILTER_HIDDEN = 64
DECAY_TARGET = 1e-2
DECAY_PCT_SHORT = 0.3
DECAY_PCT_LONG = 1.5
D_FF = 2816
EPS = 1e-6

kernel_name = "hyena_diffattn_prefix_dit_step"


def rmsnorm(x, w):
    x32 = x.astype(jnp.float32)
    y = x32 * lax.rsqrt(jnp.mean(x32 * x32, axis=-1, keepdims=True) + EPS)
    return (y * w.astype(jnp.float32)).astype(x.dtype)


def adaln_mod(cond, w_ada, b_ada):
    m = jax.nn.silu(cond) @ w_ada + b_ada
    return jnp.split(m[:, None, :], 6, axis=-1)


def modulate(h, shift, scale):
    return h * (1.0 + scale) + shift


def dwconv3(x, w, b):
    xp = jnp.pad(x, ((0, 0), (1, 1), (0, 0)))
    return xp[:, :-2] * w[0] + xp[:, 1:-1] * w[1] + xp[:, 2:] * w[2] + b


def hyena_filter(L, f_w1, f_b1, f_freq, f_w2, f_b2, f_w3):
    t = jnp.linspace(0.0, 1.0, L, dtype=jnp.float32)[:, None]
    w = 2.0 * math.pi * jnp.arange(L, dtype=jnp.float32)[:, None] / L
    bands = jnp.linspace(1e-4, EMB_BANDS - 1, EMB_BANDS, dtype=jnp.float32)
    z = jnp.concatenate([t, jnp.cos(bands * w), -jnp.sin(bands * w)], axis=-1)
    z = z.astype(f_w1.dtype)
    hid = jnp.sin(f_freq * (z @ f_w1 + f_b1))
    hid = jnp.sin(f_freq * (hid @ f_w2 + f_b2))
    h = (hid @ f_w3).astype(jnp.float32)
    min_decay = math.log(DECAY_TARGET) / DECAY_PCT_LONG
    max_decay = math.log(DECAY_TARGET) / DECAY_PCT_SHORT
    deltas = jnp.linspace(min_decay, max_decay, D_MODEL, dtype=jnp.float32)
    decay = jnp.exp(-t * jnp.abs(deltas))
    h_fwd = h[:, :D_MODEL] * decay
    h_bwd = h[:, D_MODEL:] * decay
    zero = jnp.zeros((1, D_MODEL), jnp.float32)
    return jnp.concatenate([h_fwd, zero, h_bwd[1:][::-1]], axis=0)


def long_conv(u, k_circ, d_bias):
    L = u.shape[1]
    uf = jnp.fft.rfft(u.astype(jnp.float32), n=2 * L, axis=1)
    kf = jnp.fft.rfft(k_circ, n=2 * L, axis=0)
    y = jnp.fft.irfft(uf * kf[None], n=2 * L, axis=1)[:, :L]
    y = y + u.astype(jnp.float32) * d_bias.astype(jnp.float32)
    return y.astype(u.dtype)


def hyena_mixer(h, w_in, b_in, w_short, b_short, f_w1, f_b1, f_freq, f_w2, f_b2, f_w3,
                d_bias, w_out, b_out):
    L = h.shape[1]
    u = dwconv3(h @ w_in + b_in, w_short, b_short)
    x0, x1, v = jnp.split(u, HYENA_ORDER + 1, axis=-1)
    k_circ = hyena_filter(L, f_w1, f_b1, f_freq, f_w2, f_b2, f_w3)
    y = x0 * long_conv(v * x1, k_circ, d_bias)
    return y @ w_out + b_out


def axial_rope_tables(rows, dtype):
    r = jnp.repeat(jnp.arange(rows, dtype=jnp.float32), GRID_W)
    cidx = jnp.tile(jnp.arange(GRID_W, dtype=jnp.float32), rows)
    inv = ROPE_THETA ** (-jnp.arange(ROT_FREQS, dtype=jnp.float32) / ROT_FREQS)
    ar = r[:, None] * inv
    ac = cidx[:, None] * inv
    cos = jnp.concatenate([jnp.cos(ar), jnp.cos(ar), jnp.cos(ac), jnp.cos(ac)], axis=-1)
    sin = jnp.concatenate([jnp.sin(ar), jnp.sin(ar), jnp.sin(ac), jnp.sin(ac)], axis=-1)
    return cos[:, None, :].astype(dtype), sin[:, None, :].astype(dtype)


def rope_2d(x, cos, sin):
    xa = x.reshape(x.shape[:-1] + (2, 2, ROT_FREQS))
    x1, x2 = xa[..., 0, :], xa[..., 1, :]
    rot = jnp.stack([-x2, x1], axis=-2).reshape(x.shape)
    return x * cos + rot * sin


def qkv_heads(h, w_qkv):
    B, L, _ = h.shape
    qkv = h @ w_qkv
    q, k, v = jnp.split(qkv, 3, axis=-1)
    q = q.reshape(B, L, N_HEADS, 2, HEAD_DIM).transpose(0, 2, 1, 3, 4)
    k = k.reshape(B, L, N_HEADS, 2, HEAD_DIM).transpose(0, 2, 1, 3, 4)
    v = v.reshape(B, L, N_HEADS, V_DIM).transpose(0, 2, 1, 3)
    return q, k, v


def diff_attention(q, k, v, lam, lam_init, subln_w):
    B, H, Lq = q.shape[:3]
    nb = Lq // Q_BLOCK
    scale = HEAD_DIM ** -0.5
    k1, k2 = k[:, :, :, 0], k[:, :, :, 1]
    qb = q.reshape(B, H, nb, Q_BLOCK, 2, HEAD_DIM).transpose(2, 0, 1, 3, 4, 5)

    def block(qblk):
        s1 = jnp.einsum('bhqd,bhkd->bhqk', qblk[:, :, :, 0], k1).astype(jnp.float32) * scale
        s2 = jnp.einsum('bhqd,bhkd->bhqk', qblk[:, :, :, 1], k2).astype(jnp.float32) * scale
        p = jax.nn.softmax(s1, axis=-1) - lam * jax.nn.softmax(s2, axis=-1)
        return jnp.einsum('bhqk,bhkd->bhqd', p.astype(v.dtype), v)

    o = lax.map(block, qb)
    o = o.transpose(1, 2, 0, 3, 4).reshape(B, H, Lq, V_DIM)
    o = rmsnorm(o, subln_w) * (1.0 - lam_init)
    return o.transpose(0, 2, 1, 3).reshape(B, Lq, N_HEADS * V_DIM)


def conv_ffn(h, w_up, w_dw, b_dw, w_down):
    u = dwconv3(h @ w_up, w_dw, b_dw)
    g, val = jnp.split(u, 2, axis=-1)
    return (jax.nn.silu(g) * val) @ w_down


def setup_inputs(seed: int = 0) -> dict:
    key = jax.random.key(seed)
    ks = iter(jax.random.split(key, 40))
    D = D_MODEL

    def nrm(shape, s):
        return jax.random.normal(next(ks), shape, jnp.float32) * s

    return {
        "x_prompt": nrm((BATCH, SEQ, D), 1.0),
        "x_sample": nrm((DEC_BATCH, DEC_SEQ, D), 1.0),
        "cache_k": nrm((DEC_BATCH, N_ATTN, N_HEADS, PAST_LEN, V_DIM), 1.0),
        "cache_v": nrm((DEC_BATCH, N_ATTN, N_HEADS, PAST_LEN, V_DIM), 1.0),
        "c": nrm((DEC_BATCH, D), 1.0),
        "c_ctx": nrm((D,), 1.0),
        "w_ada": nrm((DEPTH, D, 6 * D), 0.5 * D ** -0.5),
        "b_ada": nrm((DEPTH, 6 * D), 0.01),
        "norm_w": 1.0 + nrm((DEPTH, 4, D), 0.05),
        "hy_w_in": nrm((N_HYENA, D, 3 * D), D ** -0.5),
        "hy_b_in": nrm((N_HYENA, 3 * D), 0.01),
        "hy_w_short": nrm((N_HYENA, SHORT_W, 3 * D), SHORT_W ** -0.5),
        "hy_b_short": nrm((N_HYENA, 3 * D), 0.01),
        "hy_f_w1": nrm((N_HYENA, EMB_DIM, FILTER_HIDDEN), EMB_DIM ** -0.5),
        "hy_f_b1": nrm((N_HYENA, FILTER_HIDDEN), 0.1),
        "hy_f_freq": 1.0 + nrm((N_HYENA, FILTER_HIDDEN), 0.05),
        "hy_f_w2": nrm((N_HYENA, FILTER_HIDDEN, FILTER_HIDDEN), FILTER_HIDDEN ** -0.5),
        "hy_f_b2": nrm((N_HYENA, FILTER_HIDDEN), 0.1),
        "hy_f_w3": nrm((N_HYENA, FILTER_HIDDEN, 2 * D), FILTER_HIDDEN ** -0.5),
        "hy_d_bias": nrm((N_HYENA, D), 0.5),
        "hy_w_out": nrm((N_HYENA, D, D), D ** -0.5),
        "hy_b_out": nrm((N_HYENA, D), 0.01),
        "at_w_qkv": nrm((N_ATTN, D, 3 * D), D ** -0.5),
        "at_w_out": nrm((N_ATTN, D, D), D ** -0.5),
        "at_lambda_q1": nrm((N_ATTN, HEAD_DIM), 0.1),
        "at_lambda_k1": nrm((N_ATTN, HEAD_DIM), 0.1),
        "at_lambda_q2": nrm((N_ATTN, HEAD_DIM), 0.1),
        "at_lambda_k2": nrm((N_ATTN, HEAD_DIM), 0.1),
        "at_subln": 1.0 + nrm((N_ATTN, V_DIM), 0.05),
        "ffn_w_up": nrm((DEPTH, D, 2 * D_FF), D ** -0.5),
        "ffn_w_dw": nrm((DEPTH, 3, 2 * D_FF), 3 ** -0.5),
        "ffn_b_dw": nrm((DEPTH, 2 * D_FF), 0.01),
        "ffn_w_down": nrm((DEPTH, D_FF, D), D_FF ** -0.5),
    }


def reference(x_prompt, x_sample, cache_k, cache_v, c, c_ctx, w_ada, b_ada, norm_w,
              hy_w_in, hy_b_in, hy_w_short, hy_b_short, hy_f_w1, hy_f_b1, hy_f_freq,
              hy_f_w2, hy_f_b2, hy_f_w3, hy_d_bias, hy_w_out, hy_b_out,
              at_w_qkv, at_w_out, at_lambda_q1, at_lambda_k1, at_lambda_q2, at_lambda_k2,
              at_subln, ffn_w_up, ffn_w_dw, ffn_b_dw, ffn_w_down):
    ROWS = x_sample.shape[1] // GRID_W
    cos_s, sin_s = axial_rope_tables(ROWS, x_sample.dtype)
    xp, xs = x_prompt, x_sample
    Bp, Lp = xp.shape[:2]
    Bs, Ls = xs.shape[:2]
    Lc = cache_k.shape[3]
    new_k, new_v = [], []
    for i in range(DEPTH):
        mp = adaln_mod(c_ctx[None, :], w_ada[i], b_ada[i])
        ms = adaln_mod(c, w_ada[i], b_ada[i])
        j = i // N_MIXERS
        hp = modulate(rmsnorm(xp, norm_w[i, 0]), mp[0], mp[1])
        hs = modulate(rmsnorm(xs, norm_w[i, 0]), ms[0], ms[1])
        if i % N_MIXERS == 0:
            hy = (hy_w_in[j], hy_b_in[j], hy_w_short[j], hy_b_short[j], hy_f_w1[j],
                  hy_f_b1[j], hy_f_freq[j], hy_f_w2[j], hy_f_b2[j], hy_f_w3[j],
                  hy_d_bias[j], hy_w_out[j], hy_b_out[j])
            yp = hyena_mixer(hp, *hy)
            ys = hyena_mixer(hs, *hy)
        else:
            lam_init = 0.8 - 0.6 * math.exp(-0.3 * i)
            lam = (jnp.exp(jnp.sum(at_lambda_q1[j] * at_lambda_k1[j]).astype(jnp.float32))
                   - jnp.exp(jnp.sum(at_lambda_q2[j] * at_lambda_k2[j]).astype(jnp.float32))
                   + lam_init)
            qp, kp, vp = qkv_heads(hp, at_w_qkv[j])
            new_k.append(kp.reshape(Bp, N_HEADS, Lp, V_DIM))
            new_v.append(vp)
            yp = diff_attention(qp, kp, vp, lam, lam_init, at_subln[j]) @ at_w_out[j]
            qs, ks_, vs = qkv_heads(hs, at_w_qkv[j])
            qs = rope_2d(qs, cos_s, sin_s)
            ks_ = rope_2d(ks_, cos_s, sin_s)
            k_ctx = cache_k[:, j].reshape(Bs, N_HEADS, Lc, 2, HEAD_DIM).astype(ks_.dtype)
            k_all = jnp.concatenate([k_ctx, ks_], axis=2)
            v_all = jnp.concatenate([cache_v[:, j].astype(vs.dtype), vs], axis=2)
            ys = diff_attention(qs, k_all, v_all, lam, lam_init, at_subln[j]) @ at_w_out[j]
        xp = xp + mp[2] * rmsnorm(yp, norm_w[i, 1])
        xs = xs + ms[2] * rmsnorm(ys, norm_w[i, 1])
        ff = (ffn_w_up[i], ffn_w_dw[i], ffn_b_dw[i], ffn_w_down[i])
        hp = modulate(rmsnorm(xp, norm_w[i, 2]), mp[3], mp[4])
        hs = modulate(rmsnorm(xs, norm_w[i, 2]), ms[3], ms[4])
        xp = xp + mp[5] * rmsnorm(conv_ffn(hp, *ff), norm_w[i, 3])
        xs = xs + ms[5] * rmsnorm(conv_ffn(hs, *ff), norm_w[i, 3])
    new_cache_k = jnp.stack(new_k, axis=1)
    new_cache_v = jnp.stack(new_v, axis=1)
    return (xp, xs, new_cache_k, new_cache_v)
```

```python
import functools
import math

import numpy as np
import jax
import jax.numpy as jnp
from jax import lax
from jax.experimental import pallas as pl
from jax.experimental.pallas import tpu as pltpu

D = 1024
BP, LP = 16, 256
BS, LS = 2, 2048
MP = BP * LP
MS = BS * LS
M = MP + MS
DEPTH = 4
GRID_W = 64
N_HEADS = 8
HEAD_DIM = 64
V_DIM = 128
ROPE_THETA = 10000.0
ROT_FREQS = 16
EMB_BANDS = 16
EMB_DIM = 33
FILTER_HIDDEN = 64
DECAY_TARGET = 1e-2
DECAY_PCT_SHORT = 0.3
DECAY_PCT_LONG = 1.5
D_FF = 2816
EPS = 1e-6
PAST = 256

CB = 256
NB_S = LS // CB
N_CHUNK = 2 + 2 * NB_S
TROW = 256
SUB = 8
VMEM_LIMIT = 56 * 1024 * 1024

F32 = jnp.float32
BF16 = jnp.bfloat16


def _dot(a, b):
    return jnp.dot(a, b, preferred_element_type=F32)


def _split(x):
    hi = x.astype(BF16)
    lo = (x - hi.astype(F32)).astype(BF16)
    return hi, lo


def _dot3(a, b):
    ah, al = a
    bh, bl = _split(b)
    return _dot(ah, bh) + _dot(al, bh) + _dot(ah, bl)


def _rms(x, w):
    ms = jnp.mean(x * x, axis=-1, keepdims=True)
    return x * lax.rsqrt(ms + EPS) * w


def _silu(x):
    return x / (1.0 + jnp.exp(-x))


def _cp(sem, vmem=VMEM_LIMIT):
    return pltpu.CompilerParams(dimension_semantics=sem, vmem_limit_bytes=vmem)


def _ada_kernel(c_ref, w_ref, b_ref, o_ref):
    s = _silu(c_ref[...]).astype(BF16)
    o_ref[...] = _dot(s, w_ref[...].astype(BF16)) + b_ref[...]


def _ada(cond8, w_ada, b_ada):
    out = pl.pallas_call(
        _ada_kernel,
        grid=(DEPTH, 6),
        in_specs=[
            pl.BlockSpec((SUB, D), lambda l, k: (0, 0)),
            pl.BlockSpec((None, D, D), lambda l, k: (l, 0, k)),
            pl.BlockSpec((None, 1, D), lambda l, k: (l, 0, k)),
        ],
        out_specs=pl.BlockSpec((None, SUB, D), lambda l, k: (l * 6 + k, 0, 0)),
        out_shape=jax.ShapeDtypeStruct((DEPTH * 6, SUB, D), F32),
        compiler_params=_cp(("arbitrary", "arbitrary")),
        name="ada",
    )(cond8, w_ada, b_ada.reshape(DEPTH, 1, 6 * D))
    return out.reshape(DEPTH * 6 * SUB, 1, D)


def _mod_index(layer, which, tm):
    base = (layer * 6 + which) * SUB
    n_p = MP // tm
    per_b = LS // tm

    def imap(i, *_):
        r = jnp.where(i < n_p, 0, 1 + (i - n_p) // per_b)
        return (base + r, 0, 0)

    return imap


def _in_kernel(has_bias, x_ref, nw_ref, sh_ref, sc_ref, w_ref, *rest):
    if has_bias:
        b_ref, o_ref, h_ref = rest
    else:
        o_ref, h_ref = rest

    @pl.when(pl.program_id(1) == 0)
    def _():
        y = _rms(x_ref[...], nw_ref[...])
        h_ref[...] = (y * (1.0 + sc_ref[...]) + sh_ref[...]).astype(BF16)

    acc = _dot(h_ref[...], w_ref[...])
    if has_bias:
        acc = acc + b_ref[...]
    o_ref[...] = acc


def _proj_in(x, nw, mods, layer, k_shift, w_bf, bias, tm, tn):
    n = w_bf.shape[1]
    has_bias = bias is not None
    in_specs = [
        pl.BlockSpec((tm, D), lambda i, j: (i, 0)),
        pl.BlockSpec((1, D), lambda i, j: (0, 0)),
        pl.BlockSpec((None, 1, D), _mod_index(layer, k_shift, tm)),
        pl.BlockSpec((None, 1, D), _mod_index(layer, k_shift + 1, tm)),
        pl.BlockSpec((D, tn), lambda i, j: (0, j)),
    ]
    args = [x, nw.reshape(1, D), mods, mods, w_bf]
    if has_bias:
        in_specs.append(pl.BlockSpec((1, tn), lambda i, j: (0, j)))
        args.append(bias.reshape(1, n))
    return pl.pallas_call(
        functools.partial(_in_kernel, has_bias),
        grid=(M // tm, n // tn),
        in_specs=in_specs,
        out_specs=pl.BlockSpec((tm, tn), lambda i, j: (i, j)),
        out_shape=jax.ShapeDtypeStruct((M, n), F32),
        scratch_shapes=[pltpu.VMEM((tm, D), BF16)],
        compiler_params=_cp(("arbitrary", "arbitrary")),
        name="proj_in",
    )(*args)


def _out_kernel(has_bias, a_ref, w_ref, *rest):
    if has_bias:
        b_ref, nw_ref, g_ref, x_ref, o_ref = rest
    else:
        nw_ref, g_ref, x_ref, o_ref = rest
    y = _dot(a_ref[...], w_ref[...])
    if has_bias:
        y = y + b_ref[...]
    o_ref[...] = x_ref[...] + g_ref[...] * _rms(y, nw_ref[...])


def _proj_out(a_bf, w_bf, bias, nw, mods, layer, k_gate, x, tm):
    k = a_bf.shape[1]
    has_bias = bias is not None
    in_specs = [
        pl.BlockSpec((tm, k), lambda i: (i, 0)),
        pl.BlockSpec((k, D), lambda i: (0, 0)),
    ]
    args = [a_bf, w_bf]
    if has_bias:
        in_specs.append(pl.BlockSpec((1, D), lambda i: (0, 0)))
        args.append(bias.reshape(1, D))
    in_specs += [
        pl.BlockSpec((1, D), lambda i: (0, 0)),
        pl.BlockSpec((None, 1, D), _mod_index(layer, k_gate, tm)),
        pl.BlockSpec((tm, D), lambda i: (i, 0)),
    ]
    args += [nw.reshape(1, D), mods, x]
    return pl.pallas_call(
        functools.partial(_out_kernel, has_bias),
        grid=(M // tm,),
        in_specs=in_specs,
        out_specs=pl.BlockSpec((tm, D), lambda i: (i, 0)),
        out_shape=jax.ShapeDtypeStruct((M, D), F32),
        compiler_params=_cp(("arbitrary",)),
        name="proj_out",
    )(*args)


def _seq_edges(i):
    n_p = MP // TROW
    per_b = LS // TROW
    pos = (i - n_p) % per_b
    first = jnp.logical_or(i < n_p, pos == 0)
    last = jnp.logical_or(i < n_p, pos == per_b - 1)
    return first, last


def _conv3(main_ref, prev_ref, next_ref, w_ref, b_ref, first, last):
    u = main_ref[...]
    rows = lax.broadcasted_iota(jnp.int32, u.shape, 0)
    pr = jnp.where(first, 0.0, prev_ref[SUB - 1:SUB, :])
    nx = jnp.where(last, 0.0, next_ref[0:1, :])
    up = jnp.where(rows == 0, pr, pltpu.roll(u, 1, axis=0))
    dn = jnp.where(rows == TROW - 1, nx, pltpu.roll(u, TROW - 1, axis=0))
    return up * w_ref[0:1, :] + u * w_ref[1:2, :] + dn * w_ref[2:3, :] + b_ref[...]


def _halo_specs(tc, cblk):
    r = TROW // SUB
    last_blk = M // SUB - 1
    return [
        pl.BlockSpec((TROW, tc), lambda i: (i, cblk)),
        pl.BlockSpec((SUB, tc), lambda i: (jnp.maximum(i * r - 1, 0), cblk)),
        pl.BlockSpec((SUB, tc), lambda i: (jnp.minimum((i + 1) * r, last_blk), cblk)),
    ]


def _hy_dw_kernel(*refs):
    first, last = _seq_edges(pl.program_id(0))
    streams = []
    for s in range(3):
        m, p, n, w, b = refs[5 * s:5 * s + 5]
        streams.append(_conv3(m, p, n, w, b, first, last))
    x0_ref, z_ref = refs[15], refs[16]
    x0_ref[...] = streams[0]
    z_ref[...] = streams[2] * streams[1]


def _hy_dw(u, w_short, b_short):
    in_specs, args = [], []
    for s in range(3):
        in_specs += _halo_specs(D, s)
        in_specs += [pl.BlockSpec((3, D), lambda i, s=s: (0, s)),
                     pl.BlockSpec((1, D), lambda i, s=s: (0, s))]
        args += [u, u, u, w_short, b_short.reshape(1, 3 * D)]
    return pl.pallas_call(
        _hy_dw_kernel,
        grid=(M // TROW,),
        in_specs=in_specs,
        out_specs=[pl.BlockSpec((TROW, D), lambda i: (i, 0))] * 2,
        out_shape=[jax.ShapeDtypeStruct((M, D), F32)] * 2,
        compiler_params=_cp(("arbitrary",)),
        name="hy_dw",
    )(*args)


def _ffn_dw_kernel(*refs):
    first, last = _seq_edges(pl.program_id(0))
    g = _conv3(*refs[0:5], first, last)
    val = _conv3(*refs[5:10], first, last)
    refs[10][...] = (_silu(g) * val).astype(BF16)


def _ffn_dw(u, w_dw, b_dw):
    in_specs, args = [], []
    for s in range(2):
        in_specs += _halo_specs(D_FF, s)
        in_specs += [pl.BlockSpec((3, D_FF), lambda i, s=s: (0, s)),
                     pl.BlockSpec((1, D_FF), lambda i, s=s: (0, s))]
        args += [u, u, u, w_dw, b_dw.reshape(1, 2 * D_FF)]
    return pl.pallas_call(
        _ffn_dw_kernel,
        grid=(M // TROW,),
        in_specs=in_specs,
        out_specs=pl.BlockSpec((TROW, D_FF), lambda i: (i, 0)),
        out_shape=jax.ShapeDtypeStruct((M, D_FF), BF16),
        compiler_params=_cp(("arbitrary",)),
        name="ffn_dw",
    )(*args)


def _dft_mats():
    n = np.arange(CB, dtype=np.float64)
    f = np.arange(CB, dtype=np.float64)
    ang = 2.0 * np.pi * np.outer(f, n) / (2 * CB)
    fwd = np.concatenate([np.cos(ang), -np.sin(ang)], axis=0)
    fwd[CB] = np.cos(np.pi * n)
    scale = np.full((2 * CB, 1), 2.0 / (2 * CB))
    scale[0] = scale[CB] = 1.0 / (2 * CB)
    inv = (fwd * scale).T
    sgn = np.where(np.arange(CB) % 2 == 0, 1.0, -1.0)
    sgn2 = np.concatenate([sgn, sgn])[:, None]
    sgn2[CB] = 1.0
    return (jnp.asarray(fwd, F32), jnp.asarray(inv, F32), jnp.asarray(sgn2, F32))


def _filter_features(L):
    t = np.linspace(0.0, 1.0, L)[:, None]
    w = 2.0 * np.pi * np.arange(L)[:, None] / L
    bands = np.linspace(1e-4, EMB_BANDS - 1, EMB_BANDS)
    z = np.concatenate([t, np.cos(bands * w), -np.sin(bands * w)], axis=-1)
    return t, z


def _filter_tables():
    zs, ts, ms = [], [], []
    for L in (LP, LS):
        t, z = _filter_features(L)
        z = np.pad(z, ((0, 0), (0, FILTER_HIDDEN - EMB_DIM)))
        idx = np.abs(np.arange(2 * L) - L) % L
        zs.append(z[idx])
        ts.append(t[idx])
        ms.append((np.arange(2 * L) != 0).astype(np.float64)[:, None])
    return tuple(jnp.asarray(np.concatenate(a), F32) for a in (zs, ts, ms))


def _filt_kernel(z_ref, t_ref, m_ref, w1_ref, b1_ref, fr_ref, w2_ref, b2_ref, w3_ref,
                 ad_ref, f_ref, o_ref):
    fr = fr_ref[...]
    hid = jnp.sin(fr * (_dot(z_ref[...].astype(BF16), w1_ref[...].astype(BF16)) + b1_ref[...]))
    hid = jnp.sin(fr * (_dot(hid.astype(BF16), w2_ref[...].astype(BF16)) + b2_ref[...]))
    h = _dot(hid.astype(BF16), w3_ref[...].astype(BF16))
    taps = h * jnp.exp(-t_ref[...] * ad_ref[...]) * m_ref[...]
    o_ref[...] = _dot3(_split(f_ref[...]), taps)


def _is_bwd_chunk(c):
    return jnp.logical_or(c == 0, jnp.logical_and(c >= 2, c < 2 + NB_S))


def _filter_spectra(tabs, fwd, w1, b1, freq, w2, b2, w3, absd):
    z_tab, t_tab, m_tab = tabs
    small = lambda shape: pl.BlockSpec(shape, lambda c: (0, 0))
    return pl.pallas_call(
        _filt_kernel,
        grid=(N_CHUNK,),
        in_specs=[
            pl.BlockSpec((CB, FILTER_HIDDEN), lambda c: (c, 0)),
            pl.BlockSpec((CB, 1), lambda c: (c, 0)),
            pl.BlockSpec((CB, 1), lambda c: (c, 0)),
            small((FILTER_HIDDEN, FILTER_HIDDEN)),
            small((1, FILTER_HIDDEN)),
            small((1, FILTER_HIDDEN)),
            small((FILTER_HIDDEN, FILTER_HIDDEN)),
            small((1, FILTER_HIDDEN)),
            pl.BlockSpec((FILTER_HIDDEN, D), lambda c: (0, jnp.where(_is_bwd_chunk(c), 1, 0))),
            small((1, D)),
            small((2 * CB, CB)),
        ],
        out_specs=pl.BlockSpec((None, 2 * CB, D), lambda c: (c, 0, 0)),
        out_shape=jax.ShapeDtypeStruct((N_CHUNK, 2 * CB, D), F32),
        compiler_params=_cp(("arbitrary",)),
        name="hy_filter",
    )(z_tab, t_tab, m_tab, w1, b1.reshape(1, -1), freq.reshape(1, -1), w2, b2.reshape(1, -1),
      w3, absd, fwd)


def _conv_kernel(tc, z_ref, x0_ref, a_ref, f_ref, gi_ref, sg_ref, db_ref, o_ref, u_scr):
    unit = pl.program_id(1)
    fmat = _split(f_ref[...])
    gmat = _split(gi_ref[...])
    sgn = sg_ref[...]
    row0 = lax.broadcasted_iota(jnp.int32, (CB, tc), 0) == 0
    db = db_ref[...]

    def window(c):
        g = a_ref[c] + sgn * a_ref[c - 1]
        gre, gim = g[:CB], g[CB:]
        return gre, jnp.where(row0, 0.0, gim), jnp.where(row0, gim, gre)

    def emit(blk, yre, yim):
        rows = pl.ds(blk * CB, CB)
        y = _dot3(gmat, jnp.concatenate([yre, yim], axis=0))
        zb = z_ref[rows, :]
        o_ref[rows, :] = (x0_ref[rows, :] * (y + zb * db)).astype(BF16)

    @pl.when(unit < MP // LS)
    def _():
        gre, gimz, grez = window(1)
        for s in range(LS // LP):
            u = _dot3(fmat, z_ref[pl.ds(s * CB, CB), :])
            ure, uim = u[:CB], u[CB:]
            emit(s, gre * ure - gimz * uim, grez * uim + gimz * ure)

    @pl.when(unit >= MP // LS)
    def _():
        for j in range(NB_S):
            u_scr[j] = _dot3(fmat, z_ref[pl.ds(j * CB, CB), :])
        for i in range(NB_S):
            yre = jnp.zeros((CB, tc), F32)
            yim = jnp.zeros((CB, tc), F32)
            for j in range(NB_S):
                gre, gimz, grez = window(2 + NB_S + i - j)
                ure, uim = u_scr[j, :CB, :], u_scr[j, CB:, :]
                yre = yre + gre * ure - gimz * uim
                yim = yim + grez * uim + gimz * ure
            emit(i, yre, yim)


def _long_conv(z, x0, spectra, fwd, inv, sgn, d_bias, tc=256):
    return pl.pallas_call(
        functools.partial(_conv_kernel, tc),
        grid=(D // tc, M // LS),
        in_specs=[
            pl.BlockSpec((LS, tc), lambda c, u: (u, c)),
            pl.BlockSpec((LS, tc), lambda c, u: (u, c)),
            pl.BlockSpec((N_CHUNK, 2 * CB, tc), lambda c, u: (0, 0, c)),
            pl.BlockSpec((2 * CB, CB), lambda c, u: (0, 0)),
            pl.BlockSpec((CB, 2 * CB), lambda c, u: (0, 0)),
            pl.BlockSpec((2 * CB, 1), lambda c, u: (0, 0)),
            pl.BlockSpec((1, tc), lambda c, u: (0, c)),
        ],
        out_specs=pl.BlockSpec((LS, tc), lambda c, u: (u, c)),
        out_shape=jax.ShapeDtypeStruct((M, D), BF16),
        scratch_shapes=[pltpu.VMEM((NB_S, 2 * CB, tc), F32)],
        compiler_params=_cp(("arbitrary", "arbitrary")),
        name="hy_conv",
    )(z, x0, spectra, fwd, inv, sgn, d_bias.reshape(1, D))


def _rope_tables(L):
    rows = L // GRID_W
    r = np.repeat(np.arange(rows, dtype=np.float64), GRID_W)
    cidx = np.tile(np.arange(GRID_W, dtype=np.float64), rows)
    inv = ROPE_THETA ** (-np.arange(ROT_FREQS, dtype=np.float64) / ROT_FREQS)
    ar = r[:, None] * inv
    ac = cidx[:, None] * inv
    cos = np.concatenate([np.cos(ar), np.cos(ar), np.cos(ac), np.cos(ac)] * 2, axis=-1)
    sin = np.concatenate([np.sin(ar), np.sin(ar), np.sin(ac), np.sin(ac)] * 2, axis=-1)
    first_half = (np.arange(V_DIM) % (2 * ROT_FREQS)) < ROT_FREQS
    sin_a = np.where(first_half, -sin, 0.0)
    sin_b = np.where(first_half, 0.0, sin)
    return tuple(jnp.asarray(a, F32) for a in (cos, sin_a, sin_b))


def _rope(x, cos, sin_a, sin_b):
    return (x * cos + pltpu.roll(x, V_DIM - ROT_FREQS, axis=1) * sin_a
            + pltpu.roll(x, ROT_FREQS, axis=1) * sin_b)


def _lambda(lv, lam_init):
    a = jnp.exp(jnp.sum(lv[0:1, :] * lv[1:2, :], axis=-1, keepdims=True))
    b = jnp.exp(jnp.sum(lv[2:3, :] * lv[3:4, :], axis=-1, keepdims=True))
    return a - b + lam_init


def _diff_attn(q, k1, k2, v, lam, lam_init, subln):
    dn = (((1,), (1,)), ((), ()))
    s1 = lax.dot_general(q[:, :HEAD_DIM].astype(BF16), k1, dn, preferred_element_type=F32)
    s2 = lax.dot_general(q[:, HEAD_DIM:].astype(BF16), k2, dn, preferred_element_type=F32)
    e1 = jnp.exp(s1 - jnp.max(s1, axis=-1, keepdims=True))
    e2 = jnp.exp(s2 - jnp.max(s2, axis=-1, keepdims=True))
    r1 = 1.0 / jnp.sum(e1, axis=-1, keepdims=True)
    r2 = lam / jnp.sum(e2, axis=-1, keepdims=True)
    p = e1 * r1 - e2 * r2
    o = _dot(p.astype(BF16), v)
    return _rms(o, subln) * (1.0 - lam_init)


def _attn_p_kernel(lam_init, q_ref, k_ref, v_ref, lv_ref, sub_ref, o_ref, nk_ref, nv_ref):
    k = k_ref[...]
    v = v_ref[...]
    nk_ref[...] = k
    nv_ref[...] = v
    lam = _lambda(lv_ref[...], lam_init)
    o = _diff_attn(q_ref[...] * HEAD_DIM ** -0.5, k[:, :HEAD_DIM].astype(BF16),
                   k[:, HEAD_DIM:].astype(BF16), v.astype(BF16), lam, lam_init, sub_ref[...])
    o_ref[...] = o.astype(BF16)


def _attn_prompt(qkv, lamv, subln, lam_init):
    head = lambda off: pl.BlockSpec((LP, V_DIM), lambda b, h: (b, off + h))
    cache = pl.BlockSpec((None, None, LP, V_DIM), lambda b, h: (b, h, 0, 0))
    return pl.pallas_call(
        functools.partial(_attn_p_kernel, lam_init),
        grid=(BP, N_HEADS),
        in_specs=[head(0), head(N_HEADS), head(2 * N_HEADS),
                  pl.BlockSpec((4, HEAD_DIM), lambda b, h: (0, 0)),
                  pl.BlockSpec((1, V_DIM), lambda b, h: (0, 0))],
        out_specs=[pl.BlockSpec((LP, V_DIM), lambda b, h: (b, h)), cache, cache],
        out_shape=[jax.ShapeDtypeStruct((MP, D), BF16),
                   jax.ShapeDtypeStruct((BP, N_HEADS, LP, V_DIM), F32),
                   jax.ShapeDtypeStruct((BP, N_HEADS, LP, V_DIM), F32)],
        compiler_params=_cp(("arbitrary", "arbitrary")),
        name="attn_ctx",
    )(qkv, qkv, qkv, lamv, subln)


def _attn_s_kernel(lam_init, q_ref, k_ref, v_ref, ck_ref, cv_ref, cosq_ref, saq_ref, sbq_ref,
                   cosk_ref, sak_ref, sbk_ref, lv_ref, sub_ref, o_ref, k1_scr, k2_scr, v_scr):
    @pl.when(pl.program_id(2) == 0)
    def _():
        kr = _rope(k_ref[...], cosk_ref[...], sak_ref[...], sbk_ref[...])
        ck = ck_ref[...]
        k1_scr[0:PAST, :] = ck[:, :HEAD_DIM].astype(BF16)
        k2_scr[0:PAST, :] = ck[:, HEAD_DIM:].astype(BF16)
        k1_scr[PAST:, :] = kr[:, :HEAD_DIM].astype(BF16)
        k2_scr[PAST:, :] = kr[:, HEAD_DIM:].astype(BF16)
        v_scr[0:PAST, :] = cv_ref[...].astype(BF16)
        v_scr[PAST:, :] = v_ref[...].astype(BF16)

    lam = _lambda(lv_ref[...], lam_init)
    q = _rope(q_ref[...], cosq_ref[...], saq_ref[...], sbq_ref[...]) * HEAD_DIM ** -0.5
    o = _diff_attn(q, k1_scr[...], k2_scr[...], v_scr[...], lam, lam_init, sub_ref[...])
    o_ref[...] = o.astype(BF16)


def _attn_sample(qkv, cache_k, cache_v, j, ropes, lamv, subln, lam_init, tq=256):
    nq = LS // tq
    row0 = MP // tq
    cos, sin_a, sin_b = ropes
    seq = lambda off: pl.BlockSpec((LS, V_DIM), lambda b, h, q: (MP // LS + b, off + h))
    ctx = pl.BlockSpec((None, None, None, PAST, V_DIM), lambda b, h, q: (b, j, h, 0, 0))
    tab_q = pl.BlockSpec((tq, V_DIM), lambda b, h, q: (q, 0))
    tab_k = pl.BlockSpec((LS, V_DIM), lambda b, h, q: (0, 0))
    return pl.pallas_call(
        functools.partial(_attn_s_kernel, lam_init),
        grid=(BS, N_HEADS, nq),
        in_specs=[pl.BlockSpec((tq, V_DIM), lambda b, h, q: (row0 + b * nq + q, h)),
                  seq(N_HEADS), seq(2 * N_HEADS), ctx, ctx,
                  tab_q, tab_q, tab_q, tab_k, tab_k, tab_k,
                  pl.BlockSpec((4, HEAD_DIM), lambda b, h, q: (0, 0)),
                  pl.BlockSpec((1, V_DIM), lambda b, h, q: (0, 0))],
        out_specs=pl.BlockSpec((tq, V_DIM), lambda b, h, q: (b * nq + q, h)),
        out_shape=jax.ShapeDtypeStruct((MS, D), BF16),
        scratch_shapes=[pltpu.VMEM((PAST + LS, HEAD_DIM), BF16),
                        pltpu.VMEM((PAST + LS, HEAD_DIM), BF16),
                        pltpu.VMEM((PAST + LS, V_DIM), BF16)],
        compiler_params=_cp(("arbitrary", "arbitrary", "arbitrary")),
        name="attn_lat",
    )(qkv, qkv, qkv, cache_k, cache_v, cos, sin_a, sin_b, cos, sin_a, sin_b, lamv, subln)


def kernel(x_prompt, x_sample, cache_k, cache_v, c, c_ctx, w_ada, b_ada, norm_w, hy_w_in, hy_b_in, hy_w_short, hy_b_short, hy_f_w1, hy_f_b1, hy_f_freq, hy_f_w2, hy_f_b2, hy_f_w3, hy_d_bias, hy_w_out, hy_b_out, at_w_qkv, at_w_out, at_lambda_q1, at_lambda_k1, at_lambda_q2, at_lambda_k2, at_subln, ffn_w_up, ffn_w_dw, ffn_b_dw, ffn_w_down):
    x = jnp.concatenate([x_prompt.reshape(MP, D), x_sample.reshape(MS, D)], axis=0)
    cond8 = jnp.concatenate([c_ctx[None, :], c, jnp.zeros((SUB - 1 - BS, D), F32)], axis=0)
    mods = _ada(cond8, w_ada, b_ada)

    fwd, inv, sgn = _dft_mats()
    tabs = _filter_tables()
    min_decay = math.log(DECAY_TARGET) / DECAY_PCT_LONG
    max_decay = math.log(DECAY_TARGET) / DECAY_PCT_SHORT
    absd = jnp.asarray(np.abs(np.linspace(min_decay, max_decay, D))[None, :], F32)
    ropes = _rope_tables(LS)
    w1_pad = jnp.pad(hy_f_w1, ((0, 0), (0, FILTER_HIDDEN - EMB_DIM), (0, 0)))

    new_k, new_v = [], []
    for i in range(DEPTH):
        j = i // 2
        if i % 2 == 0:
            u = _proj_in(x, norm_w[i, 0], mods, i, 0, hy_w_in[j].astype(BF16), hy_b_in[j],
                         tm=512, tn=1024)
            x0, z = _hy_dw(u, hy_w_short[j], hy_b_short[j])
            spectra = _filter_spectra(tabs, fwd, w1_pad[j], hy_f_b1[j], hy_f_freq[j],
                                      hy_f_w2[j], hy_f_b2[j], hy_f_w3[j], absd)
            a = _long_conv(z, x0, spectra, fwd, inv, sgn, hy_d_bias[j])
            x = _proj_out(a, hy_w_out[j].astype(BF16), hy_b_out[j], norm_w[i, 1], mods, i, 2,
                          x, tm=512)
        else:
            lam_init = 0.8 - 0.6 * math.exp(-0.3 * i)
            lamv = jnp.stack([at_lambda_q1[j], at_lambda_k1[j], at_lambda_q2[j],
                              at_lambda_k2[j]], axis=0)
            subln = at_subln[j].reshape(1, V_DIM)
            qkv = _proj_in(x, norm_w[i, 0], mods, i, 0, at_w_qkv[j].astype(BF16), None,
                           tm=512, tn=1024)
            o_p, nk, nv = _attn_prompt(qkv, lamv, subln, lam_init)
            o_s = _attn_sample(qkv, cache_k, cache_v, j, ropes, lamv, subln, lam_init)
            new_k.append(nk)
            new_v.append(nv)
            a = jnp.concatenate([o_p, o_s], axis=0)
            x = _proj_out(a, at_w_out[j].astype(BF16), None, norm_w[i, 1], mods, i, 2, x, tm=512)
        u = _proj_in(x, norm_w[i, 2], mods, i, 3, ffn_w_up[i].astype(BF16), None,
                     tm=512, tn=2 * D_FF // 4)
        act = _ffn_dw(u, ffn_w_dw[i], ffn_b_dw[i])
        x = _proj_out(act, ffn_w_down[i].astype(BF16), None, norm_w[i, 3], mods, i, 5, x, tm=512)

    y_prompt = x[:MP].reshape(BP, LP, D)
    y_sample = x[MP:].reshape(BS, LS, D)
    return (y_prompt, y_sample, jnp.stack(new_k, axis=1), jnp.stack(new_v, axis=1))
```

```python
import functools
import math

import numpy as np
import jax
import jax.numpy as jnp
from jax import lax
from jax.experimental import pallas as pl
from jax.experimental.pallas import tpu as pltpu

D = 1024
BP, LP = 16, 256
BS, LS = 2, 2048
MP = BP * LP
MS = BS * LS
M = MP + MS
DEPTH = 4
GRID_W = 64
N_HEADS = 8
HEAD_DIM = 64
V_DIM = 128
ROPE_THETA = 10000.0
ROT_FREQS = 16
EMB_BANDS = 16
EMB_DIM = 33
FILTER_HIDDEN = 64
DECAY_TARGET = 1e-2
DECAY_PCT_SHORT = 0.3
DECAY_PCT_LONG = 1.5
D_FF = 2816
EPS = 1e-6
PAST = 256

CB = 256
NB_S = LS // CB
N_CHUNK = 2 + 2 * NB_S
TROW = 256
SUB = 8
VMEM_LIMIT = 56 * 1024 * 1024

F32 = jnp.float32
BF16 = jnp.bfloat16


def _dot(a, b):
    return jnp.dot(a, b, preferred_element_type=F32)


def _split(x):
    hi = x.astype(BF16)
    lo = (x - hi.astype(F32)).astype(BF16)
    return hi, lo


def _dot3(a, b):
    ah, al = a
    bh, bl = _split(b)
    return _dot(ah, bh) + _dot(al, bh) + _dot(ah, bl)


def _rms(x, w):
    ms = jnp.mean(x * x, axis=-1, keepdims=True)
    return x * lax.rsqrt(ms + EPS) * w


def _silu(x):
    return x / (1.0 + jnp.exp(-x))


def _cp(sem, vmem=VMEM_LIMIT):
    return pltpu.CompilerParams(dimension_semantics=sem, vmem_limit_bytes=vmem)


def _ada_kernel(c_ref, w_ref, b_ref, o_ref):
    s = _silu(c_ref[...]).astype(BF16)
    o_ref[...] = _dot(s, w_ref[...].astype(BF16)) + b_ref[...]


def _ada(cond8, w_ada, b_ada):
    out = pl.pallas_call(
        _ada_kernel,
        grid=(DEPTH, 6),
        in_specs=[
            pl.BlockSpec((SUB, D), lambda l, k: (0, 0)),
            pl.BlockSpec((None, D, D), lambda l, k: (l, 0, k)),
            pl.BlockSpec((None, 1, D), lambda l, k: (l, 0, k)),
        ],
        out_specs=pl.BlockSpec((None, SUB, D), lambda l, k: (l * 6 + k, 0, 0)),
        out_shape=jax.ShapeDtypeStruct((DEPTH * 6, SUB, D), F32),
        compiler_params=_cp(("arbitrary", "arbitrary")),
        name="ada",
    )(cond8, w_ada, b_ada.reshape(DEPTH, 1, 6 * D))
    return out.reshape(DEPTH * 6 * SUB, 1, D)


def _mod_index(layer, which, tm):
    base = (layer * 6 + which) * SUB
    n_p = MP // tm
    per_b = LS // tm

    def imap(i, *_):
        r = jnp.where(i < n_p, 0, 1 + (i - n_p) // per_b)
        return (base + r, 0, 0)

    return imap


def _in_kernel(has_bias, x_ref, nw_ref, sh_ref, sc_ref, w_ref, *rest):
    if has_bias:
        b_ref, o_ref, h_ref = rest
    else:
        o_ref, h_ref = rest

    @pl.when(pl.program_id(1) == 0)
    def _():
        y = _rms(x_ref[...], nw_ref[...])
        h_ref[...] = (y * (1.0 + sc_ref[...]) + sh_ref[...]).astype(BF16)

    acc = _dot(h_ref[...], w_ref[...])
    if has_bias:
        acc = acc + b_ref[...]
    o_ref[...] = acc


def _proj_in(x, nw, mods, layer, k_shift, w_bf, bias, tm, tn):
    n = w_bf.shape[1]
    has_bias = bias is not None
    in_specs = [
        pl.BlockSpec((tm, D), lambda i, j: (i, 0)),
        pl.BlockSpec((1, D), lambda i, j: (0, 0)),
        pl.BlockSpec((None, 1, D), _mod_index(layer, k_shift, tm)),
        pl.BlockSpec((None, 1, D), _mod_index(layer, k_shift + 1, tm)),
        pl.BlockSpec((D, tn), lambda i, j: (0, j)),
    ]
    args = [x, nw.reshape(1, D), mods, mods, w_bf]
    if has_bias:
        in_specs.append(pl.BlockSpec((1, tn), lambda i, j: (0, j)))
        args.append(bias.reshape(1, n))
    return pl.pallas_call(
        functools.partial(_in_kernel, has_bias),
        grid=(M // tm, n // tn),
        in_specs=in_specs,
        out_specs=pl.BlockSpec((tm, tn), lambda i, j: (i, j)),
        out_shape=jax.ShapeDtypeStruct((M, n), F32),
        scratch_shapes=[pltpu.VMEM((tm, D), BF16)],
        compiler_params=_cp(("arbitrary", "arbitrary")),
        name="proj_in",
    )(*args)


def _out_kernel(has_bias, a_ref, w_ref, *rest):
    if has_bias:
        b_ref, nw_ref, g_ref, x_ref, o_ref = rest
    else:
        nw_ref, g_ref, x_ref, o_ref = rest
    y = _dot(a_ref[...], w_ref[...])
    if has_bias:
        y = y + b_ref[...]
    o_ref[...] = x_ref[...] + g_ref[...] * _rms(y, nw_ref[...])


def _proj_out(a_bf, w_bf, bias, nw, mods, layer, k_gate, x, tm):
    k = a_bf.shape[1]
    has_bias = bias is not None
    in_specs = [
        pl.BlockSpec((tm, k), lambda i: (i, 0)),
        pl.BlockSpec((k, D), lambda i: (0, 0)),
    ]
    args = [a_bf, w_bf]
    if has_bias:
        in_specs.append(pl.BlockSpec((1, D), lambda i: (0, 0)))
        args.append(bias.reshape(1, D))
    in_specs += [
        pl.BlockSpec((1, D), lambda i: (0, 0)),
        pl.BlockSpec((None, 1, D), _mod_index(layer, k_gate, tm)),
        pl.BlockSpec((tm, D), lambda i: (i, 0)),
    ]
    args += [nw.reshape(1, D), mods, x]
    return pl.pallas_call(
        functools.partial(_out_kernel, has_bias),
        grid=(M // tm,),
        in_specs=in_specs,
        out_specs=pl.BlockSpec((tm, D), lambda i: (i, 0)),
        out_shape=jax.ShapeDtypeStruct((M, D), F32),
        compiler_params=_cp(("arbitrary",)),
        name="proj_out",
    )(*args)


def _seq_edges(i):
    n_p = MP // TROW
    per_b = LS // TROW
    pos = (i - n_p) % per_b
    first = jnp.logical_or(i < n_p, pos == 0)
    last = jnp.logical_or(i < n_p, pos == per_b - 1)
    return first, last


def _conv3(main_ref, prev_ref, next_ref, w_ref, b_ref, first, last):
    u = main_ref[...]
    rows = lax.broadcasted_iota(jnp.int32, u.shape, 0)
    pr = jnp.where(first, 0.0, prev_ref[SUB - 1:SUB, :])
    nx = jnp.where(last, 0.0, next_ref[0:1, :])
    up = jnp.where(rows == 0, pr, pltpu.roll(u, 1, axis=0))
    dn = jnp.where(rows == TROW - 1, nx, pltpu.roll(u, TROW - 1, axis=0))
    return up * w_ref[0:1, :] + u * w_ref[1:2, :] + dn * w_ref[2:3, :] + b_ref[...]


def _halo_specs(tc, cblk):
    r = TROW // SUB
    last_blk = M // SUB - 1
    return [
        pl.BlockSpec((TROW, tc), lambda i: (i, cblk)),
        pl.BlockSpec((SUB, tc), lambda i: (jnp.maximum(i * r - 1, 0), cblk)),
        pl.BlockSpec((SUB, tc), lambda i: (jnp.minimum((i + 1) * r, last_blk), cblk)),
    ]


def _hy_dw_kernel(*refs):
    first, last = _seq_edges(pl.program_id(0))
    streams = []
    for s in range(3):
        m, p, n, w, b = refs[5 * s:5 * s + 5]
        streams.append(_conv3(m, p, n, w, b, first, last))
    x0_ref, z_ref = refs[15], refs[16]
    x0_ref[...] = streams[0]
    z_ref[...] = streams[2] * streams[1]


def _hy_dw(u, w_short, b_short):
    in_specs, args = [], []
    for s in range(3):
        in_specs += _halo_specs(D, s)
        in_specs += [pl.BlockSpec((3, D), lambda i, s=s: (0, s)),
                     pl.BlockSpec((1, D), lambda i, s=s: (0, s))]
        args += [u, u, u, w_short, b_short.reshape(1, 3 * D)]
    return pl.pallas_call(
        _hy_dw_kernel,
        grid=(M // TROW,),
        in_specs=in_specs,
        out_specs=[pl.BlockSpec((TROW, D), lambda i: (i, 0))] * 2,
        out_shape=[jax.ShapeDtypeStruct((M, D), F32)] * 2,
        compiler_params=_cp(("arbitrary",)),
        name="hy_dw",
    )(*args)


def _ffn_dw_kernel(*refs):
    first, last = _seq_edges(pl.program_id(0))
    g = _conv3(*refs[0:5], first, last)
    val = _conv3(*refs[5:10], first, last)
    refs[10][...] = (_silu(g) * val).astype(BF16)


def _ffn_dw(u, w_dw, b_dw):
    in_specs, args = [], []
    for s in range(2):
        in_specs += _halo_specs(D_FF, s)
        in_specs += [pl.BlockSpec((3, D_FF), lambda i, s=s: (0, s)),
                     pl.BlockSpec((1, D_FF), lambda i, s=s: (0, s))]
        args += [u, u, u, w_dw, b_dw.reshape(1, 2 * D_FF)]
    return pl.pallas_call(
        _ffn_dw_kernel,
        grid=(M // TROW,),
        in_specs=in_specs,
        out_specs=pl.BlockSpec((TROW, D_FF), lambda i: (i, 0)),
        out_shape=jax.ShapeDtypeStruct((M, D_FF), BF16),
        compiler_params=_cp(("arbitrary",)),
        name="ffn_dw",
    )(*args)


HALO = 2 * SUB
CK = 256


def _ext_specs(tm):
    r = tm // SUB
    last_blk = M // SUB - 1
    return [
        pl.BlockSpec((tm, D), lambda i: (i, 0)),
        pl.BlockSpec((SUB, D), lambda i: (jnp.maximum(i * r - 1, 0), 0)),
        pl.BlockSpec((SUB, D), lambda i: (jnp.minimum((i + 1) * r, last_blk), 0)),
    ]


def _fill_h(tm, h_scr, x_ref, xp_ref, xn_ref, nw_ref, sh_ref, sc_ref):
    def hmod(x):
        return (_rms(x, nw_ref[...]) * (1.0 + sc_ref[...]) + sh_ref[...]).astype(BF16)
    h_scr[0:tm, :] = hmod(x_ref[...])
    h_scr[tm:tm + SUB, :] = hmod(xp_ref[...])
    h_scr[tm + SUB:tm + HALO, :] = hmod(xn_ref[...])


def _tile_flags(tm):
    i = pl.program_id(0)
    is_ctx = i < MP // tm
    lseq = jnp.where(is_ctx, LP, LS)
    starts = ((i * tm) & (lseq - 1)) == 0
    ends = (((i + 1) * tm) & (lseq - 1)) == 0
    return is_ctx, starts, ends


SLAB = 16


def _edge_slabs(tm):
    return sorted({b for b in range(0, tm, LP)} | {b + LP - SLAB for b in range(0, tm, LP)})


def _conv3_bulk(tm, u, w, b):
    return (pltpu.roll(u, 1, axis=0) * w[0:1, :] + u * w[1:2, :]
            + pltpu.roll(u, tm - 1, axis=0) * w[2:3, :] + b)


def _conv3_slab(tm, u_ext, s, w, b, flags):
    is_ctx, starts, ends = flags
    us = u_ext[s:s + SLAB, :]
    if s == 0:
        prev = jnp.where(starts, 0.0, u_ext[tm + SUB - 1:tm + SUB, :])
    else:
        prev = u_ext[s - 1:s, :]
        if s % LP == 0:
            prev = jnp.where(is_ctx, 0.0, prev)
    if s + SLAB == tm:
        nxt = jnp.where(ends, 0.0, u_ext[tm + SUB:tm + SUB + 1, :])
    else:
        nxt = u_ext[s + SLAB:s + SLAB + 1, :]
        if (s + SLAB) % LP == 0:
            nxt = jnp.where(is_ctx, 0.0, nxt)
    rows = lax.broadcasted_iota(jnp.int32, us.shape, 0)
    up = jnp.where(rows == 0, prev, pltpu.roll(us, 1, axis=0))
    dn = jnp.where(rows == SLAB - 1, nxt, pltpu.roll(us, SLAB - 1, axis=0))
    return up * w[0:1, :] + us * w[1:2, :] + dn * w[2:3, :] + b


def _ffn_kernel(tm, x_ref, xp_ref, xn_ref, nwi_ref, sh_ref, sc_ref, wup_ref, wdw_ref, bdw_ref,
                wdn_ref, nwo_ref, g_ref, o_ref, h_scr, act_scr):
    _fill_h(tm, h_scr, x_ref, xp_ref, xn_ref, nwi_ref, sh_ref, sc_ref)
    flags = _tile_flags(tm)
    h = h_scr[...]
    for c in range(D_FF // CK):
        cg = slice(c * CK, (c + 1) * CK)
        cv = slice(D_FF + c * CK, D_FF + (c + 1) * CK)
        g_ext = _dot(h, wup_ref[:, cg])
        v_ext = _dot(h, wup_ref[:, cv])
        wg, bg, wv, bv = wdw_ref[:, cg], bdw_ref[:, cg], wdw_ref[:, cv], bdw_ref[:, cv]
        g = _conv3_bulk(tm, g_ext[0:tm, :], wg, bg)
        val = _conv3_bulk(tm, v_ext[0:tm, :], wv, bv)
        act_scr[:, cg] = (_silu(g) * val).astype(BF16)
        for s in _edge_slabs(tm):
            g = _conv3_slab(tm, g_ext, s, wg, bg, flags)
            val = _conv3_slab(tm, v_ext, s, wv, bv, flags)
            act_scr[s:s + SLAB, cg] = (_silu(g) * val).astype(BF16)
    y = _dot(act_scr[...], wdn_ref[...])
    o_ref[...] = x_ref[...] + g_ref[...] * _rms(y, nwo_ref[...])


def _resident(shape):
    return pl.BlockSpec(shape, lambda i: (0,) * len(shape), pipeline_mode=pl.Buffered(1))


def _ffn(x, nw_in, nw_out, mods, layer, wup_bf, w_dw, b_dw, wdn_bf, tm=512):
    vec = pl.BlockSpec((1, D), lambda i: (0, 0))
    return pl.pallas_call(
        functools.partial(_ffn_kernel, tm),
        grid=(M // tm,),
        in_specs=_ext_specs(tm) + [
            vec,
            pl.BlockSpec((None, 1, D), _mod_index(layer, 3, tm)),
            pl.BlockSpec((None, 1, D), _mod_index(layer, 4, tm)),
            _resident((D, 2 * D_FF)),
            _resident((3, 2 * D_FF)),
            _resident((1, 2 * D_FF)),
            _resident((D_FF, D)),
            vec,
            pl.BlockSpec((None, 1, D), _mod_index(layer, 5, tm)),
        ],
        out_specs=pl.BlockSpec((tm, D), lambda i: (i, 0)),
        out_shape=jax.ShapeDtypeStruct((M, D), F32),
        scratch_shapes=[pltpu.VMEM((tm + HALO, D), BF16), pltpu.VMEM((tm, D_FF), BF16)],
        compiler_params=_cp(("arbitrary",)),
        name="ffn",
    )(x, x, x, nw_in.reshape(1, D), mods, mods, wup_bf, w_dw, b_dw.reshape(1, 2 * D_FF), wdn_bf,
      nw_out.reshape(1, D), mods)


def _hy_in_kernel(tm, x_ref, xp_ref, xn_ref, nw_ref, sh_ref, sc_ref, w_ref, b_ref, ws_ref, bs_ref,
                  x0_ref, z_ref, h_scr):
    _fill_h(tm, h_scr, x_ref, xp_ref, xn_ref, nw_ref, sh_ref, sc_ref)
    flags = _tile_flags(tm)
    h = h_scr[...]
    for c in range(D // CK):
        cc = slice(c * CK, (c + 1) * CK)
        u_ext, ws, bs = [], [], []
        for s in range(3):
            cs = slice(s * D + c * CK, s * D + (c + 1) * CK)
            u_ext.append(_dot(h, w_ref[:, cs]) + b_ref[:, cs])
            ws.append(ws_ref[:, cs])
            bs.append(bs_ref[:, cs])
        out = [_conv3_bulk(tm, u_ext[s][0:tm, :], ws[s], bs[s]) for s in range(3)]
        x0_ref[:, cc] = out[0].astype(BF16)
        z_ref[:, cc] = out[2] * out[1]
        for r in _edge_slabs(tm):
            out = [_conv3_slab(tm, u_ext[s], r, ws[s], bs[s], flags) for s in range(3)]
            x0_ref[r:r + SLAB, cc] = out[0].astype(BF16)
            z_ref[r:r + SLAB, cc] = out[2] * out[1]


def _hy_in(x, nw, mods, layer, w_bf, b_in, w_short, b_short, tm=512):
    return pl.pallas_call(
        functools.partial(_hy_in_kernel, tm),
        grid=(M // tm,),
        in_specs=_ext_specs(tm) + [
            pl.BlockSpec((1, D), lambda i: (0, 0)),
            pl.BlockSpec((None, 1, D), _mod_index(layer, 0, tm)),
            pl.BlockSpec((None, 1, D), _mod_index(layer, 1, tm)),
            _resident((D, 3 * D)),
            _resident((1, 3 * D)),
            _resident((3, 3 * D)),
            _resident((1, 3 * D)),
        ],
        out_specs=[pl.BlockSpec((tm, D), lambda i: (i, 0))] * 2,
        out_shape=[jax.ShapeDtypeStruct((M, D), BF16), jax.ShapeDtypeStruct((M, D), F32)],
        scratch_shapes=[pltpu.VMEM((tm + HALO, D), BF16)],
        compiler_params=_cp(("arbitrary",)),
        name="hy_in",
    )(x, x, x, nw.reshape(1, D), mods, mods, w_bf, b_in.reshape(1, 3 * D), w_short,
      b_short.reshape(1, 3 * D))


def _dft_mats():
    n = np.arange(CB, dtype=np.float64)
    f = np.arange(CB, dtype=np.float64)
    ang = 2.0 * np.pi * np.outer(f, n) / (2 * CB)
    fwd = np.concatenate([np.cos(ang), -np.sin(ang)], axis=0)
    fwd[CB] = np.cos(np.pi * n)
    scale = np.full((2 * CB, 1), 2.0 / (2 * CB))
    scale[0] = scale[CB] = 1.0 / (2 * CB)
    inv = (fwd * scale).T
    sgn = np.where(np.arange(CB) % 2 == 0, 1.0, -1.0)
    sgn2 = np.concatenate([sgn, sgn])[:, None]
    sgn2[CB] = 1.0
    return (jnp.asarray(fwd, F32), jnp.asarray(inv, F32), jnp.asarray(sgn2, F32))


def _filter_features(L):
    t = np.linspace(0.0, 1.0, L)[:, None]
    w = 2.0 * np.pi * np.arange(L)[:, None] / L
    bands = np.linspace(1e-4, EMB_BANDS - 1, EMB_BANDS)
    z = np.concatenate([t, np.cos(bands * w), -np.sin(bands * w)], axis=-1)
    return t, z


def _filter_tables():
    zs, ts, ms = [], [], []
    for L in (LP, LS):
        t, z = _filter_features(L)
        z = np.pad(z, ((0, 0), (0, FILTER_HIDDEN - EMB_DIM)))
        idx = np.abs(np.arange(2 * L) - L) % L
        zs.append(z[idx])
        ts.append(t[idx])
        ms.append((np.arange(2 * L) != 0).astype(np.float64)[:, None])
    return tuple(jnp.asarray(np.concatenate(a), F32) for a in (zs, ts, ms))


def _filt_kernel(z_ref, t_ref, m_ref, w1_ref, b1_ref, fr_ref, w2_ref, b2_ref, w3_ref,
                 ad_ref, f_ref, o_ref):
    fr = fr_ref[...]
    hid = jnp.sin(fr * (_dot(z_ref[...].astype(BF16), w1_ref[...].astype(BF16)) + b1_ref[...]))
    hid = jnp.sin(fr * (_dot(hid.astype(BF16), w2_ref[...].astype(BF16)) + b2_ref[...]))
    h = _dot(hid.astype(BF16), w3_ref[...].astype(BF16))
    taps = h * jnp.exp(-t_ref[...] * ad_ref[...]) * m_ref[...]
    o_ref[...] = _dot3(_split(f_ref[...]), taps)


def _is_bwd_chunk(c):
    return jnp.logical_or(c == 0, jnp.logical_and(c >= 2, c < 2 + NB_S))


def _filter_spectra(tabs, fwd, w1, b1, freq, w2, b2, w3, absd):
    z_tab, t_tab, m_tab = tabs
    small = lambda shape: pl.BlockSpec(shape, lambda c: (0, 0))
    return pl.pallas_call(
        _filt_kernel,
        grid=(N_CHUNK,),
        in_specs=[
            pl.BlockSpec((CB, FILTER_HIDDEN), lambda c: (c, 0)),
            pl.BlockSpec((CB, 1), lambda c: (c, 0)),
            pl.BlockSpec((CB, 1), lambda c: (c, 0)),
            small((FILTER_HIDDEN, FILTER_HIDDEN)),
            small((1, FILTER_HIDDEN)),
            small((1, FILTER_HIDDEN)),
            small((FILTER_HIDDEN, FILTER_HIDDEN)),
            small((1, FILTER_HIDDEN)),
            pl.BlockSpec((FILTER_HIDDEN, D), lambda c: (0, jnp.where(_is_bwd_chunk(c), 1, 0))),
            small((1, D)),
            small((2 * CB, CB)),
        ],
        out_specs=pl.BlockSpec((None, 2 * CB, D), lambda c: (c, 0, 0)),
        out_shape=jax.ShapeDtypeStruct((N_CHUNK, 2 * CB, D), F32),
        compiler_params=_cp(("arbitrary",)),
        name="hy_filter",
    )(z_tab, t_tab, m_tab, w1, b1.reshape(1, -1), freq.reshape(1, -1), w2, b2.reshape(1, -1),
      w3, absd, fwd)


def _conv_kernel(tc, z_ref, x0_ref, a_ref, f_ref, gi_ref, sg_ref, db_ref, o_ref, u_scr):
    unit = pl.program_id(1)
    fmat = _split(f_ref[...])
    gmat = _split(gi_ref[...])
    sgn = sg_ref[...]
    row0 = lax.broadcasted_iota(jnp.int32, (CB, tc), 0) == 0
    db = db_ref[...]

    def window(c):
        g = a_ref[c] + sgn * a_ref[c - 1]
        gre, gim = g[:CB], g[CB:]
        return gre, jnp.where(row0, 0.0, gim), jnp.where(row0, gim, gre)

    def emit(blk, yre, yim):
        rows = pl.ds(blk * CB, CB)
        y = _dot3(gmat, jnp.concatenate([yre, yim], axis=0))
        zb = z_ref[rows, :]
        o_ref[rows, :] = (x0_ref[rows, :] * (y + zb * db)).astype(BF16)

    @pl.when(unit < MP // LS)
    def _():
        gre, gimz, grez = window(1)
        for s in range(LS // LP):
            u = _dot3(fmat, z_ref[pl.ds(s * CB, CB), :])
            ure, uim = u[:CB], u[CB:]
            emit(s, gre * ure - gimz * uim, grez * uim + gimz * ure)

    @pl.when(unit >= MP // LS)
    def _():
        for j in range(NB_S):
            u_scr[j] = _dot3(fmat, z_ref[pl.ds(j * CB, CB), :])
        for i in range(NB_S):
            yre = jnp.zeros((CB, tc), F32)
            yim = jnp.zeros((CB, tc), F32)
            for j in range(NB_S):
                gre, gimz, grez = window(2 + NB_S + i - j)
                ure, uim = u_scr[j, :CB, :], u_scr[j, CB:, :]
                yre = yre + gre * ure - gimz * uim
                yim = yim + grez * uim + gimz * ure
            emit(i, yre, yim)


def _long_conv(z, x0, spectra, fwd, inv, sgn, d_bias, tc=256):
    return pl.pallas_call(
        functools.partial(_conv_kernel, tc),
        grid=(D // tc, M // LS),
        in_specs=[
            pl.BlockSpec((LS, tc), lambda c, u: (u, c)),
            pl.BlockSpec((LS, tc), lambda c, u: (u, c)),
            pl.BlockSpec((N_CHUNK, 2 * CB, tc), lambda c, u: (0, 0, c)),
            pl.BlockSpec((2 * CB, CB), lambda c, u: (0, 0)),
            pl.BlockSpec((CB, 2 * CB), lambda c, u: (0, 0)),
            pl.BlockSpec((2 * CB, 1), lambda c, u: (0, 0)),
            pl.BlockSpec((1, tc), lambda c, u: (0, c)),
        ],
        out_specs=pl.BlockSpec((LS, tc), lambda c, u: (u, c)),
        out_shape=jax.ShapeDtypeStruct((M, D), BF16),
        scratch_shapes=[pltpu.VMEM((NB_S, 2 * CB, tc), F32)],
        compiler_params=_cp(("arbitrary", "arbitrary")),
        name="hy_conv",
    )(z, x0, spectra, fwd, inv, sgn, d_bias.reshape(1, D))


def _rope_tables(L):
    rows = L // GRID_W
    r = np.repeat(np.arange(rows, dtype=np.float64), GRID_W)
    cidx = np.tile(np.arange(GRID_W, dtype=np.float64), rows)
    inv = ROPE_THETA ** (-np.arange(ROT_FREQS, dtype=np.float64) / ROT_FREQS)
    ar = r[:, None] * inv
    ac = cidx[:, None] * inv
    cos = np.concatenate([np.cos(ar), np.cos(ar), np.cos(ac), np.cos(ac)] * 2, axis=-1)
    sin = np.concatenate([np.sin(ar), np.sin(ar), np.sin(ac), np.sin(ac)] * 2, axis=-1)
    first_half = (np.arange(V_DIM) % (2 * ROT_FREQS)) < ROT_FREQS
    sin_a = np.where(first_half, -sin, 0.0)
    sin_b = np.where(first_half, 0.0, sin)
    return tuple(jnp.asarray(a, F32) for a in (cos, sin_a, sin_b))


def _rope(x, cos, sin_a, sin_b):
    return (x * cos + pltpu.roll(x, V_DIM - ROT_FREQS, axis=1) * sin_a
            + pltpu.roll(x, ROT_FREQS, axis=1) * sin_b)


QSCALE = HEAD_DIM ** -0.5 * math.log2(math.e)


def _lambda(lv, lam_init):
    a = jnp.exp(jnp.sum(lv[0:1, :] * lv[1:2, :], axis=-1, keepdims=True))
    b = jnp.exp(jnp.sum(lv[2:3, :] * lv[3:4, :], axis=-1, keepdims=True))
    return a - b + lam_init


def _diff_attn(q, k1, k2, v, lam, lam_init, subln):
    dn = (((1,), (1,)), ((), ()))
    s1 = lax.dot_general(q[:, :HEAD_DIM].astype(BF16), k1, dn, preferred_element_type=F32)
    s2 = lax.dot_general(q[:, HEAD_DIM:].astype(BF16), k2, dn, preferred_element_type=F32)
    e1 = jnp.exp2(s1 - jnp.max(s1, axis=-1, keepdims=True))
    e2 = jnp.exp2(s2 - jnp.max(s2, axis=-1, keepdims=True))
    r1 = 1.0 / jnp.sum(e1, axis=-1, keepdims=True)
    r2 = lam / jnp.sum(e2, axis=-1, keepdims=True)
    p = e1 * r1 - e2 * r2
    o = _dot(p.astype(BF16), v)
    return _rms(o, subln) * (1.0 - lam_init)


def _attn_p_kernel(lam_init, q_ref, k_ref, v_ref, lv_ref, sub_ref, o_ref, nk_ref, nv_ref):
    lam = _lambda(lv_ref[...], lam_init)
    for h in range(N_HEADS):
        hs = slice(h * V_DIM, (h + 1) * V_DIM)
        k = k_ref[:, hs]
        v = v_ref[:, hs]
        nk_ref[h] = k
        nv_ref[h] = v
        o = _diff_attn(q_ref[:, hs] * QSCALE, k[:, :HEAD_DIM].astype(BF16),
                       k[:, HEAD_DIM:].astype(BF16), v.astype(BF16), lam, lam_init, sub_ref[...])
        o_ref[:, hs] = o.astype(BF16)


def _attn_prompt(qkv, lamv, subln, lam_init):
    part = lambda p: pl.BlockSpec((LP, D), lambda b: (b, p))
    cache = pl.BlockSpec((None, N_HEADS, LP, V_DIM), lambda b: (b, 0, 0, 0))
    return pl.pallas_call(
        functools.partial(_attn_p_kernel, lam_init),
        grid=(BP,),
        in_specs=[part(0), part(1), part(2),
                  pl.BlockSpec((4, HEAD_DIM), lambda b: (0, 0)),
                  pl.BlockSpec((1, V_DIM), lambda b: (0, 0))],
        out_specs=[pl.BlockSpec((LP, D), lambda b: (b, 0)), cache, cache],
        out_shape=[jax.ShapeDtypeStruct((MP, D), BF16),
                   jax.ShapeDtypeStruct((BP, N_HEADS, LP, V_DIM), F32),
                   jax.ShapeDtypeStruct((BP, N_HEADS, LP, V_DIM), F32)],
        compiler_params=_cp(("arbitrary",)),
        name="attn_ctx",
    )(qkv, qkv, qkv, lamv, subln)


def _attn_s_kernel(lam_init, q_ref, k_ref, v_ref, ck_ref, cv_ref, cosq_ref, saq_ref, sbq_ref,
                   cosk_ref, sak_ref, sbk_ref, lv_ref, sub_ref, o_ref, k1_scr, k2_scr, v_scr):
    @pl.when(pl.program_id(2) == 0)
    def _():
        kr = _rope(k_ref[...], cosk_ref[...], sak_ref[...], sbk_ref[...])
        ck = ck_ref[...]
        k1_scr[0:PAST, :] = ck[:, :HEAD_DIM].astype(BF16)
        k2_scr[0:PAST, :] = ck[:, HEAD_DIM:].astype(BF16)
        k1_scr[PAST:, :] = kr[:, :HEAD_DIM].astype(BF16)
        k2_scr[PAST:, :] = kr[:, HEAD_DIM:].astype(BF16)
        v_scr[0:PAST, :] = cv_ref[...].astype(BF16)
        v_scr[PAST:, :] = v_ref[...].astype(BF16)

    lam = _lambda(lv_ref[...], lam_init)
    q = _rope(q_ref[...], cosq_ref[...], saq_ref[...], sbq_ref[...]) * QSCALE
    o = _diff_attn(q, k1_scr[...], k2_scr[...], v_scr[...], lam, lam_init, sub_ref[...])
    o_ref[...] = o.astype(BF16)


def _attn_sample(qkv, cache_k, cache_v, j, ropes, lamv, subln, lam_init, tq=512):
    nq = LS // tq
    row0 = MP // tq
    cos, sin_a, sin_b = ropes
    seq = lambda off: pl.BlockSpec((LS, V_DIM), lambda b, h, q: (MP // LS + b, off + h))
    ctx = pl.BlockSpec((None, None, None, PAST, V_DIM), lambda b, h, q: (b, j, h, 0, 0))
    tab_q = pl.BlockSpec((tq, V_DIM), lambda b, h, q: (q, 0))
    tab_k = pl.BlockSpec((LS, V_DIM), lambda b, h, q: (0, 0))
    return pl.pallas_call(
        functools.partial(_attn_s_kernel, lam_init),
        grid=(BS, N_HEADS, nq),
        in_specs=[pl.BlockSpec((tq, V_DIM), lambda b, h, q: (row0 + b * nq + q, h)),
                  seq(N_HEADS), seq(2 * N_HEADS), ctx, ctx,
                  tab_q, tab_q, tab_q, tab_k, tab_k, tab_k,
                  pl.BlockSpec((4, HEAD_DIM), lambda b, h, q: (0, 0)),
                  pl.BlockSpec((1, V_DIM), lambda b, h, q: (0, 0))],
        out_specs=pl.BlockSpec((tq, V_DIM), lambda b, h, q: (b * nq + q, h)),
        out_shape=jax.ShapeDtypeStruct((MS, D), BF16),
        scratch_shapes=[pltpu.VMEM((PAST + LS, HEAD_DIM), BF16),
                        pltpu.VMEM((PAST + LS, HEAD_DIM), BF16),
                        pltpu.VMEM((PAST + LS, V_DIM), BF16)],
        compiler_params=_cp(("arbitrary", "arbitrary", "arbitrary")),
        name="attn_lat",
    )(qkv, qkv, qkv, cache_k, cache_v, cos, sin_a, sin_b, cos, sin_a, sin_b, lamv, subln)


def kernel(x_prompt, x_sample, cache_k, cache_v, c, c_ctx, w_ada, b_ada, norm_w, hy_w_in, hy_b_in, hy_w_short, hy_b_short, hy_f_w1, hy_f_b1, hy_f_freq, hy_f_w2, hy_f_b2, hy_f_w3, hy_d_bias, hy_w_out, hy_b_out, at_w_qkv, at_w_out, at_lambda_q1, at_lambda_k1, at_lambda_q2, at_lambda_k2, at_subln, ffn_w_up, ffn_w_dw, ffn_b_dw, ffn_w_down):
    x = jnp.concatenate([x_prompt.reshape(MP, D), x_sample.reshape(MS, D)], axis=0)
    cond8 = jnp.concatenate([c_ctx[None, :], c, jnp.zeros((SUB - 1 - BS, D), F32)], axis=0)
    mods = _ada(cond8, w_ada, b_ada)

    fwd, inv, sgn = _dft_mats()
    tabs = _filter_tables()
    min_decay = math.log(DECAY_TARGET) / DECAY_PCT_LONG
    max_decay = math.log(DECAY_TARGET) / DECAY_PCT_SHORT
    absd = jnp.asarray(np.abs(np.linspace(min_decay, max_decay, D))[None, :], F32)
    ropes = _rope_tables(LS)
    w1_pad = jnp.pad(hy_f_w1, ((0, 0), (0, FILTER_HIDDEN - EMB_DIM), (0, 0)))

    new_k, new_v = [], []
    for i in range(DEPTH):
        j = i // 2
        if i % 2 == 0:
            x0, z = _hy_in(x, norm_w[i, 0], mods, i, hy_w_in[j].astype(BF16), hy_b_in[j],
                           hy_w_short[j], hy_b_short[j])
            spectra = _filter_spectra(tabs, fwd, w1_pad[j], hy_f_b1[j], hy_f_freq[j],
                                      hy_f_w2[j], hy_f_b2[j], hy_f_w3[j], absd)
            a = _long_conv(z, x0, spectra, fwd, inv, sgn, hy_d_bias[j])
            x = _proj_out(a, hy_w_out[j].astype(BF16), hy_b_out[j], norm_w[i, 1], mods, i, 2,
                          x, tm=512)
        else:
            lam_init = 0.8 - 0.6 * math.exp(-0.3 * i)
            lamv = jnp.stack([at_lambda_q1[j], at_lambda_k1[j], at_lambda_q2[j],
                              at_lambda_k2[j]], axis=0)
            subln = at_subln[j].reshape(1, V_DIM)
            qkv = _proj_in(x, norm_w[i, 0], mods, i, 0, at_w_qkv[j].astype(BF16), None,
                           tm=512, tn=1024)
            o_p, nk, nv = _attn_prompt(qkv, lamv, subln, lam_init)
            o_s = _attn_sample(qkv, cache_k, cache_v, j, ropes, lamv, subln, lam_init)
            new_k.append(nk)
            new_v.append(nv)
            a = jnp.concatenate([o_p, o_s], axis=0)
            x = _proj_out(a, at_w_out[j].astype(BF16), None, norm_w[i, 1], mods, i, 2, x, tm=512)
        x = _ffn(x, norm_w[i, 2], norm_w[i, 3], mods, i, ffn_w_up[i].astype(BF16), ffn_w_dw[i],
                 ffn_b_dw[i], ffn_w_down[i].astype(BF16))

    y_prompt = x[:MP].reshape(BP, LP, D)
    y_sample = x[MP:].reshape(BS, LS, D)
    return (y_prompt, y_sample, jnp.stack(new_k, axis=1), jnp.stack(new_v, axis=1))
```

```python
import functools
import math

import numpy as np
import jax
import jax.numpy as jnp
from jax import lax
from jax.experimental import pallas as pl
from jax.experimental.pallas import tpu as pltpu

D = 1024
BP, LP = 16, 256
BS, LS = 2, 2048
MP = BP * LP
MS = BS * LS
M = MP + MS
DEPTH = 4
GRID_W = 64
N_HEADS = 8
HEAD_DIM = 64
V_DIM = 128
ROPE_THETA = 10000.0
ROT_FREQS = 16
EMB_BANDS = 16
EMB_DIM = 33
FILTER_HIDDEN = 64
DECAY_TARGET = 1e-2
DECAY_PCT_SHORT = 0.3
DECAY_PCT_LONG = 1.5
D_FF = 2816
EPS = 1e-6
PAST = 256

CB = 256
NB_S = LS // CB
N_CHUNK = 2 + 2 * NB_S
SUB = 8
HALO = 2 * SUB
SLAB = 16
CK = 256
TM = 512
VMEM_LIMIT = 56 * 1024 * 1024
QSCALE = HEAD_DIM ** -0.5 * math.log2(math.e)

F32 = jnp.float32
BF16 = jnp.bfloat16


def _dot(a, b):
    return jnp.dot(a, b, preferred_element_type=F32)


def _split(x):
    hi = x.astype(BF16)
    lo = (x - hi.astype(F32)).astype(BF16)
    return hi, lo


def _dot3(a, b):
    ah, al = a
    bh, bl = _split(b)
    return _dot(ah, bh) + _dot(al, bh) + _dot(ah, bl)


def _rms(x, w):
    ms = jnp.mean(x * x, axis=-1, keepdims=True)
    return x * lax.rsqrt(ms + EPS) * w


def _silu(x):
    return x / (1.0 + jnp.exp(-x))


def _cp(sem, vmem=VMEM_LIMIT):
    return pltpu.CompilerParams(dimension_semantics=sem, vmem_limit_bytes=vmem)


def _resident(shape):
    return pl.BlockSpec(shape, lambda i: (0,) * len(shape), pipeline_mode=pl.Buffered(1))


N_CTX_TILES = MP // TM


def _ctx_tile(i):
    return jnp.minimum(i, N_CTX_TILES - 1)


def _lat_tile(i):
    return jnp.maximum(i - N_CTX_TILES, 0)


def _joint_tile(i):
    return i


def _is_ctx():
    return pl.program_id(0) < N_CTX_TILES


def _tile_spec(width, tile_of):
    return pl.BlockSpec((TM, width), lambda i: (tile_of(i), 0))


def _ext_specs(n_rows, tile_of):
    r = TM // SUB
    last_blk = n_rows // SUB - 1
    return [
        pl.BlockSpec((TM, D), lambda i: (tile_of(i), 0)),
        pl.BlockSpec((SUB, D), lambda i: (jnp.maximum(tile_of(i) * r - 1, 0), 0)),
        pl.BlockSpec((SUB, D), lambda i: (jnp.minimum((tile_of(i) + 1) * r, last_blk), 0)),
    ]


def _per_part(n_parts, fn, refs):
    if n_parts == 1:
        fn(*refs)
        return
    k = len(refs) // 2
    pl.when(_is_ctx())(lambda: fn(*refs[:k]))
    pl.when(jnp.logical_not(_is_ctx()))(lambda: fn(*refs[k:]))


def _ada_kernel(c_ref, w_ref, b_ref, o_ref):
    s = _silu(c_ref[...]).astype(BF16)
    o_ref[...] = _dot(s, w_ref[...].astype(BF16)) + b_ref[...]


def _ada(cond8, w_ada, b_ada):
    out = pl.pallas_call(
        _ada_kernel,
        grid=(DEPTH, 6),
        in_specs=[
            pl.BlockSpec((SUB, D), lambda l, k: (0, 0)),
            pl.BlockSpec((None, D, D), lambda l, k: (l, 0, k)),
            pl.BlockSpec((None, 1, D), lambda l, k: (l, 0, k)),
        ],
        out_specs=pl.BlockSpec((None, SUB, D), lambda l, k: (l * 6 + k, 0, 0)),
        out_shape=jax.ShapeDtypeStruct((DEPTH * 6, SUB, D), F32),
        compiler_params=_cp(("arbitrary", "arbitrary")),
        name="ada",
    )(cond8, w_ada, b_ada.reshape(DEPTH, 1, 6 * D))
    return out.reshape(DEPTH * 6 * SUB, 1, D)


def _mod_spec(layer, which):
    base = (layer * 6 + which) * SUB
    per_b = LS // TM

    def imap(i):
        r = jnp.where(i < N_CTX_TILES, 0, 1 + (i - N_CTX_TILES) // per_b)
        return (base + r, 0, 0)

    return pl.BlockSpec((None, 1, D), imap)


_VEC = pl.BlockSpec((1, D), lambda i: (0, 0))


def _hmod(x, nw_ref, sh_ref, sc_ref):
    return (_rms(x, nw_ref[...]) * (1.0 + sc_ref[...]) + sh_ref[...]).astype(BF16)


def _fill_h(h_scr, nw_ref, sh_ref, sc_ref, x_ref, xp_ref, xn_ref):
    h_scr[0:TM, :] = _hmod(x_ref[...], nw_ref, sh_ref, sc_ref)
    h_scr[TM:TM + SUB, :] = _hmod(xp_ref[...], nw_ref, sh_ref, sc_ref)
    h_scr[TM + SUB:TM + HALO, :] = _hmod(xn_ref[...], nw_ref, sh_ref, sc_ref)


def _tile_flags():
    i = pl.program_id(0)
    is_ctx = _is_ctx()
    lseq = jnp.where(is_ctx, LP, LS)
    starts = ((i * TM) & (lseq - 1)) == 0
    ends = (((i + 1) * TM) & (lseq - 1)) == 0
    return is_ctx, starts, ends


def _edge_slabs():
    return sorted({b for b in range(0, TM, LP)} | {b + LP - SLAB for b in range(0, TM, LP)})


def _conv3_bulk(u, w, b):
    return (pltpu.roll(u, 1, axis=0) * w[0:1, :] + u * w[1:2, :]
            + pltpu.roll(u, TM - 1, axis=0) * w[2:3, :] + b)


def _conv3_slab(u_ext, s, w, b, flags):
    is_ctx, starts, ends = flags
    us = u_ext[s:s + SLAB, :]
    if s == 0:
        prev = jnp.where(starts, 0.0, u_ext[TM + SUB - 1:TM + SUB, :])
    else:
        prev = u_ext[s - 1:s, :]
        if s % LP == 0:
            prev = jnp.where(is_ctx, 0.0, prev)
    if s + SLAB == TM:
        nxt = jnp.where(ends, 0.0, u_ext[TM + SUB:TM + SUB + 1, :])
    else:
        nxt = u_ext[s + SLAB:s + SLAB + 1, :]
        if (s + SLAB) % LP == 0:
            nxt = jnp.where(is_ctx, 0.0, nxt)
    rows = lax.broadcasted_iota(jnp.int32, us.shape, 0)
    up = jnp.where(rows == 0, prev, pltpu.roll(us, 1, axis=0))
    dn = jnp.where(rows == SLAB - 1, nxt, pltpu.roll(us, SLAB - 1, axis=0))
    return up * w[0:1, :] + us * w[1:2, :] + dn * w[2:3, :] + b


def _ffn_kernel(split_out, x_ref, xp_ref, xn_ref, nwi_ref, sh_ref, sc_ref, wup_ref, wdw_ref,
                bdw_ref, wdn_ref, nwo_ref, g_ref, *rest):
    h_scr, act_scr = rest[-2:]
    _fill_h(h_scr, nwi_ref, sh_ref, sc_ref, x_ref, xp_ref, xn_ref)
    flags = _tile_flags()
    h = h_scr[...]
    for c in range(D_FF // CK):
        cg = slice(c * CK, (c + 1) * CK)
        cv = slice(D_FF + c * CK, D_FF + (c + 1) * CK)
        g_ext = _dot(h, wup_ref[:, cg])
        v_ext = _dot(h, wup_ref[:, cv])
        wg, bg, wv, bv = wdw_ref[:, cg], bdw_ref[:, cg], wdw_ref[:, cv], bdw_ref[:, cv]
        g = _conv3_bulk(g_ext[0:TM, :], wg, bg)
        val = _conv3_bulk(v_ext[0:TM, :], wv, bv)
        act_scr[:, cg] = (_silu(g) * val).astype(BF16)
        for s in _edge_slabs():
            g = _conv3_slab(g_ext, s, wg, bg, flags)
            val = _conv3_slab(v_ext, s, wv, bv, flags)
            act_scr[s:s + SLAB, cg] = (_silu(g) * val).astype(BF16)
    y = _dot(act_scr[...], wdn_ref[...])
    res = x_ref[...] + g_ref[...] * _rms(y, nwo_ref[...])
    if split_out:
        yc_ref, yl_ref = rest[0:2]
        is_ctx = flags[0]

        @pl.when(is_ctx)
        def _():
            yc_ref[...] = res

        @pl.when(jnp.logical_not(is_ctx))
        def _():
            yl_ref[...] = res
    else:
        rest[0][...] = res


def _ffn(x, nw_in, nw_out, mods, layer, wup_bf, w_dw, b_dw, wdn_bf, split_out):
    if split_out:
        out_specs = [_tile_spec(D, _ctx_tile), _tile_spec(D, _lat_tile)]
        out_shape = [jax.ShapeDtypeStruct((MP, D), F32), jax.ShapeDtypeStruct((MS, D), F32)]
    else:
        out_specs = _tile_spec(D, _joint_tile)
        out_shape = jax.ShapeDtypeStruct((M, D), F32)
    return pl.pallas_call(
        functools.partial(_ffn_kernel, split_out),
        grid=(M // TM,),
        in_specs=_ext_specs(M, _joint_tile) + [
            _VEC, _mod_spec(layer, 3), _mod_spec(layer, 4),
            _resident((D, 2 * D_FF)), _resident((3, 2 * D_FF)), _resident((1, 2 * D_FF)),
            _resident((D_FF, D)),
            _VEC, _mod_spec(layer, 5),
        ],
        out_specs=out_specs,
        out_shape=out_shape,
        scratch_shapes=[pltpu.VMEM((TM + HALO, D), BF16), pltpu.VMEM((TM, D_FF), BF16)],
        compiler_params=_cp(("arbitrary",)),
        name="ffn",
    )(x, x, x, nw_in.reshape(1, D), mods, mods, wup_bf, w_dw, b_dw.reshape(1, 2 * D_FF), wdn_bf,
      nw_out.reshape(1, D), mods)


def _hy_in_kernel(n_parts, *refs):
    x_refs = refs[:3 * n_parts]
    (nw_ref, sh_ref, sc_ref, w_ref, b_ref, ws_ref, bs_ref, x0_ref, z_ref,
     h_scr) = refs[3 * n_parts:]
    _per_part(n_parts, functools.partial(_fill_h, h_scr, nw_ref, sh_ref, sc_ref), x_refs)
    flags = _tile_flags()
    h = h_scr[...]
    for c in range(D // CK):
        cc = slice(c * CK, (c + 1) * CK)
        u_ext, ws, bs = [], [], []
        for s in range(3):
            cs = slice(s * D + c * CK, s * D + (c + 1) * CK)
            u_ext.append(_dot(h, w_ref[:, cs]) + b_ref[:, cs])
            ws.append(ws_ref[:, cs])
            bs.append(bs_ref[:, cs])
        out = [_conv3_bulk(u_ext[s][0:TM, :], ws[s], bs[s]) for s in range(3)]
        x0_ref[:, cc] = out[0].astype(BF16)
        z_ref[:, cc] = out[2] * out[1]
        for r in _edge_slabs():
            out = [_conv3_slab(u_ext[s], r, ws[s], bs[s], flags) for s in range(3)]
            x0_ref[r:r + SLAB, cc] = out[0].astype(BF16)
            z_ref[r:r + SLAB, cc] = out[2] * out[1]


def _hy_in(x_parts, nw, mods, layer, w_bf, b_in, w_short, b_short):
    if len(x_parts) == 1:
        x_specs = _ext_specs(M, _joint_tile)
    else:
        x_specs = _ext_specs(MP, _ctx_tile) + _ext_specs(MS, _lat_tile)
    x_args = [a for a in x_parts for _ in range(3)]
    return pl.pallas_call(
        functools.partial(_hy_in_kernel, len(x_parts)),
        grid=(M // TM,),
        in_specs=x_specs + [
            _VEC, _mod_spec(layer, 0), _mod_spec(layer, 1),
            _resident((D, 3 * D)), _resident((1, 3 * D)), _resident((3, 3 * D)),
            _resident((1, 3 * D)),
        ],
        out_specs=[_tile_spec(D, _joint_tile)] * 2,
        out_shape=[jax.ShapeDtypeStruct((M, D), BF16), jax.ShapeDtypeStruct((M, D), F32)],
        scratch_shapes=[pltpu.VMEM((TM + HALO, D), BF16)],
        compiler_params=_cp(("arbitrary",)),
        name="hy_in",
    )(*x_args, nw.reshape(1, D), mods, mods, w_bf, b_in.reshape(1, 3 * D), w_short,
      b_short.reshape(1, 3 * D))


def _out_kernel(n_a, n_x, has_bias, *refs):
    a_refs, x_refs = refs[:n_a], refs[n_a:n_a + n_x]
    rest = refs[n_a + n_x:]
    if has_bias:
        w_ref, b_ref, nw_ref, g_ref, o_ref = rest
    else:
        w_ref, nw_ref, g_ref, o_ref = rest

    def mixed(a_ref):
        y = _dot(a_ref[...], w_ref[...])
        if has_bias:
            y = y + b_ref[...]
        o_ref[...] = g_ref[...] * _rms(y, nw_ref[...])

    def residual(x_ref):
        o_ref[...] = o_ref[...] + x_ref[...]

    _per_part(n_a, mixed, a_refs)
    _per_part(n_x, residual, x_refs)


def _proj_out(a_parts, x_parts, w_bf, bias, nw, mods, layer):
    k = w_bf.shape[0]
    has_bias = bias is not None

    def specs(parts, width):
        if len(parts) == 1:
            return [_tile_spec(width, _joint_tile)]
        return [_tile_spec(width, _ctx_tile), _tile_spec(width, _lat_tile)]

    in_specs = specs(a_parts, k) + specs(x_parts, D) + [_resident((k, D))]
    args = list(a_parts) + list(x_parts) + [w_bf]
    if has_bias:
        in_specs.append(_VEC)
        args.append(bias.reshape(1, D))
    in_specs += [_VEC, _mod_spec(layer, 2)]
    args += [nw.reshape(1, D), mods]
    return pl.pallas_call(
        functools.partial(_out_kernel, len(a_parts), len(x_parts), has_bias),
        grid=(M // TM,),
        in_specs=in_specs,
        out_specs=_tile_spec(D, _joint_tile),
        out_shape=jax.ShapeDtypeStruct((M, D), F32),
        compiler_params=_cp(("arbitrary",)),
        name="proj_out",
    )(*args)


def _dft_mats():
    n = np.arange(CB, dtype=np.float64)
    f = np.arange(CB, dtype=np.float64)
    ang = 2.0 * np.pi * np.outer(f, n) / (2 * CB)
    fwd = np.concatenate([np.cos(ang), -np.sin(ang)], axis=0)
    fwd[CB] = np.cos(np.pi * n)
    scale = np.full((2 * CB, 1), 2.0 / (2 * CB))
    scale[0] = scale[CB] = 1.0 / (2 * CB)
    inv = (fwd * scale).T
    sgn = np.where(np.arange(CB) % 2 == 0, 1.0, -1.0)
    sgn2 = np.concatenate([sgn, sgn])[:, None]
    sgn2[CB] = 1.0
    return (jnp.asarray(fwd, F32), jnp.asarray(inv, F32), jnp.asarray(sgn2, F32))


def _filter_features(L):
    t = np.linspace(0.0, 1.0, L)[:, None]
    w = 2.0 * np.pi * np.arange(L)[:, None] / L
    bands = np.linspace(1e-4, EMB_BANDS - 1, EMB_BANDS)
    z = np.concatenate([t, np.cos(bands * w), -np.sin(bands * w)], axis=-1)
    return t, z


def _filter_tables():
    zs, ts, ms = [], [], []
    for L in (LP, LS):
        t, z = _filter_features(L)
        z = np.pad(z, ((0, 0), (0, FILTER_HIDDEN - EMB_DIM)))
        idx = np.abs(np.arange(2 * L) - L) % L
        zs.append(z[idx])
        ts.append(t[idx])
        ms.append((np.arange(2 * L) != 0).astype(np.float64)[:, None])
    return tuple(jnp.asarray(np.concatenate(a), F32) for a in (zs, ts, ms))


def _filt_kernel(z_ref, t_ref, m_ref, w1_ref, b1_ref, fr_ref, w2_ref, b2_ref, w3_ref,
                 ad_ref, f_ref, o_ref):
    fr = fr_ref[...]
    hid = jnp.sin(fr * (_dot(z_ref[...].astype(BF16), w1_ref[...].astype(BF16)) + b1_ref[...]))
    hid = jnp.sin(fr * (_dot(hid.astype(BF16), w2_ref[...].astype(BF16)) + b2_ref[...]))
    h = _dot(hid.astype(BF16), w3_ref[...].astype(BF16))
    taps = h * jnp.exp(-t_ref[...] * ad_ref[...]) * m_ref[...]
    o_ref[...] = _dot3(_split(f_ref[...]), taps)


def _is_bwd_chunk(c):
    return jnp.logical_or(c == 0, jnp.logical_and(c >= 2, c < 2 + NB_S))


def _filter_spectra(tabs, fwd, w1, b1, freq, w2, b2, w3, absd):
    z_tab, t_tab, m_tab = tabs
    small = lambda shape: pl.BlockSpec(shape, lambda c: (0, 0))
    return pl.pallas_call(
        _filt_kernel,
        grid=(N_CHUNK,),
        in_specs=[
            pl.BlockSpec((CB, FILTER_HIDDEN), lambda c: (c, 0)),
            pl.BlockSpec((CB, 1), lambda c: (c, 0)),
            pl.BlockSpec((CB, 1), lambda c: (c, 0)),
            small((FILTER_HIDDEN, FILTER_HIDDEN)),
            small((1, FILTER_HIDDEN)),
            small((1, FILTER_HIDDEN)),
            small((FILTER_HIDDEN, FILTER_HIDDEN)),
            small((1, FILTER_HIDDEN)),
            pl.BlockSpec((FILTER_HIDDEN, D), lambda c: (0, jnp.where(_is_bwd_chunk(c), 1, 0))),
            small((1, D)),
            small((2 * CB, CB)),
        ],
        out_specs=pl.BlockSpec((None, 2 * CB, D), lambda c: (c, 0, 0)),
        out_shape=jax.ShapeDtypeStruct((N_CHUNK, 2 * CB, D), F32),
        compiler_params=_cp(("arbitrary",)),
        name="hy_filter",
    )(z_tab, t_tab, m_tab, w1, b1.reshape(1, -1), freq.reshape(1, -1), w2, b2.reshape(1, -1),
      w3, absd, fwd)


def _conv_kernel(tc, z_ref, x0_ref, a_ref, f_ref, gi_ref, sg_ref, db_ref, o_ref, u_scr):
    unit = pl.program_id(1)
    fmat = _split(f_ref[...])
    gmat = _split(gi_ref[...])
    sgn = sg_ref[...]
    row0 = lax.broadcasted_iota(jnp.int32, (CB, tc), 0) == 0
    db = db_ref[...]

    def window(c):
        g = a_ref[c] + sgn * a_ref[c - 1]
        gre, gim = g[:CB], g[CB:]
        return gre, jnp.where(row0, 0.0, gim), jnp.where(row0, gim, gre)

    def emit(blk, yre, yim):
        rows = pl.ds(blk * CB, CB)
        y = _dot3(gmat, jnp.concatenate([yre, yim], axis=0))
        zb = z_ref[rows, :]
        o_ref[rows, :] = (x0_ref[rows, :] * (y + zb * db)).astype(BF16)

    @pl.when(unit < MP // LS)
    def _():
        gre, gimz, grez = window(1)
        for s in range(LS // LP):
            u = _dot3(fmat, z_ref[pl.ds(s * CB, CB), :])
            ure, uim = u[:CB], u[CB:]
            emit(s, gre * ure - gimz * uim, grez * uim + gimz * ure)

    @pl.when(unit >= MP // LS)
    def _():
        for j in range(NB_S):
            u_scr[j] = _dot3(fmat, z_ref[pl.ds(j * CB, CB), :])
        for i in range(NB_S):
            yre = jnp.zeros((CB, tc), F32)
            yim = jnp.zeros((CB, tc), F32)
            for j in range(NB_S):
                gre, gimz, grez = window(2 + NB_S + i - j)
                ure, uim = u_scr[j, :CB, :], u_scr[j, CB:, :]
                yre = yre + gre * ure - gimz * uim
                yim = yim + grez * uim + gimz * ure
            emit(i, yre, yim)


def _long_conv(z, x0, spectra, fwd, inv, sgn, d_bias, tc=256):
    return pl.pallas_call(
        functools.partial(_conv_kernel, tc),
        grid=(D // tc, M // LS),
        in_specs=[
            pl.BlockSpec((LS, tc), lambda c, u: (u, c)),
            pl.BlockSpec((LS, tc), lambda c, u: (u, c)),
            pl.BlockSpec((N_CHUNK, 2 * CB, tc), lambda c, u: (0, 0, c)),
            pl.BlockSpec((2 * CB, CB), lambda c, u: (0, 0)),
            pl.BlockSpec((CB, 2 * CB), lambda c, u: (0, 0)),
            pl.BlockSpec((2 * CB, 1), lambda c, u: (0, 0)),
            pl.BlockSpec((1, tc), lambda c, u: (0, c)),
        ],
        out_specs=pl.BlockSpec((LS, tc), lambda c, u: (u, c)),
        out_shape=jax.ShapeDtypeStruct((M, D), BF16),
        scratch_shapes=[pltpu.VMEM((NB_S, 2 * CB, tc), F32)],
        compiler_params=_cp(("arbitrary", "arbitrary")),
        name="hy_conv",
    )(z, x0, spectra, fwd, inv, sgn, d_bias.reshape(1, D))


def _rope_tables(L):
    rows = L // GRID_W
    r = np.repeat(np.arange(rows, dtype=np.float64), GRID_W)
    cidx = np.tile(np.arange(GRID_W, dtype=np.float64), rows)
    inv = ROPE_THETA ** (-np.arange(ROT_FREQS, dtype=np.float64) / ROT_FREQS)
    ar = r[:, None] * inv
    ac = cidx[:, None] * inv
    cos = np.concatenate([np.cos(ar), np.cos(ar), np.cos(ac), np.cos(ac)] * 2, axis=-1)
    sin = np.concatenate([np.sin(ar), np.sin(ar), np.sin(ac), np.sin(ac)] * 2, axis=-1)
    first_half = (np.arange(V_DIM) % (2 * ROT_FREQS)) < ROT_FREQS
    sin_a = np.where(first_half, -sin, 0.0)
    sin_b = np.where(first_half, 0.0, sin)
    return tuple(jnp.asarray(a, F32) for a in (cos, sin_a, sin_b))


def _rope(x, cos, sin_a, sin_b):
    return (x * cos + pltpu.roll(x, V_DIM - ROT_FREQS, axis=1) * sin_a
            + pltpu.roll(x, ROT_FREQS, axis=1) * sin_b)


def _qkv_kernel(x_ref, nw_ref, sh_ref, sc_ref, w_ref, cos_ref, sa_ref, sb_ref,
                qkv_ref, nk_ref, nv_ref):
    h = _hmod(x_ref[...], nw_ref, sh_ref, sc_ref)
    is_ctx = _is_ctx()
    heads_per_chunk = CK // V_DIM
    for c in range(3 * D // CK):
        part = c // (D // CK)
        head0 = (c % (D // CK)) * heads_per_chunk
        scale = QSCALE if part == 0 else 1.0
        cs = slice(c * CK, (c + 1) * CK)
        u = _dot(h, w_ref[:, cs])

        @pl.when(is_ctx)
        def _():
            qkv_ref[:, cs] = (u * scale).astype(BF16)
            cache = nk_ref if part == 1 else nv_ref
            if part > 0:
                for s in range(TM // LP):
                    for hh in range(heads_per_chunk):
                        cache[s, head0 + hh] = u[s * LP:(s + 1) * LP, hh * V_DIM:(hh + 1) * V_DIM]

        @pl.when(jnp.logical_not(is_ctx))
        def _():
            if part == 2:
                qkv_ref[:, cs] = u.astype(BF16)
            else:
                for hh in range(heads_per_chunk):
                    hs = slice(hh * V_DIM, (hh + 1) * V_DIM)
                    r = _rope(u[:, hs], cos_ref[...], sa_ref[...], sb_ref[...]) * scale
                    qkv_ref[:, c * CK + hh * V_DIM:c * CK + (hh + 1) * V_DIM] = r.astype(BF16)


def _qkv(x, nw, mods, layer, w_bf, ropes):
    per_b = LS // TM
    tab = pl.BlockSpec((TM, V_DIM), lambda i: (_lat_tile(i) % per_b, 0))
    seqs = TM // LP
    cache = pl.BlockSpec((seqs, N_HEADS, LP, V_DIM), lambda i: (_ctx_tile(i), 0, 0, 0))
    return pl.pallas_call(
        _qkv_kernel,
        grid=(M // TM,),
        in_specs=[_tile_spec(D, _joint_tile), _VEC, _mod_spec(layer, 0), _mod_spec(layer, 1),
                  _resident((D, 3 * D)), tab, tab, tab],
        out_specs=[_tile_spec(3 * D, _joint_tile), cache, cache],
        out_shape=[jax.ShapeDtypeStruct((M, 3 * D), BF16),
                   jax.ShapeDtypeStruct((BP, N_HEADS, LP, V_DIM), F32),
                   jax.ShapeDtypeStruct((BP, N_HEADS, LP, V_DIM), F32)],
        compiler_params=_cp(("arbitrary",)),
        name="qkv",
    )(x, nw.reshape(1, D), mods, mods, w_bf, *ropes)


def _lambda(lv, lam_init):
    a = jnp.exp(jnp.sum(lv[0:1, :] * lv[1:2, :], axis=-1, keepdims=True))
    b = jnp.exp(jnp.sum(lv[2:3, :] * lv[3:4, :], axis=-1, keepdims=True))
    return a - b + lam_init


def _diff_attn(q, chunks, lam, lam_init, subln):
    t = q.shape[0]
    lane = lax.broadcasted_iota(jnp.int32, q.shape, 1)
    zero = jnp.zeros_like(q)
    q2 = jnp.concatenate([jnp.where(lane < HEAD_DIM, q, zero),
                          jnp.where(lane < HEAD_DIM, zero, q)], axis=0)
    m = l = acc = None
    for k, v in chunks:
        s = lax.dot_general(q2, k, (((1,), (1,)), ((), ())), preferred_element_type=F32)
        mc = jnp.max(s, axis=-1, keepdims=True)
        m_new = mc if m is None else jnp.maximum(m, mc)
        e = jnp.exp2(s - m_new)
        lc = jnp.sum(e, axis=-1, keepdims=True)
        pv = _dot(e.astype(BF16), v)
        if m is None:
            l, acc = lc, pv
        else:
            alpha = jnp.exp2(m - m_new)
            l = alpha * l + lc
            acc = alpha * acc + pv
        m = m_new
    o = acc[:t] * (1.0 / l[:t]) - acc[t:] * (lam / l[t:])
    return _rms(o, subln) * (1.0 - lam_init)


def _attn_c_kernel(lam_init, q_ref, k_ref, v_ref, lv_ref, sub_ref, o_ref):
    lam = _lambda(lv_ref[...], lam_init)
    for h in range(N_HEADS):
        hs = slice(h * V_DIM, (h + 1) * V_DIM)
        o = _diff_attn(q_ref[:, hs], [(k_ref[:, hs], v_ref[:, hs])], lam, lam_init, sub_ref[...])
        o_ref[:, hs] = o.astype(BF16)


def _attn_ctx(qkv, lamv, subln, lam_init):
    part = lambda p: pl.BlockSpec((LP, D), lambda b: (b, p))
    return pl.pallas_call(
        functools.partial(_attn_c_kernel, lam_init),
        grid=(BP,),
        in_specs=[part(0), part(1), part(2),
                  pl.BlockSpec((4, HEAD_DIM), lambda b: (0, 0)),
                  pl.BlockSpec((1, V_DIM), lambda b: (0, 0))],
        out_specs=pl.BlockSpec((LP, D), lambda b: (b, 0)),
        out_shape=jax.ShapeDtypeStruct((MP, D), BF16),
        compiler_params=_cp(("arbitrary",)),
        name="attn_ctx",
    )(qkv, qkv, qkv, lamv, subln)


KEY_CHUNK = 512


def _attn_l_kernel(lam_init, q_ref, k_ref, v_ref, ck_ref, cv_ref, lv_ref, sub_ref, o_ref):
    chunks = [(ck_ref[...].astype(BF16), cv_ref[...].astype(BF16))]
    for c in range(LS // KEY_CHUNK):
        rows = pl.ds(c * KEY_CHUNK, KEY_CHUNK)
        chunks.append((k_ref[rows, :], v_ref[rows, :]))
    lam = _lambda(lv_ref[...], lam_init)
    o = _diff_attn(q_ref[...], chunks, lam, lam_init, sub_ref[...])
    o_ref[...] = o.astype(BF16)


def _attn_lat(qkv, cache_k, cache_v, j, lamv, subln, lam_init, tq=256):
    nq = LS // tq
    row0 = MP // tq
    seq = lambda off: pl.BlockSpec((LS, V_DIM), lambda b, h, q: (MP // LS + b, off + h))
    ctx = pl.BlockSpec((None, None, None, PAST, V_DIM), lambda b, h, q: (b, j, h, 0, 0))
    return pl.pallas_call(
        functools.partial(_attn_l_kernel, lam_init),
        grid=(BS, N_HEADS, nq),
        in_specs=[pl.BlockSpec((tq, V_DIM), lambda b, h, q: (row0 + b * nq + q, h)),
                  seq(N_HEADS), seq(2 * N_HEADS), ctx, ctx,
                  pl.BlockSpec((4, HEAD_DIM), lambda b, h, q: (0, 0)),
                  pl.BlockSpec((1, V_DIM), lambda b, h, q: (0, 0))],
        out_specs=pl.BlockSpec((tq, V_DIM), lambda b, h, q: (b * nq + q, h)),
        out_shape=jax.ShapeDtypeStruct((MS, D), BF16),
        compiler_params=_cp(("arbitrary", "arbitrary", "arbitrary")),
        name="attn_lat",
    )(qkv, qkv, qkv, cache_k, cache_v, lamv, subln)


def kernel(x_prompt, x_sample, cache_k, cache_v, c, c_ctx, w_ada, b_ada, norm_w, hy_w_in, hy_b_in, hy_w_short, hy_b_short, hy_f_w1, hy_f_b1, hy_f_freq, hy_f_w2, hy_f_b2, hy_f_w3, hy_d_bias, hy_w_out, hy_b_out, at_w_qkv, at_w_out, at_lambda_q1, at_lambda_k1, at_lambda_q2, at_lambda_k2, at_subln, ffn_w_up, ffn_w_dw, ffn_b_dw, ffn_w_down):
    cond8 = jnp.concatenate([c_ctx[None, :], c, jnp.zeros((SUB - 1 - BS, D), F32)], axis=0)
    mods = _ada(cond8, w_ada, b_ada)

    fwd, inv, sgn = _dft_mats()
    tabs = _filter_tables()
    min_decay = math.log(DECAY_TARGET) / DECAY_PCT_LONG
    max_decay = math.log(DECAY_TARGET) / DECAY_PCT_SHORT
    absd = jnp.asarray(np.abs(np.linspace(min_decay, max_decay, D))[None, :], F32)
    ropes = _rope_tables(LS)
    w1_pad = jnp.pad(hy_f_w1, ((0, 0), (0, FILTER_HIDDEN - EMB_DIM), (0, 0)))

    x_parts = [x_prompt.reshape(MP, D), x_sample.reshape(MS, D)]
    new_k, new_v = [], []
    for i in range(DEPTH):
        j = i // 2
        if i % 2 == 0:
            x0, z = _hy_in(x_parts, norm_w[i, 0], mods, i, hy_w_in[j].astype(BF16), hy_b_in[j],
                           hy_w_short[j], hy_b_short[j])
            spectra = _filter_spectra(tabs, fwd, w1_pad[j], hy_f_b1[j], hy_f_freq[j],
                                      hy_f_w2[j], hy_f_b2[j], hy_f_w3[j], absd)
            a_parts = [_long_conv(z, x0, spectra, fwd, inv, sgn, hy_d_bias[j])]
            w_out, b_out = hy_w_out[j], hy_b_out[j]
        else:
            lam_init = 0.8 - 0.6 * math.exp(-0.3 * i)
            lamv = jnp.stack([at_lambda_q1[j], at_lambda_k1[j], at_lambda_q2[j],
                              at_lambda_k2[j]], axis=0)
            subln = at_subln[j].reshape(1, V_DIM)
            qkv, nk, nv = _qkv(x_parts[0], norm_w[i, 0], mods, i, at_w_qkv[j].astype(BF16), ropes)
            new_k.append(nk)
            new_v.append(nv)
            a_parts = [_attn_ctx(qkv, lamv, subln, lam_init),
                       _attn_lat(qkv, cache_k, cache_v, j, lamv, subln, lam_init)]
            w_out, b_out = at_w_out[j], None
        x = _proj_out(a_parts, x_parts, w_out.astype(BF16), b_out, norm_w[i, 1], mods, i)
        y = _ffn(x, norm_w[i, 2], norm_w[i, 3], mods, i, ffn_w_up[i].astype(BF16), ffn_w_dw[i],
                 ffn_b_dw[i], ffn_w_down[i].astype(BF16), split_out=(i == DEPTH - 1))
        x_parts = list(y) if i == DEPTH - 1 else [y]

    y_prompt = x_parts[0].reshape(BP, LP, D)
    y_sample = x_parts[1].reshape(BS, LS, D)
    return (y_prompt, y_sample, jnp.stack(new_k, axis=1), jnp.stack(new_v, axis=1))
```

```python
import functools
import math

import numpy as np
import jax
import jax.numpy as jnp
from jax import lax
from jax.experimental import pallas as pl
from jax.experimental.pallas import tpu as pltpu

D = 1024
BP, LP = 16, 256
BS, LS = 2, 2048
MP = BP * LP
MS = BS * LS
M = MP + MS
DEPTH = 4
N_ATTN = DEPTH // 2
GRID_W = 64
N_HEADS = 8
HEAD_DIM = 64
V_DIM = 128
ROPE_THETA = 10000.0
ROT_FREQS = 16
EMB_BANDS = 16
EMB_DIM = 33
FILTER_HIDDEN = 64
DECAY_TARGET = 1e-2
DECAY_PCT_SHORT = 0.3
DECAY_PCT_LONG = 1.5
D_FF = 2816
EPS = 1e-6
PAST = 256

CB = 256
NB_S = LS // CB
N_CHUNK = 2 + 2 * NB_S
SUB = 8
HALO = 2 * SUB
SLAB = 16
CK = 256
TM = 512
KEY_CHUNK = 512
VMEM_LIMIT = 56 * 1024 * 1024
QSCALE = HEAD_DIM ** -0.5 * math.log2(math.e)

F32 = jnp.float32
BF16 = jnp.bfloat16


def _dot(a, b):
    return jnp.dot(a, b, preferred_element_type=F32)


def _split(x):
    hi = x.astype(BF16)
    lo = (x - hi.astype(F32)).astype(BF16)
    return hi, lo


def _dot3(a, b):
    ah, al = a
    bh, bl = _split(b)
    return _dot(ah, bh) + _dot(al, bh) + _dot(ah, bl)


def _rms(x, w):
    ms = jnp.mean(x * x, axis=-1, keepdims=True)
    return x * lax.rsqrt(ms + EPS) * w


def _silu(x):
    return x / (1.0 + jnp.exp(-x))


def _cp(sem, vmem=VMEM_LIMIT):
    return pltpu.CompilerParams(dimension_semantics=sem, vmem_limit_bytes=vmem)


def _lay(shape, idx):
    return pl.BlockSpec((None,) + tuple(shape), lambda i: (idx,) + (0,) * len(shape),
                        pipeline_mode=pl.Buffered(1))


def _norm_spec(layer, k):
    return _lay((1, D), layer * 4 + k)


N_CTX_TILES = MP // TM
N_LAT_TILES = MS // TM


def _ctx_tile(i):
    return jnp.minimum(i, N_CTX_TILES - 1)


def _lat_tile(i):
    return jnp.maximum(i - N_CTX_TILES, 0)


def _joint_tile(i):
    return i


def _is_ctx():
    return pl.program_id(0) < N_CTX_TILES


def _tile_spec(width, tile_of):
    return pl.BlockSpec((TM, width), lambda i: (tile_of(i), 0))


def _ext_specs(n_rows, tile_of):
    r = TM // SUB
    last_blk = n_rows // SUB - 1
    return [
        pl.BlockSpec((TM, D), lambda i: (tile_of(i), 0)),
        pl.BlockSpec((SUB, D), lambda i: (jnp.maximum(tile_of(i) * r - 1, 0), 0)),
        pl.BlockSpec((SUB, D), lambda i: (jnp.minimum((tile_of(i) + 1) * r, last_blk), 0)),
    ]


def _per_part(n_parts, fn, refs):
    if n_parts == 1:
        fn(*refs)
        return
    k = len(refs) // 2
    pl.when(_is_ctx())(lambda: fn(*refs[:k]))
    pl.when(jnp.logical_not(_is_ctx()))(lambda: fn(*refs[k:]))


def _ada_kernel(c_ref, w_ref, b_ref, o_ref):
    s = _silu(c_ref[...]).astype(BF16)
    o_ref[...] = _dot(s, w_ref[...].astype(BF16)) + b_ref[...]


def _ada(cond8, w_ada, b_ada):
    out = pl.pallas_call(
        _ada_kernel,
        grid=(DEPTH, 6),
        in_specs=[
            pl.BlockSpec((SUB, D), lambda l, k: (0, 0)),
            pl.BlockSpec((None, D, D), lambda l, k: (l, 0, k)),
            pl.BlockSpec((None, 1, D), lambda l, k: (l, 0, k)),
        ],
        out_specs=pl.BlockSpec((None, SUB, D), lambda l, k: (l * 6 + k, 0, 0)),
        out_shape=jax.ShapeDtypeStruct((DEPTH * 6, SUB, D), F32),
        compiler_params=_cp(("arbitrary", "arbitrary")),
        name="ada",
    )(cond8, w_ada, b_ada.reshape(DEPTH, 1, 6 * D))
    return out.reshape(DEPTH * 6 * SUB, 1, D)


def _mod_spec(layer, which, tile_of=_joint_tile):
    base = (layer * 6 + which) * SUB
    per_b = LS // TM

    def imap(i):
        t = tile_of(i)
        r = jnp.where(t < N_CTX_TILES, 0, 1 + (t - N_CTX_TILES) // per_b)
        return (base + r, 0, 0)

    return pl.BlockSpec((None, 1, D), imap)


def _hmod(x, nw_ref, sh_ref, sc_ref):
    return (_rms(x, nw_ref[...]) * (1.0 + sc_ref[...]) + sh_ref[...]).astype(BF16)


def _fill_h(h_scr, nw_ref, sh_ref, sc_ref, x_ref, xp_ref, xn_ref):
    h_scr[0:TM, :] = _hmod(x_ref[...], nw_ref, sh_ref, sc_ref)
    h_scr[TM:TM + SUB, :] = _hmod(xp_ref[...], nw_ref, sh_ref, sc_ref)
    h_scr[TM + SUB:TM + HALO, :] = _hmod(xn_ref[...], nw_ref, sh_ref, sc_ref)


def _tile_flags():
    i = pl.program_id(0)
    is_ctx = _is_ctx()
    lseq = jnp.where(is_ctx, LP, LS)
    starts = ((i * TM) & (lseq - 1)) == 0
    ends = (((i + 1) * TM) & (lseq - 1)) == 0
    return is_ctx, starts, ends


def _edge_slabs():
    return sorted({b for b in range(0, TM, LP)} | {b + LP - SLAB for b in range(0, TM, LP)})


def _conv3_bulk(u, w, b):
    return (pltpu.roll(u, 1, axis=0) * w[0:1, :] + u * w[1:2, :]
            + pltpu.roll(u, TM - 1, axis=0) * w[2:3, :] + b)


def _conv3_slab(u_ext, s, w, b, flags):
    is_ctx, starts, ends = flags
    us = u_ext[s:s + SLAB, :]
    if s == 0:
        prev = jnp.where(starts, 0.0, u_ext[TM + SUB - 1:TM + SUB, :])
    else:
        prev = u_ext[s - 1:s, :]
        if s % LP == 0:
            prev = jnp.where(is_ctx, 0.0, prev)
    if s + SLAB == TM:
        nxt = jnp.where(ends, 0.0, u_ext[TM + SUB:TM + SUB + 1, :])
    else:
        nxt = u_ext[s + SLAB:s + SLAB + 1, :]
        if (s + SLAB) % LP == 0:
            nxt = jnp.where(is_ctx, 0.0, nxt)
    rows = lax.broadcasted_iota(jnp.int32, us.shape, 0)
    up = jnp.where(rows == 0, prev, pltpu.roll(us, 1, axis=0))
    dn = jnp.where(rows == SLAB - 1, nxt, pltpu.roll(us, SLAB - 1, axis=0))
    return up * w[0:1, :] + us * w[1:2, :] + dn * w[2:3, :] + b


def _ffn_kernel(split_out, x_ref, xp_ref, xn_ref, nwi_ref, sh_ref, sc_ref, wup_ref, wdw_ref,
                bdw_ref, wdn_ref, nwo_ref, g_ref, *rest):
    h_scr, act_scr = rest[-2:]
    _fill_h(h_scr, nwi_ref, sh_ref, sc_ref, x_ref, xp_ref, xn_ref)
    flags = _tile_flags()
    h = h_scr[...]
    for c in range(D_FF // CK):
        cg = slice(c * CK, (c + 1) * CK)
        cv = slice(D_FF + c * CK, D_FF + (c + 1) * CK)
        g_ext = _dot(h, wup_ref[:, cg])
        v_ext = _dot(h, wup_ref[:, cv])
        wg, bg, wv, bv = wdw_ref[:, cg], bdw_ref[:, cg], wdw_ref[:, cv], bdw_ref[:, cv]
        g = _conv3_bulk(g_ext[0:TM, :], wg, bg)
        val = _conv3_bulk(v_ext[0:TM, :], wv, bv)
        act_scr[:, cg] = (_silu(g) * val).astype(BF16)
        for s in _edge_slabs():
            g = _conv3_slab(g_ext, s, wg, bg, flags)
            val = _conv3_slab(v_ext, s, wv, bv, flags)
            act_scr[s:s + SLAB, cg] = (_silu(g) * val).astype(BF16)
    y = _dot(act_scr[...], wdn_ref[...])
    res = x_ref[...] + g_ref[...] * _rms(y, nwo_ref[...])
    if split_out:
        yc_ref, yl_ref = rest[0:2]
        is_ctx = flags[0]

        @pl.when(is_ctx)
        def _():
            yc_ref[...] = res

        @pl.when(jnp.logical_not(is_ctx))
        def _():
            yl_ref[...] = res
    else:
        rest[0][...] = res


def _ffn(x, nw_all, mods, layer, wup_bf, w_dw, b_dw, wdn_bf, split_out):
    if split_out:
        out_specs = [_tile_spec(D, _ctx_tile), _tile_spec(D, _lat_tile)]
        out_shape = [jax.ShapeDtypeStruct((MP, D), F32), jax.ShapeDtypeStruct((MS, D), F32)]
    else:
        out_specs = _tile_spec(D, _joint_tile)
        out_shape = jax.ShapeDtypeStruct((M, D), F32)
    return pl.pallas_call(
        functools.partial(_ffn_kernel, split_out),
        grid=(M // TM,),
        in_specs=_ext_specs(M, _joint_tile) + [
            _norm_spec(layer, 2), _mod_spec(layer, 3), _mod_spec(layer, 4),
            _lay((D, 2 * D_FF), layer), _lay((3, 2 * D_FF), layer), _lay((1, 2 * D_FF), layer),
            _lay((D_FF, D), layer),
            _norm_spec(layer, 3), _mod_spec(layer, 5),
        ],
        out_specs=out_specs,
        out_shape=out_shape,
        scratch_shapes=[pltpu.VMEM((TM + HALO, D), BF16), pltpu.VMEM((TM, D_FF), BF16)],
        compiler_params=_cp(("arbitrary",)),
        name="ffn",
    )(x, x, x, nw_all, mods, mods, wup_bf, w_dw, b_dw, wdn_bf, nw_all, mods)


def _hy_in_kernel(n_parts, *refs):
    x_refs = refs[:3 * n_parts]
    (nw_ref, sh_ref, sc_ref, w_ref, b_ref, ws_ref, bs_ref, x0_ref, z_ref,
     h_scr) = refs[3 * n_parts:]
    _per_part(n_parts, functools.partial(_fill_h, h_scr, nw_ref, sh_ref, sc_ref), x_refs)
    flags = _tile_flags()
    h = h_scr[...]
    for c in range(D // CK):
        cc = slice(c * CK, (c + 1) * CK)
        u_ext, ws, bs = [], [], []
        for s in range(3):
            cs = slice(s * D + c * CK, s * D + (c + 1) * CK)
            u_ext.append(_dot(h, w_ref[:, cs]) + b_ref[:, cs])
            ws.append(ws_ref[:, cs])
            bs.append(bs_ref[:, cs])
        out = [_conv3_bulk(u_ext[s][0:TM, :], ws[s], bs[s]) for s in range(3)]
        x0_ref[:, cc] = out[0].astype(BF16)
        z_ref[:, cc] = out[2] * out[1]
        for r in _edge_slabs():
            out = [_conv3_slab(u_ext[s], r, ws[s], bs[s], flags) for s in range(3)]
            x0_ref[r:r + SLAB, cc] = out[0].astype(BF16)
            z_ref[r:r + SLAB, cc] = out[2] * out[1]


def _hy_in(x_parts, nw_all, mods, layer, j, w_bf, b_in, w_short, b_short):
    if len(x_parts) == 1:
        x_specs = _ext_specs(M, _joint_tile)
    else:
        x_specs = _ext_specs(MP, _ctx_tile) + _ext_specs(MS, _lat_tile)
    x_args = [a for a in x_parts for _ in range(3)]
    return pl.pallas_call(
        functools.partial(_hy_in_kernel, len(x_parts)),
        grid=(M // TM,),
        in_specs=x_specs + [
            _norm_spec(layer, 0), _mod_spec(layer, 0), _mod_spec(layer, 1),
            _lay((D, 3 * D), j), _lay((1, 3 * D), j), _lay((3, 3 * D), j), _lay((1, 3 * D), j),
        ],
        out_specs=[_tile_spec(D, _joint_tile)] * 2,
        out_shape=[jax.ShapeDtypeStruct((M, D), BF16), jax.ShapeDtypeStruct((M, D), F32)],
        scratch_shapes=[pltpu.VMEM((TM + HALO, D), BF16)],
        compiler_params=_cp(("arbitrary",)),
        name="hy_in",
    )(*x_args, nw_all, mods, mods, w_bf, b_in, w_short, b_short)


def _out_kernel(n_a, n_x, has_bias, *refs):
    a_refs, x_refs = refs[:n_a], refs[n_a:n_a + n_x]
    rest = refs[n_a + n_x:]
    if has_bias:
        w_ref, b_ref, nw_ref, g_ref, o_ref = rest
    else:
        w_ref, nw_ref, g_ref, o_ref = rest

    def mixed(a_ref):
        y = _dot(a_ref[...], w_ref[...])
        if has_bias:
            y = y + b_ref[...]
        o_ref[...] = g_ref[...] * _rms(y, nw_ref[...])

    def residual(x_ref):
        o_ref[...] = o_ref[...] + x_ref[...]

    _per_part(n_a, mixed, a_refs)
    _per_part(n_x, residual, x_refs)


def _proj_out(a_parts, x_parts, w_bf, bias, nw_all, mods, layer, j):
    k = w_bf.shape[1]
    has_bias = bias is not None

    def specs(parts, width):
        if len(parts) == 1:
            return [_tile_spec(width, _joint_tile)]
        return [_tile_spec(width, _ctx_tile), _tile_spec(width, _lat_tile)]

    in_specs = specs(a_parts, k) + specs(x_parts, D) + [_lay((k, D), j)]
    args = list(a_parts) + list(x_parts) + [w_bf]
    if has_bias:
        in_specs.append(_lay((1, D), j))
        args.append(bias)
    in_specs += [_norm_spec(layer, 1), _mod_spec(layer, 2)]
    args += [nw_all, mods]
    return pl.pallas_call(
        functools.partial(_out_kernel, len(a_parts), len(x_parts), has_bias),
        grid=(M // TM,),
        in_specs=in_specs,
        out_specs=_tile_spec(D, _joint_tile),
        out_shape=jax.ShapeDtypeStruct((M, D), F32),
        compiler_params=_cp(("arbitrary",)),
        name="proj_out",
    )(*args)


def _dft_mats():
    n = np.arange(CB, dtype=np.float64)
    f = np.arange(CB, dtype=np.float64)
    ang = 2.0 * np.pi * np.outer(f, n) / (2 * CB)
    fwd = np.concatenate([np.cos(ang), -np.sin(ang)], axis=0)
    fwd[CB] = np.cos(np.pi * n)
    scale = np.full((2 * CB, 1), 2.0 / (2 * CB))
    scale[0] = scale[CB] = 1.0 / (2 * CB)
    inv = (fwd * scale).T
    sgn = np.where(np.arange(CB) % 2 == 0, 1.0, -1.0)
    sgn2 = np.concatenate([sgn, sgn])[:, None]
    sgn2[CB] = 1.0
    return (jnp.asarray(fwd, F32), jnp.asarray(inv, F32), jnp.asarray(sgn2, F32))


def _filter_features(L):
    t = np.linspace(0.0, 1.0, L)[:, None]
    w = 2.0 * np.pi * np.arange(L)[:, None] / L
    bands = np.linspace(1e-4, EMB_BANDS - 1, EMB_BANDS)
    z = np.concatenate([t, np.cos(bands * w), -np.sin(bands * w)], axis=-1)
    return t, z


def _filter_tables():
    zs, ts, ms = [], [], []
    for L in (LP, LS):
        t, z = _filter_features(L)
        z = np.pad(z, ((0, 0), (0, FILTER_HIDDEN - EMB_DIM)))
        idx = np.abs(np.arange(2 * L) - L) % L
        zs.append(z[idx])
        ts.append(t[idx])
        ms.append((np.arange(2 * L) != 0).astype(np.float64)[:, None])
    return tuple(jnp.asarray(np.concatenate(a), F32) for a in (zs, ts, ms))


def _filt_kernel(z_ref, t_ref, m_ref, w1_ref, b1_ref, fr_ref, w2_ref, b2_ref, w3_ref,
                 ad_ref, f_ref, o_ref):
    fr = fr_ref[...]
    hid = jnp.sin(fr * (_dot(z_ref[...].astype(BF16), w1_ref[...].astype(BF16)) + b1_ref[...]))
    hid = jnp.sin(fr * (_dot(hid.astype(BF16), w2_ref[...].astype(BF16)) + b2_ref[...]))
    h = _dot(hid.astype(BF16), w3_ref[...].astype(BF16))
    taps = h * jnp.exp(-t_ref[...] * ad_ref[...]) * m_ref[...]
    o_ref[...] = _dot3(_split(f_ref[...]), taps)


def _is_bwd_chunk(c):
    return jnp.logical_or(c == 0, jnp.logical_and(c >= 2, c < 2 + NB_S))


def _filter_spectra(tabs, fwd, j, w1, b1, freq, w2, b2, w3, absd):
    z_tab, t_tab, m_tab = tabs
    small = lambda shape: pl.BlockSpec(shape, lambda c: (0, 0))
    hid = _lay((1, FILTER_HIDDEN), j)
    return pl.pallas_call(
        _filt_kernel,
        grid=(N_CHUNK,),
        in_specs=[
            pl.BlockSpec((CB, FILTER_HIDDEN), lambda c: (c, 0)),
            pl.BlockSpec((CB, 1), lambda c: (c, 0)),
            pl.BlockSpec((CB, 1), lambda c: (c, 0)),
            _lay((FILTER_HIDDEN, FILTER_HIDDEN), j), hid, hid,
            _lay((FILTER_HIDDEN, FILTER_HIDDEN), j), hid,
            pl.BlockSpec((None, FILTER_HIDDEN, D),
                         lambda c: (j, 0, jnp.where(_is_bwd_chunk(c), 1, 0))),
            small((1, D)),
            small((2 * CB, CB)),
        ],
        out_specs=pl.BlockSpec((None, 2 * CB, D), lambda c: (c, 0, 0)),
        out_shape=jax.ShapeDtypeStruct((N_CHUNK, 2 * CB, D), F32),
        compiler_params=_cp(("arbitrary",)),
        name="hy_filter",
    )(z_tab, t_tab, m_tab, w1, b1, freq, w2, b2, w3, absd, fwd)


def _conv_kernel(tc, z_ref, x0_ref, a_ref, f_ref, gi_ref, sg_ref, db_ref, o_ref, u_scr):
    unit = pl.program_id(1)
    fmat = _split(f_ref[...])
    gmat = _split(gi_ref[...])
    sgn = sg_ref[...]
    row0 = lax.broadcasted_iota(jnp.int32, (CB, tc), 0) == 0
    db = db_ref[...]

    def window(c):
        g = a_ref[c] + sgn * a_ref[c - 1]
        gre, gim = g[:CB], g[CB:]
        return gre, jnp.where(row0, 0.0, gim), jnp.where(row0, gim, gre)

    def emit(blk, yre, yim):
        rows = pl.ds(blk * CB, CB)
        y = _dot3(gmat, jnp.concatenate([yre, yim], axis=0))
        zb = z_ref[rows, :]
        o_ref[rows, :] = (x0_ref[rows, :] * (y + zb * db)).astype(BF16)

    @pl.when(unit < MP // LS)
    def _():
        gre, gimz, grez = window(1)
        for s in range(LS // LP):
            u = _dot3(fmat, z_ref[pl.ds(s * CB, CB), :])
            ure, uim = u[:CB], u[CB:]
            emit(s, gre * ure - gimz * uim, grez * uim + gimz * ure)

    @pl.when(unit >= MP // LS)
    def _():
        for j in range(NB_S):
            u_scr[j] = _dot3(fmat, z_ref[pl.ds(j * CB, CB), :])
        for i in range(NB_S):
            yre = jnp.zeros((CB, tc), F32)
            yim = jnp.zeros((CB, tc), F32)
            for j in range(NB_S):
                gre, gimz, grez = window(2 + NB_S + i - j)
                ure, uim = u_scr[j, :CB, :], u_scr[j, CB:, :]
                yre = yre + gre * ure - gimz * uim
                yim = yim + grez * uim + gimz * ure
            emit(i, yre, yim)


def _long_conv(z, x0, spectra, fwd, inv, sgn, d_bias, j, tc=256):
    return pl.pallas_call(
        functools.partial(_conv_kernel, tc),
        grid=(D // tc, M // LS),
        in_specs=[
            pl.BlockSpec((LS, tc), lambda c, u: (u, c)),
            pl.BlockSpec((LS, tc), lambda c, u: (u, c)),
            pl.BlockSpec((N_CHUNK, 2 * CB, tc), lambda c, u: (0, 0, c)),
            pl.BlockSpec((2 * CB, CB), lambda c, u: (0, 0)),
            pl.BlockSpec((CB, 2 * CB), lambda c, u: (0, 0)),
            pl.BlockSpec((2 * CB, 1), lambda c, u: (0, 0)),
            pl.BlockSpec((None, 1, tc), lambda c, u: (j, 0, c)),
        ],
        out_specs=pl.BlockSpec((LS, tc), lambda c, u: (u, c)),
        out_shape=jax.ShapeDtypeStruct((M, D), BF16),
        scratch_shapes=[pltpu.VMEM((NB_S, 2 * CB, tc), F32)],
        compiler_params=_cp(("arbitrary", "arbitrary")),
        name="hy_conv",
    )(z, x0, spectra, fwd, inv, sgn, d_bias)


def _rope_tables(L):
    rows = L // GRID_W
    r = np.repeat(np.arange(rows, dtype=np.float64), GRID_W)
    cidx = np.tile(np.arange(GRID_W, dtype=np.float64), rows)
    inv = ROPE_THETA ** (-np.arange(ROT_FREQS, dtype=np.float64) / ROT_FREQS)
    ar = r[:, None] * inv
    ac = cidx[:, None] * inv
    cos = np.concatenate([np.cos(ar), np.cos(ar), np.cos(ac), np.cos(ac)] * 2, axis=-1)
    sin = np.concatenate([np.sin(ar), np.sin(ar), np.sin(ac), np.sin(ac)] * 2, axis=-1)
    first_half = (np.arange(V_DIM) % (2 * ROT_FREQS)) < ROT_FREQS
    sin_a = np.where(first_half, -sin, 0.0)
    sin_b = np.where(first_half, 0.0, sin)
    return tuple(jnp.asarray(a, F32) for a in (cos, sin_a, sin_b))


def _rope(x, cos, sin_a, sin_b):
    return (x * cos + pltpu.roll(x, V_DIM - ROT_FREQS, axis=1) * sin_a
            + pltpu.roll(x, ROT_FREQS, axis=1) * sin_b)


HEADS_PER_CHUNK = CK // V_DIM


def _qkv_chunks():
    per_part = D // CK
    return [(slice(c * CK, (c + 1) * CK), c // per_part, (c % per_part) * HEADS_PER_CHUNK)
            for c in range(3 * per_part)]


def _qkv_c_kernel(aliased, x_ref, nw_ref, sh_ref, sc_ref, w_ref, *rest):
    qkv_ref, nk_ref, nv_ref = rest[2:] if aliased else rest
    h = _hmod(x_ref[...], nw_ref, sh_ref, sc_ref)
    for cs, part, head0 in _qkv_chunks():
        u = _dot(h, w_ref[:, cs])
        qkv_ref[:, cs] = (u * QSCALE if part == 0 else u).astype(BF16)
        if part > 0:
            cache = nk_ref if part == 1 else nv_ref
            for s in range(TM // LP):
                for hh in range(HEADS_PER_CHUNK):
                    cache[s, head0 + hh] = u[s * LP:(s + 1) * LP, hh * V_DIM:(hh + 1) * V_DIM]


def _qkv_ctx(x, nw_all, mods, layer, j, w_bf, caches):
    seqs = TM // LP
    cache = pl.BlockSpec((seqs, None, N_HEADS, LP, V_DIM), lambda i: (i, j, 0, 0, 0))
    cache_shape = jax.ShapeDtypeStruct((BP, N_ATTN, N_HEADS, LP, V_DIM), F32)
    in_specs = [_tile_spec(D, _joint_tile), _norm_spec(layer, 0), _mod_spec(layer, 0),
                _mod_spec(layer, 1), _lay((D, 3 * D), j)]
    args = [x, nw_all, mods, mods, w_bf]
    aliases = {}
    if caches is not None:
        in_specs += [pl.BlockSpec(memory_space=pl.ANY)] * 2
        args += list(caches)
        aliases = {len(args) - 2: 1, len(args) - 1: 2}
    return pl.pallas_call(
        functools.partial(_qkv_c_kernel, caches is not None),
        grid=(N_CTX_TILES,),
        in_specs=in_specs,
        out_specs=[_tile_spec(3 * D, _joint_tile), cache, cache],
        out_shape=[jax.ShapeDtypeStruct((MP, 3 * D), BF16), cache_shape, cache_shape],
        input_output_aliases=aliases,
        compiler_params=_cp(("arbitrary",)),
        name="qkv_ctx",
    )(*args)


def _qkv_l_kernel(x_ref, nw_ref, sh_ref, sc_ref, w_ref, cos_ref, sa_ref, sb_ref, qkv_ref):
    h = _hmod(x_ref[...], nw_ref, sh_ref, sc_ref)
    for cs, part, _ in _qkv_chunks():
        u = _dot(h, w_ref[:, cs])
        if part == 2:
            qkv_ref[:, cs] = u.astype(BF16)
            continue
        for hh in range(HEADS_PER_CHUNK):
            r = _rope(u[:, hh * V_DIM:(hh + 1) * V_DIM], cos_ref[...], sa_ref[...], sb_ref[...])
            if part == 0:
                r = r * QSCALE
            qkv_ref[:, cs.start + hh * V_DIM:cs.start + (hh + 1) * V_DIM] = r.astype(BF16)


def _qkv_lat(x, nw_all, mods, layer, j, w_bf, ropes):
    tile = lambda i: i + N_CTX_TILES
    tab = pl.BlockSpec((TM, V_DIM), lambda i: (i % (LS // TM), 0))
    return pl.pallas_call(
        _qkv_l_kernel,
        grid=(N_LAT_TILES,),
        in_specs=[_tile_spec(D, tile), _norm_spec(layer, 0), _mod_spec(layer, 0, tile),
                  _mod_spec(layer, 1, tile), _lay((D, 3 * D), j), tab, tab, tab],
        out_specs=_tile_spec(3 * D, _joint_tile),
        out_shape=jax.ShapeDtypeStruct((MS, 3 * D), BF16),
        compiler_params=_cp(("arbitrary",)),
        name="qkv_lat",
    )(x, nw_all, mods, mods, w_bf, *ropes)


def _lambda(lv, lam_init):
    a = jnp.exp(jnp.sum(lv[0:1, :] * lv[1:2, :], axis=-1, keepdims=True))
    b = jnp.exp(jnp.sum(lv[2:3, :] * lv[3:4, :], axis=-1, keepdims=True))
    return a - b + lam_init


def _diff_attn(q, chunks, lam, lam_init, subln):
    t = q.shape[0]
    lane = lax.broadcasted_iota(jnp.int32, q.shape, 1)
    zero = jnp.zeros_like(q)
    q2 = jnp.concatenate([jnp.where(lane < HEAD_DIM, q, zero),
                          jnp.where(lane < HEAD_DIM, zero, q)], axis=0)
    m = l = acc = None
    for k, v in chunks:
        s = lax.dot_general(q2, k, (((1,), (1,)), ((), ())), preferred_element_type=F32)
        mc = jnp.max(s, axis=-1, keepdims=True)
        m_new = mc if m is None else jnp.maximum(m, mc)
        e = jnp.exp2(s - m_new)
        lc = jnp.sum(e, axis=-1, keepdims=True)
        pv = _dot(e.astype(BF16), v)
        if m is None:
            l, acc = lc, pv
        else:
            alpha = jnp.exp2(m - m_new)
            l = alpha * l + lc
            acc = alpha * acc + pv
        m = m_new
    o = acc[:t] * (1.0 / l[:t]) - acc[t:] * (lam / l[t:])
    return _rms(o, subln) * (1.0 - lam_init)


def _attn_c_kernel(lam_init, q_ref, k_ref, v_ref, lv_ref, sub_ref, o_ref):
    lam = _lambda(lv_ref[...], lam_init)
    for h in range(N_HEADS):
        hs = slice(h * V_DIM, (h + 1) * V_DIM)
        o = _diff_attn(q_ref[:, hs], [(k_ref[:, hs], v_ref[:, hs])], lam, lam_init, sub_ref[...])
        o_ref[:, hs] = o.astype(BF16)


def _attn_ctx(qkv_c, lamv, subln, j, lam_init):
    part = lambda p: pl.BlockSpec((LP, D), lambda b: (b, p))
    return pl.pallas_call(
        functools.partial(_attn_c_kernel, lam_init),
        grid=(BP,),
        in_specs=[part(0), part(1), part(2),
                  pl.BlockSpec((None, 4, HEAD_DIM), lambda b: (j, 0, 0)),
                  pl.BlockSpec((None, 1, V_DIM), lambda b: (j, 0, 0))],
        out_specs=pl.BlockSpec((LP, D), lambda b: (b, 0)),
        out_shape=jax.ShapeDtypeStruct((MP, D), BF16),
        compiler_params=_cp(("arbitrary",)),
        name="attn_ctx",
    )(qkv_c, qkv_c, qkv_c, lamv, subln)


def _attn_l_kernel(lam_init, q_ref, k_ref, v_ref, ck_ref, cv_ref, lv_ref, sub_ref, o_ref):
    chunks = [(ck_ref[...].astype(BF16), cv_ref[...].astype(BF16))]
    for c in range(LS // KEY_CHUNK):
        rows = pl.ds(c * KEY_CHUNK, KEY_CHUNK)
        chunks.append((k_ref[rows, :], v_ref[rows, :]))
    lam = _lambda(lv_ref[...], lam_init)
    o = _diff_attn(q_ref[...], chunks, lam, lam_init, sub_ref[...])
    o_ref[...] = o.astype(BF16)


def _attn_lat(qkv_l, cache_k, cache_v, lamv, subln, j, lam_init, tq=256):
    nq = LS // tq
    seq = lambda off: pl.BlockSpec((LS, V_DIM), lambda b, h, q: (b, off + h))
    ctx = pl.BlockSpec((None, None, None, PAST, V_DIM), lambda b, h, q: (b, j, h, 0, 0))
    return pl.pallas_call(
        functools.partial(_attn_l_kernel, lam_init),
        grid=(BS, N_HEADS, nq),
        in_specs=[pl.BlockSpec((tq, V_DIM), lambda b, h, q: (b * nq + q, h)),
                  seq(N_HEADS), seq(2 * N_HEADS), ctx, ctx,
                  pl.BlockSpec((None, 4, HEAD_DIM), lambda b, h, q: (j, 0, 0)),
                  pl.BlockSpec((None, 1, V_DIM), lambda b, h, q: (j, 0, 0))],
        out_specs=pl.BlockSpec((tq, V_DIM), lambda b, h, q: (b * nq + q, h)),
        out_shape=jax.ShapeDtypeStruct((MS, D), BF16),
        compiler_params=_cp(("arbitrary", "arbitrary", "arbitrary")),
        name="attn_lat",
    )(qkv_l, qkv_l, qkv_l, cache_k, cache_v, lamv, subln)


def kernel(x_prompt, x_sample, cache_k, cache_v, c, c_ctx, w_ada, b_ada, norm_w, hy_w_in, hy_b_in, hy_w_short, hy_b_short, hy_f_w1, hy_f_b1, hy_f_freq, hy_f_w2, hy_f_b2, hy_f_w3, hy_d_bias, hy_w_out, hy_b_out, at_w_qkv, at_w_out, at_lambda_q1, at_lambda_k1, at_lambda_q2, at_lambda_k2, at_subln, ffn_w_up, ffn_w_dw, ffn_b_dw, ffn_w_down):
    cond8 = jnp.concatenate([c_ctx[None, :], c, jnp.zeros((SUB - 1 - BS, D), F32)], axis=0)
    mods = _ada(cond8, w_ada, b_ada)

    fwd, inv, sgn = _dft_mats()
    tabs = _filter_tables()
    min_decay = math.log(DECAY_TARGET) / DECAY_PCT_LONG
    max_decay = math.log(DECAY_TARGET) / DECAY_PCT_SHORT
    absd = jnp.asarray(np.abs(np.linspace(min_decay, max_decay, D))[None, :], F32)
    ropes = _rope_tables(LS)

    row = lambda a: a.reshape(a.shape[0], 1, a.shape[1])
    nw_all = norm_w.reshape(DEPTH * 4, 1, D)
    w_in_bf, w_hy_out_bf = hy_w_in.astype(BF16), hy_w_out.astype(BF16)
    w_qkv_bf, w_at_out_bf = at_w_qkv.astype(BF16), at_w_out.astype(BF16)
    w_up_bf, w_down_bf = ffn_w_up.astype(BF16), ffn_w_down.astype(BF16)
    w1_pad = jnp.pad(hy_f_w1, ((0, 0), (0, FILTER_HIDDEN - EMB_DIM), (0, 0)))
    lamv = jnp.stack([at_lambda_q1, at_lambda_k1, at_lambda_q2, at_lambda_k2], axis=1)
    subln = row(at_subln)

    x_parts = [x_prompt.reshape(MP, D), x_sample.reshape(MS, D)]
    caches = None
    for i in range(DEPTH):
        j = i // 2
        if i % 2 == 0:
            x0, z = _hy_in(x_parts, nw_all, mods, i, j, w_in_bf, row(hy_b_in), hy_w_short,
                           row(hy_b_short))
            spectra = _filter_spectra(tabs, fwd, j, w1_pad, row(hy_f_b1), row(hy_f_freq),
                                      hy_f_w2, row(hy_f_b2), hy_f_w3, absd)
            a_parts = [_long_conv(z, x0, spectra, fwd, inv, sgn, row(hy_d_bias), j)]
            w_out, b_out = w_hy_out_bf, row(hy_b_out)
        else:
            lam_init = 0.8 - 0.6 * math.exp(-0.3 * i)
            x = x_parts[0]
            qkv_c, new_k, new_v = _qkv_ctx(x, nw_all, mods, i, j, w_qkv_bf, caches)
            caches = (new_k, new_v)
            qkv_l = _qkv_lat(x, nw_all, mods, i, j, w_qkv_bf, ropes)
            a_parts = [_attn_ctx(qkv_c, lamv, subln, j, lam_init),
                       _attn_lat(qkv_l, cache_k, cache_v, lamv, subln, j, lam_init)]
            w_out, b_out = w_at_out_bf, None
        x = _proj_out(a_parts, x_parts, w_out, b_out, nw_all, mods, i, j)
        y = _ffn(x, nw_all, mods, i, w_up_bf, ffn_w_dw, row(ffn_b_dw), w_down_bf,
                 split_out=(i == DEPTH - 1))
        x_parts = list(y) if i == DEPTH - 1 else [y]

    return (x_parts[0].reshape(BP, LP, D), x_parts[1].reshape(BS, LS, D), caches[0], caches[1])
```

```python
import functools
import math

import numpy as np
import jax
import jax.numpy as jnp
from jax import lax
from jax.experimental import pallas as pl
from jax.experimental.pallas import tpu as pltpu

D = 1024
BP, LP = 16, 256
BS, LS = 2, 2048
MP = BP * LP
MS = BS * LS
M = MP + MS
DEPTH = 4
N_ATTN = DEPTH // 2
GRID_W = 64
N_HEADS = 8
HEAD_DIM = 64
V_DIM = 128
ROPE_THETA = 10000.0
ROT_FREQS = 16
EMB_BANDS = 16
EMB_DIM = 33
FILTER_HIDDEN = 64
DECAY_TARGET = 1e-2
DECAY_PCT_SHORT = 0.3
DECAY_PCT_LONG = 1.5
D_FF = 2816
EPS = 1e-6
PAST = 256

CB = 256
NB_S = LS // CB
N_CHUNK = 2 + 2 * NB_S
SUB = 8
HALO = 2 * SUB
SLAB = 16
CK = 256
TM = 512
KEY_CHUNK = 512
VMEM_LIMIT = 56 * 1024 * 1024
QSCALE = HEAD_DIM ** -0.5 * math.log2(math.e)

F32 = jnp.float32
BF16 = jnp.bfloat16


def _dot(a, b):
    return jnp.dot(a, b, preferred_element_type=F32)


def _rms(x, w):
    ms = jnp.mean(x * x, axis=-1, keepdims=True)
    return x * lax.rsqrt(ms + EPS) * w


def _silu(x):
    return x / (1.0 + jnp.exp(-x))


def _cp(sem, vmem=VMEM_LIMIT):
    return pltpu.CompilerParams(dimension_semantics=sem, vmem_limit_bytes=vmem)


def _lay(shape, idx):
    return pl.BlockSpec((None,) + tuple(shape), lambda i: (idx,) + (0,) * len(shape),
                        pipeline_mode=pl.Buffered(1))


def _norm_spec(layer, k):
    return _lay((1, D), layer * 4 + k)


N_CTX_TILES = MP // TM
N_LAT_TILES = MS // TM


def _ctx_tile(i):
    return jnp.minimum(i, N_CTX_TILES - 1)


def _lat_tile(i):
    return jnp.maximum(i - N_CTX_TILES, 0)


def _joint_tile(i):
    return i


def _is_ctx():
    return pl.program_id(0) < N_CTX_TILES


def _tile_spec(width, tile_of):
    return pl.BlockSpec((TM, width), lambda i: (tile_of(i), 0))


def _ext_specs(n_rows, tile_of):
    r = TM // SUB
    last_blk = n_rows // SUB - 1
    return [
        pl.BlockSpec((TM, D), lambda i: (tile_of(i), 0)),
        pl.BlockSpec((SUB, D), lambda i: (jnp.maximum(tile_of(i) * r - 1, 0), 0)),
        pl.BlockSpec((SUB, D), lambda i: (jnp.minimum((tile_of(i) + 1) * r, last_blk), 0)),
    ]


def _per_part(n_parts, fn, refs):
    if n_parts == 1:
        fn(*refs)
        return
    k = len(refs) // 2
    pl.when(_is_ctx())(lambda: fn(*refs[:k]))
    pl.when(jnp.logical_not(_is_ctx()))(lambda: fn(*refs[k:]))


def _ada_kernel(c_ref, w_ref, b_ref, o_ref):
    s = _silu(c_ref[...]).astype(BF16)
    o_ref[...] = _dot(s, w_ref[...].astype(BF16)) + b_ref[...]


def _ada(cond8, w_ada, b_ada):
    out = pl.pallas_call(
        _ada_kernel,
        grid=(DEPTH, 6),
        in_specs=[
            pl.BlockSpec((SUB, D), lambda l, k: (0, 0)),
            pl.BlockSpec((None, D, D), lambda l, k: (l, 0, k)),
            pl.BlockSpec((None, 1, D), lambda l, k: (l, 0, k)),
        ],
        out_specs=pl.BlockSpec((None, SUB, D), lambda l, k: (l * 6 + k, 0, 0)),
        out_shape=jax.ShapeDtypeStruct((DEPTH * 6, SUB, D), F32),
        compiler_params=_cp(("arbitrary", "arbitrary")),
        name="ada",
    )(cond8, w_ada, b_ada.reshape(DEPTH, 1, 6 * D))
    return out.reshape(DEPTH * 6 * SUB, 1, D)


def _mod_spec(layer, which, tile_of=_joint_tile):
    base = (layer * 6 + which) * SUB
    per_b = LS // TM

    def imap(i):
        t = tile_of(i)
        r = jnp.where(t < N_CTX_TILES, 0, 1 + (t - N_CTX_TILES) // per_b)
        return (base + r, 0, 0)

    return pl.BlockSpec((None, 1, D), imap)


def _hmod(x, nw_ref, sh_ref, sc_ref):
    return (_rms(x, nw_ref[...]) * (1.0 + sc_ref[...]) + sh_ref[...]).astype(BF16)


def _fill_h(h_scr, nw_ref, sh_ref, sc_ref, x_ref, xp_ref, xn_ref):
    h_scr[0:TM, :] = _hmod(x_ref[...], nw_ref, sh_ref, sc_ref)
    h_scr[TM:TM + SUB, :] = _hmod(xp_ref[...], nw_ref, sh_ref, sc_ref)
    h_scr[TM + SUB:TM + HALO, :] = _hmod(xn_ref[...], nw_ref, sh_ref, sc_ref)


def _tile_flags():
    i = pl.program_id(0)
    is_ctx = _is_ctx()
    lseq = jnp.where(is_ctx, LP, LS)
    starts = ((i * TM) & (lseq - 1)) == 0
    ends = (((i + 1) * TM) & (lseq - 1)) == 0
    return is_ctx, starts, ends


def _edge_slabs():
    return sorted({b for b in range(0, TM, LP)} | {b + LP - SLAB for b in range(0, TM, LP)})


def _conv3_bulk(u, w, b):
    return (pltpu.roll(u, 1, axis=0) * w[0:1, :] + u * w[1:2, :]
            + pltpu.roll(u, TM - 1, axis=0) * w[2:3, :] + b)


def _conv3_slab(u_ext, s, w, b, flags):
    is_ctx, starts, ends = flags
    us = u_ext[s:s + SLAB, :]
    if s == 0:
        prev = jnp.where(starts, 0.0, u_ext[TM + SUB - 1:TM + SUB, :])
    else:
        prev = u_ext[s - 1:s, :]
        if s % LP == 0:
            prev = jnp.where(is_ctx, 0.0, prev)
    if s + SLAB == TM:
        nxt = jnp.where(ends, 0.0, u_ext[TM + SUB:TM + SUB + 1, :])
    else:
        nxt = u_ext[s + SLAB:s + SLAB + 1, :]
        if (s + SLAB) % LP == 0:
            nxt = jnp.where(is_ctx, 0.0, nxt)
    rows = lax.broadcasted_iota(jnp.int32, us.shape, 0)
    up = jnp.where(rows == 0, prev, pltpu.roll(us, 1, axis=0))
    dn = jnp.where(rows == SLAB - 1, nxt, pltpu.roll(us, SLAB - 1, axis=0))
    return up * w[0:1, :] + us * w[1:2, :] + dn * w[2:3, :] + b


def _ffn_kernel(split_out, x_ref, xp_ref, xn_ref, nwi_ref, sh_ref, sc_ref, wup_ref, wdw_ref,
                bdw_ref, wdn_ref, nwo_ref, g_ref, *rest):
    h_scr, act_scr = rest[-2:]
    _fill_h(h_scr, nwi_ref, sh_ref, sc_ref, x_ref, xp_ref, xn_ref)
    flags = _tile_flags()
    h = h_scr[...]
    for c in range(D_FF // CK):
        cg = slice(c * CK, (c + 1) * CK)
        cv = slice(D_FF + c * CK, D_FF + (c + 1) * CK)
        g_ext = _dot(h, wup_ref[:, cg])
        v_ext = _dot(h, wup_ref[:, cv])
        wg, bg, wv, bv = wdw_ref[:, cg], bdw_ref[:, cg], wdw_ref[:, cv], bdw_ref[:, cv]
        g = _conv3_bulk(g_ext[0:TM, :], wg, bg)
        val = _conv3_bulk(v_ext[0:TM, :], wv, bv)
        act_scr[:, cg] = (_silu(g) * val).astype(BF16)
        for s in _edge_slabs():
            g = _conv3_slab(g_ext, s, wg, bg, flags)
            val = _conv3_slab(v_ext, s, wv, bv, flags)
            act_scr[s:s + SLAB, cg] = (_silu(g) * val).astype(BF16)
    y = _dot(act_scr[...], wdn_ref[...])
    res = x_ref[...] + g_ref[...] * _rms(y, nwo_ref[...])
    if split_out:
        yc_ref, yl_ref = rest[0:2]
        is_ctx = flags[0]

        @pl.when(is_ctx)
        def _():
            yc_ref[...] = res

        @pl.when(jnp.logical_not(is_ctx))
        def _():
            yl_ref[...] = res
    else:
        rest[0][...] = res


def _ffn(x, nw_all, mods, layer, wup_bf, w_dw, b_dw, wdn_bf, split_out):
    if split_out:
        out_specs = [_tile_spec(D, _ctx_tile), _tile_spec(D, _lat_tile)]
        out_shape = [jax.ShapeDtypeStruct((MP, D), F32), jax.ShapeDtypeStruct((MS, D), F32)]
    else:
        out_specs = _tile_spec(D, _joint_tile)
        out_shape = jax.ShapeDtypeStruct((M, D), F32)
    return pl.pallas_call(
        functools.partial(_ffn_kernel, split_out),
        grid=(M // TM,),
        in_specs=_ext_specs(M, _joint_tile) + [
            _norm_spec(layer, 2), _mod_spec(layer, 3), _mod_spec(layer, 4),
            _lay((D, 2 * D_FF), layer), _lay((3, 2 * D_FF), layer), _lay((1, 2 * D_FF), layer),
            _lay((D_FF, D), layer),
            _norm_spec(layer, 3), _mod_spec(layer, 5),
        ],
        out_specs=out_specs,
        out_shape=out_shape,
        scratch_shapes=[pltpu.VMEM((TM + HALO, D), BF16), pltpu.VMEM((TM, D_FF), BF16)],
        compiler_params=_cp(("arbitrary",)),
        name="ffn",
    )(x, x, x, nw_all, mods, mods, wup_bf, w_dw, b_dw, wdn_bf, nw_all, mods)


def _hy_in_kernel(n_parts, *refs):
    x_refs = refs[:3 * n_parts]
    (nw_ref, sh_ref, sc_ref, w_ref, b_ref, ws_ref, bs_ref, x0_ref, z_ref,
     h_scr) = refs[3 * n_parts:]
    _per_part(n_parts, functools.partial(_fill_h, h_scr, nw_ref, sh_ref, sc_ref), x_refs)
    flags = _tile_flags()
    h = h_scr[...]
    for c in range(D // CK):
        cc = slice(c * CK, (c + 1) * CK)
        u_ext, ws, bs = [], [], []
        for s in range(3):
            cs = slice(s * D + c * CK, s * D + (c + 1) * CK)
            u_ext.append(_dot(h, w_ref[:, cs]) + b_ref[:, cs])
            ws.append(ws_ref[:, cs])
            bs.append(bs_ref[:, cs])
        out = [_conv3_bulk(u_ext[s][0:TM, :], ws[s], bs[s]) for s in range(3)]
        x0_ref[:, cc] = out[0].astype(BF16)
        z_ref[:, cc] = (out[2] * out[1]).astype(BF16)
        for r in _edge_slabs():
            out = [_conv3_slab(u_ext[s], r, ws[s], bs[s], flags) for s in range(3)]
            x0_ref[r:r + SLAB, cc] = out[0].astype(BF16)
            z_ref[r:r + SLAB, cc] = (out[2] * out[1]).astype(BF16)


def _hy_in(x_parts, nw_all, mods, layer, j, w_bf, b_in, w_short, b_short):
    if len(x_parts) == 1:
        x_specs = _ext_specs(M, _joint_tile)
    else:
        x_specs = _ext_specs(MP, _ctx_tile) + _ext_specs(MS, _lat_tile)
    x_args = [a for a in x_parts for _ in range(3)]
    return pl.pallas_call(
        functools.partial(_hy_in_kernel, len(x_parts)),
        grid=(M // TM,),
        in_specs=x_specs + [
            _norm_spec(layer, 0), _mod_spec(layer, 0), _mod_spec(layer, 1),
            _lay((D, 3 * D), j), _lay((1, 3 * D), j), _lay((3, 3 * D), j), _lay((1, 3 * D), j),
        ],
        out_specs=[_tile_spec(D, _joint_tile)] * 2,
        out_shape=[jax.ShapeDtypeStruct((M, D), BF16)] * 2,
        scratch_shapes=[pltpu.VMEM((TM + HALO, D), BF16)],
        compiler_params=_cp(("arbitrary",)),
        name="hy_in",
    )(*x_args, nw_all, mods, mods, w_bf, b_in, w_short, b_short)


def _out_kernel(n_a, n_x, has_bias, *refs):
    a_refs, x_refs = refs[:n_a], refs[n_a:n_a + n_x]
    rest = refs[n_a + n_x:]
    if has_bias:
        w_ref, b_ref, nw_ref, g_ref, o_ref = rest
    else:
        w_ref, nw_ref, g_ref, o_ref = rest

    def mixed(a_ref):
        y = _dot(a_ref[...], w_ref[...])
        if has_bias:
            y = y + b_ref[...]
        o_ref[...] = g_ref[...] * _rms(y, nw_ref[...])

    def residual(x_ref):
        o_ref[...] = o_ref[...] + x_ref[...]

    _per_part(n_a, mixed, a_refs)
    _per_part(n_x, residual, x_refs)


def _proj_out(a_parts, x_parts, w_bf, bias, nw_all, mods, layer, j):
    k = w_bf.shape[1]
    has_bias = bias is not None

    def specs(parts, width):
        if len(parts) == 1:
            return [_tile_spec(width, _joint_tile)]
        return [_tile_spec(width, _ctx_tile), _tile_spec(width, _lat_tile)]

    in_specs = specs(a_parts, k) + specs(x_parts, D) + [_lay((k, D), j)]
    args = list(a_parts) + list(x_parts) + [w_bf]
    if has_bias:
        in_specs.append(_lay((1, D), j))
        args.append(bias)
    in_specs += [_norm_spec(layer, 1), _mod_spec(layer, 2)]
    args += [nw_all, mods]
    return pl.pallas_call(
        functools.partial(_out_kernel, len(a_parts), len(x_parts), has_bias),
        grid=(M // TM,),
        in_specs=in_specs,
        out_specs=_tile_spec(D, _joint_tile),
        out_shape=jax.ShapeDtypeStruct((M, D), F32),
        compiler_params=_cp(("arbitrary",)),
        name="proj_out",
    )(*args)


def _dft_mats():
    n = np.arange(CB, dtype=np.float64)
    f = np.arange(CB, dtype=np.float64)
    ang = 2.0 * np.pi * np.outer(f, n) / (2 * CB)
    fwd = np.concatenate([np.cos(ang), -np.sin(ang)], axis=0)
    fwd[CB] = np.cos(np.pi * n)
    scale = np.full((2 * CB, 1), 2.0 / (2 * CB))
    scale[0] = scale[CB] = 1.0 / (2 * CB)
    inv = (fwd * scale).T
    sgn = np.where(np.arange(CB) % 2 == 0, 1.0, -1.0)
    sgn2 = np.concatenate([sgn, sgn])[:, None]
    sgn2[CB] = 1.0
    return (jnp.asarray(fwd, F32).astype(BF16), jnp.asarray(inv, F32).astype(BF16),
            jnp.asarray(sgn2, F32))


def _filter_features(L):
    t = np.linspace(0.0, 1.0, L)[:, None]
    w = 2.0 * np.pi * np.arange(L)[:, None] / L
    bands = np.linspace(1e-4, EMB_BANDS - 1, EMB_BANDS)
    z = np.concatenate([t, np.cos(bands * w), -np.sin(bands * w)], axis=-1)
    return t, z


def _filter_tables():
    zs, ts, ms = [], [], []
    for L in (LP, LS):
        t, z = _filter_features(L)
        z = np.pad(z, ((0, 0), (0, FILTER_HIDDEN - EMB_DIM)))
        idx = np.abs(np.arange(2 * L) - L) % L
        zs.append(z[idx])
        ts.append(t[idx])
        ms.append((np.arange(2 * L) != 0).astype(np.float64)[:, None])
    return tuple(jnp.asarray(np.concatenate(a), F32) for a in (zs, ts, ms))


N_WIN = 1 + 2 * NB_S - 1


def _filt_kernel(z_ref, t_ref, m_ref, w1_ref, b1_ref, fr_ref, w2_ref, b2_ref, w3_ref,
                 ad_ref, f_ref, sg_ref, g_ref, nyq_ref, prev_scr):
    c = pl.program_id(0)

    @pl.when(c == 0)
    def _():
        prev_scr[...] = jnp.zeros_like(prev_scr)

    fr = fr_ref[...]
    hid = jnp.sin(fr * (_dot(z_ref[...].astype(BF16), w1_ref[...].astype(BF16)) + b1_ref[...]))
    hid = jnp.sin(fr * (_dot(hid.astype(BF16), w2_ref[...].astype(BF16)) + b2_ref[...]))
    h = _dot(hid.astype(BF16), w3_ref[...].astype(BF16))
    taps = h * jnp.exp(-t_ref[...] * ad_ref[...]) * m_ref[...]
    a = _dot(f_ref[...], taps.astype(BF16))
    g = a + sg_ref[...] * prev_scr[...]
    prev_scr[...] = a
    g_ref[...] = g
    nyq_ref[...] = g[CB:CB + 1, :]
    g_ref[CB:CB + 1, :] = jnp.zeros((1, D), F32)


def _is_bwd_chunk(c):
    return jnp.logical_or(c == 0, jnp.logical_and(c >= 2, c < 2 + NB_S))


def _window_of_chunk(c):
    return jnp.where(c <= 1, 0, jnp.maximum(c - 2, 1))


def _filter_spectra(tabs, fwd, sgn, j, w1, b1, freq, w2, b2, w3, absd):
    z_tab, t_tab, m_tab = tabs
    small = lambda shape: pl.BlockSpec(shape, lambda c: (0, 0))
    hid = _lay((1, FILTER_HIDDEN), j)
    return pl.pallas_call(
        _filt_kernel,
        grid=(N_CHUNK,),
        in_specs=[
            pl.BlockSpec((CB, FILTER_HIDDEN), lambda c: (c, 0)),
            pl.BlockSpec((CB, 1), lambda c: (c, 0)),
            pl.BlockSpec((CB, 1), lambda c: (c, 0)),
            _lay((FILTER_HIDDEN, FILTER_HIDDEN), j), hid, hid,
            _lay((FILTER_HIDDEN, FILTER_HIDDEN), j), hid,
            pl.BlockSpec((None, FILTER_HIDDEN, D),
                         lambda c: (j, 0, jnp.where(_is_bwd_chunk(c), 1, 0))),
            small((1, D)),
            small((2 * CB, CB)),
            small((2 * CB, 1)),
        ],
        out_specs=[pl.BlockSpec((None, 2 * CB, D), lambda c: (_window_of_chunk(c), 0, 0)),
                   pl.BlockSpec((None, 1, D), lambda c: (_window_of_chunk(c), 0, 0))],
        out_shape=[jax.ShapeDtypeStruct((N_WIN, 2 * CB, D), F32),
                   jax.ShapeDtypeStruct((N_WIN, 1, D), F32)],
        scratch_shapes=[pltpu.VMEM((2 * CB, D), F32)],
        compiler_params=_cp(("arbitrary",)),
        name="hy_filter",
    )(z_tab, t_tab, m_tab, w1, b1, freq, w2, b2, w3, absd, fwd, sgn)


def _conv_kernel(tc, z_ref, x0_ref, g_ref, nyq_ref, f_ref, gi_ref, db_ref, o_ref, u_scr):
    unit = pl.program_id(1)
    fmat = f_ref[...]
    gmat = gi_ref[...]
    row0 = lax.broadcasted_iota(jnp.int32, (CB, tc), 0) == 0
    db = db_ref[...]

    def emit(blk, yre, yim, ynyq):
        rows = pl.ds(blk * CB, CB)
        yspec = jnp.concatenate([yre, jnp.where(row0, ynyq, yim)], axis=0)
        y = _dot(gmat, yspec.astype(BF16))
        o_ref[rows, :] = (x0_ref[rows, :] * (y + z_ref[rows, :] * db)).astype(BF16)

    @pl.when(unit < MP // LS)
    def _():
        gre, gim, gnyq = g_ref[0, :CB, :], g_ref[0, CB:, :], nyq_ref[0]
        for s in range(LS // LP):
            u = _dot(fmat, z_ref[pl.ds(s * CB, CB), :])
            ure, unyq = u[:CB], u[CB:CB + 1]
            uim = jnp.where(row0, 0.0, u[CB:])
            emit(s, gre * ure - gim * uim, gre * uim + gim * ure, gnyq * unyq)

    @pl.when(unit >= MP // LS)
    def _():
        unyq = []
        for j in range(NB_S):
            u = _dot(fmat, z_ref[pl.ds(j * CB, CB), :])
            unyq.append(u[CB:CB + 1])
            u_scr[j] = u
            u_scr[j, CB:CB + 1, :] = jnp.zeros((1, tc), F32)
        for i in range(NB_S):
            yre = jnp.zeros((CB, tc), F32)
            yim = jnp.zeros((CB, tc), F32)
            ynyq = jnp.zeros((1, tc), F32)
            for j in range(NB_S):
                w = NB_S + i - j
                gre, gim = g_ref[w, :CB, :], g_ref[w, CB:, :]
                ure, uim = u_scr[j, :CB, :], u_scr[j, CB:, :]
                yre = yre + gre * ure - gim * uim
                yim = yim + gre * uim + gim * ure
                ynyq = ynyq + nyq_ref[w] * unyq[j]
            emit(i, yre, yim, ynyq)


def _long_conv(z, x0, spectra, fwd, inv, d_bias, j, tc=256):
    g_win, g_nyq = spectra
    return pl.pallas_call(
        functools.partial(_conv_kernel, tc),
        grid=(D // tc, M // LS),
        in_specs=[
            pl.BlockSpec((LS, tc), lambda c, u: (u, c)),
            pl.BlockSpec((LS, tc), lambda c, u: (u, c)),
            pl.BlockSpec((N_WIN, 2 * CB, tc), lambda c, u: (0, 0, c)),
            pl.BlockSpec((N_WIN, 1, tc), lambda c, u: (0, 0, c)),
            pl.BlockSpec((2 * CB, CB), lambda c, u: (0, 0)),
            pl.BlockSpec((CB, 2 * CB), lambda c, u: (0, 0)),
            pl.BlockSpec((None, 1, tc), lambda c, u: (j, 0, c)),
        ],
        out_specs=pl.BlockSpec((LS, tc), lambda c, u: (u, c)),
        out_shape=jax.ShapeDtypeStruct((M, D), BF16),
        scratch_shapes=[pltpu.VMEM((NB_S, 2 * CB, tc), F32)],
        compiler_params=_cp(("arbitrary", "arbitrary")),
        name="hy_conv",
    )(z, x0, g_win, g_nyq, fwd, inv, d_bias)


def _rope_tables(L):
    rows = L // GRID_W
    r = np.repeat(np.arange(rows, dtype=np.float64), GRID_W)
    cidx = np.tile(np.arange(GRID_W, dtype=np.float64), rows)
    inv = ROPE_THETA ** (-np.arange(ROT_FREQS, dtype=np.float64) / ROT_FREQS)
    ar = r[:, None] * inv
    ac = cidx[:, None] * inv
    cos = np.concatenate([np.cos(ar), np.cos(ar), np.cos(ac), np.cos(ac)] * 2, axis=-1)
    sin = np.concatenate([np.sin(ar), np.sin(ar), np.sin(ac), np.sin(ac)] * 2, axis=-1)
    first_half = (np.arange(V_DIM) % (2 * ROT_FREQS)) < ROT_FREQS
    sin_a = np.where(first_half, -sin, 0.0)
    sin_b = np.where(first_half, 0.0, sin)
    return tuple(jnp.asarray(a, F32) for a in (cos, sin_a, sin_b))


def _rope(x, cos, sin_a, sin_b):
    return (x * cos + pltpu.roll(x, V_DIM - ROT_FREQS, axis=1) * sin_a
            + pltpu.roll(x, ROT_FREQS, axis=1) * sin_b)


HEADS_PER_CHUNK = CK // V_DIM


def _qkv_chunks():
    per_part = D // CK
    return [(slice(c * CK, (c + 1) * CK), c // per_part, (c % per_part) * HEADS_PER_CHUNK)
            for c in range(3 * per_part)]


def _qkv_c_kernel(aliased, x_ref, nw_ref, sh_ref, sc_ref, w_ref, *rest):
    qkv_ref, nk_ref, nv_ref = rest[2:] if aliased else rest
    h = _hmod(x_ref[...], nw_ref, sh_ref, sc_ref)
    for cs, part, head0 in _qkv_chunks():
        u = _dot(h, w_ref[:, cs])
        qkv_ref[:, cs] = (u * QSCALE if part == 0 else u).astype(BF16)
        if part > 0:
            cache = nk_ref if part == 1 else nv_ref
            for s in range(TM // LP):
                for hh in range(HEADS_PER_CHUNK):
                    cache[s, head0 + hh] = u[s * LP:(s + 1) * LP, hh * V_DIM:(hh + 1) * V_DIM]


def _qkv_ctx(x, nw_all, mods, layer, j, w_bf, caches):
    seqs = TM // LP
    cache = pl.BlockSpec((seqs, None, N_HEADS, LP, V_DIM), lambda i: (i, j, 0, 0, 0))
    cache_shape = jax.ShapeDtypeStruct((BP, N_ATTN, N_HEADS, LP, V_DIM), F32)
    in_specs = [_tile_spec(D, _joint_tile), _norm_spec(layer, 0), _mod_spec(layer, 0),
                _mod_spec(layer, 1), _lay((D, 3 * D), j)]
    args = [x, nw_all, mods, mods, w_bf]
    aliases = {}
    if caches is not None:
        in_specs += [pl.BlockSpec(memory_space=pl.ANY)] * 2
        args += list(caches)
        aliases = {len(args) - 2: 1, len(args) - 1: 2}
    return pl.pallas_call(
        functools.partial(_qkv_c_kernel, caches is not None),
        grid=(N_CTX_TILES,),
        in_specs=in_specs,
        out_specs=[_tile_spec(3 * D, _joint_tile), cache, cache],
        out_shape=[jax.ShapeDtypeStruct((MP, 3 * D), BF16), cache_shape, cache_shape],
        input_output_aliases=aliases,
        compiler_params=_cp(("arbitrary",)),
        name="qkv_ctx",
    )(*args)


def _qkv_l_kernel(x_ref, nw_ref, sh_ref, sc_ref, w_ref, cos_ref, sa_ref, sb_ref, qkv_ref):
    h = _hmod(x_ref[...], nw_ref, sh_ref, sc_ref)
    for cs, part, _ in _qkv_chunks():
        u = _dot(h, w_ref[:, cs])
        if part == 2:
            qkv_ref[:, cs] = u.astype(BF16)
            continue
        for hh in range(HEADS_PER_CHUNK):
            r = _rope(u[:, hh * V_DIM:(hh + 1) * V_DIM], cos_ref[...], sa_ref[...], sb_ref[...])
            if part == 0:
                r = r * QSCALE
            qkv_ref[:, cs.start + hh * V_DIM:cs.start + (hh + 1) * V_DIM] = r.astype(BF16)


def _qkv_lat(x, nw_all, mods, layer, j, w_bf, ropes):
    tile = lambda i: i + N_CTX_TILES
    tab = pl.BlockSpec((TM, V_DIM), lambda i: (i % (LS // TM), 0))
    return pl.pallas_call(
        _qkv_l_kernel,
        grid=(N_LAT_TILES,),
        in_specs=[_tile_spec(D, tile), _norm_spec(layer, 0), _mod_spec(layer, 0, tile),
                  _mod_spec(layer, 1, tile), _lay((D, 3 * D), j), tab, tab, tab],
        out_specs=_tile_spec(3 * D, _joint_tile),
        out_shape=jax.ShapeDtypeStruct((MS, 3 * D), BF16),
        compiler_params=_cp(("arbitrary",)),
        name="qkv_lat",
    )(x, nw_all, mods, mods, w_bf, *ropes)


def _lambda(lv, lam_init):
    a = jnp.exp(jnp.sum(lv[0:1, :] * lv[1:2, :], axis=-1, keepdims=True))
    b = jnp.exp(jnp.sum(lv[2:3, :] * lv[3:4, :], axis=-1, keepdims=True))
    return a - b + lam_init


def _diff_attn(q, chunks, lam, lam_init, subln):
    t = q.shape[0]
    lane = lax.broadcasted_iota(jnp.int32, q.shape, 1)
    zero = jnp.zeros_like(q)
    q2 = jnp.concatenate([jnp.where(lane < HEAD_DIM, q, zero),
                          jnp.where(lane < HEAD_DIM, zero, q)], axis=0)
    m = l = acc = None
    for k, v in chunks:
        s = lax.dot_general(q2, k, (((1,), (1,)), ((), ())), preferred_element_type=F32)
        mc = jnp.max(s, axis=-1, keepdims=True)
        m_new = mc if m is None else jnp.maximum(m, mc)
        e = jnp.exp2(s - m_new)
        lc = jnp.sum(e, axis=-1, keepdims=True)
        pv = _dot(e.astype(BF16), v)
        if m is None:
            l, acc = lc, pv
        else:
            alpha = jnp.exp2(m - m_new)
            l = alpha * l + lc
            acc = alpha * acc + pv
        m = m_new
    o = acc[:t] * (1.0 / l[:t]) - acc[t:] * (lam / l[t:])
    return _rms(o, subln) * (1.0 - lam_init)


def _attn_c_kernel(lam_init, q_ref, k_ref, v_ref, lv_ref, sub_ref, o_ref):
    lam = _lambda(lv_ref[...], lam_init)
    t = q_ref.shape[0]
    heads = [slice(h * V_DIM, (h + 1) * V_DIM) for h in range(N_HEADS)]
    lane = lax.broadcasted_iota(jnp.int32, (t, V_DIM), 1)
    zero = jnp.zeros((t, V_DIM), BF16)
    dn = (((1,), (1,)), ((), ()))
    s = []
    for hs in heads:
        q = q_ref[:, hs]
        q2 = jnp.concatenate([jnp.where(lane < HEAD_DIM, q, zero),
                              jnp.where(lane < HEAD_DIM, zero, q)], axis=0)
        s.append(lax.dot_general(q2, k_ref[:, hs], dn, preferred_element_type=F32))
    e = [jnp.exp2(x - jnp.max(x, axis=-1, keepdims=True)) for x in s]
    l = [jnp.sum(x, axis=-1, keepdims=True) for x in e]
    pv = [_dot(x.astype(BF16), v_ref[:, hs]) for x, hs in zip(e, heads)]
    for hs, acc, lh in zip(heads, pv, l):
        o = acc[:t] * (1.0 / lh[:t]) - acc[t:] * (lam / lh[t:])
        o_ref[:, hs] = (_rms(o, sub_ref[...]) * (1.0 - lam_init)).astype(BF16)


def _attn_ctx(qkv_c, lamv, subln, j, lam_init):
    part = lambda p: pl.BlockSpec((LP, D), lambda b: (b, p))
    return pl.pallas_call(
        functools.partial(_attn_c_kernel, lam_init),
        grid=(BP,),
        in_specs=[part(0), part(1), part(2),
                  pl.BlockSpec((None, 4, HEAD_DIM), lambda b: (j, 0, 0)),
                  pl.BlockSpec((None, 1, V_DIM), lambda b: (j, 0, 0))],
        out_specs=pl.BlockSpec((LP, D), lambda b: (b, 0)),
        out_shape=jax.ShapeDtypeStruct((MP, D), BF16),
        compiler_params=_cp(("arbitrary",)),
        name="attn_ctx",
    )(qkv_c, qkv_c, qkv_c, lamv, subln)


def _attn_l_kernel(lam_init, q_ref, k_ref, v_ref, ck_ref, cv_ref, lv_ref, sub_ref, o_ref):
    chunks = [(ck_ref[...].astype(BF16), cv_ref[...].astype(BF16))]
    for c in range(LS // KEY_CHUNK):
        rows = pl.ds(c * KEY_CHUNK, KEY_CHUNK)
        chunks.append((k_ref[rows, :], v_ref[rows, :]))
    lam = _lambda(lv_ref[...], lam_init)
    o = _diff_attn(q_ref[...], chunks, lam, lam_init, sub_ref[...])
    o_ref[...] = o.astype(BF16)


def _attn_lat(qkv_l, cache_k, cache_v, lamv, subln, j, lam_init, tq=512):
    nq = LS // tq
    seq = lambda off: pl.BlockSpec((LS, V_DIM), lambda b, h, q: (b, off + h))
    ctx = pl.BlockSpec((None, None, None, PAST, V_DIM), lambda b, h, q: (b, j, h, 0, 0))
    return pl.pallas_call(
        functools.partial(_attn_l_kernel, lam_init),
        grid=(BS, N_HEADS, nq),
        in_specs=[pl.BlockSpec((tq, V_DIM), lambda b, h, q: (b * nq + q, h)),
                  seq(N_HEADS), seq(2 * N_HEADS), ctx, ctx,
                  pl.BlockSpec((None, 4, HEAD_DIM), lambda b, h, q: (j, 0, 0)),
                  pl.BlockSpec((None, 1, V_DIM), lambda b, h, q: (j, 0, 0))],
        out_specs=pl.BlockSpec((tq, V_DIM), lambda b, h, q: (b * nq + q, h)),
        out_shape=jax.ShapeDtypeStruct((MS, D), BF16),
        compiler_params=_cp(("arbitrary", "arbitrary", "arbitrary")),
        name="attn_lat",
    )(qkv_l, qkv_l, qkv_l, cache_k, cache_v, lamv, subln)


def kernel(x_prompt, x_sample, cache_k, cache_v, c, c_ctx, w_ada, b_ada, norm_w, hy_w_in, hy_b_in, hy_w_short, hy_b_short, hy_f_w1, hy_f_b1, hy_f_freq, hy_f_w2, hy_f_b2, hy_f_w3, hy_d_bias, hy_w_out, hy_b_out, at_w_qkv, at_w_out, at_lambda_q1, at_lambda_k1, at_lambda_q2, at_lambda_k2, at_subln, ffn_w_up, ffn_w_dw, ffn_b_dw, ffn_w_down):
    cond8 = jnp.concatenate([c_ctx[None, :], c, jnp.zeros((SUB - 1 - BS, D), F32)], axis=0)
    mods = _ada(cond8, w_ada, b_ada)

    fwd, inv, sgn = _dft_mats()
    tabs = _filter_tables()
    min_decay = math.log(DECAY_TARGET) / DECAY_PCT_LONG
    max_decay = math.log(DECAY_TARGET) / DECAY_PCT_SHORT
    absd = jnp.asarray(np.abs(np.linspace(min_decay, max_decay, D))[None, :], F32)
    ropes = _rope_tables(LS)

    row = lambda a: a.reshape(a.shape[0], 1, a.shape[1])
    nw_all = norm_w.reshape(DEPTH * 4, 1, D)
    w_in_bf, w_hy_out_bf = hy_w_in.astype(BF16), hy_w_out.astype(BF16)
    w_qkv_bf, w_at_out_bf = at_w_qkv.astype(BF16), at_w_out.astype(BF16)
    w_up_bf, w_down_bf = ffn_w_up.astype(BF16), ffn_w_down.astype(BF16)
    w1_pad = jnp.pad(hy_f_w1, ((0, 0), (0, FILTER_HIDDEN - EMB_DIM), (0, 0)))
    lamv = jnp.stack([at_lambda_q1, at_lambda_k1, at_lambda_q2, at_lambda_k2], axis=1)
    subln = row(at_subln)

    x_parts = [x_prompt.reshape(MP, D), x_sample.reshape(MS, D)]
    caches = None
    for i in range(DEPTH):
        j = i // 2
        if i % 2 == 0:
            x0, z = _hy_in(x_parts, nw_all, mods, i, j, w_in_bf, row(hy_b_in), hy_w_short,
                           row(hy_b_short))
            spectra = _filter_spectra(tabs, fwd, sgn, j, w1_pad, row(hy_f_b1), row(hy_f_freq),
                                      hy_f_w2, row(hy_f_b2), hy_f_w3, absd)
            a_parts = [_long_conv(z, x0, spectra, fwd, inv, row(hy_d_bias), j)]
            w_out, b_out = w_hy_out_bf, row(hy_b_out)
        else:
            lam_init = 0.8 - 0.6 * math.exp(-0.3 * i)
            x = x_parts[0]
            qkv_c, new_k, new_v = _qkv_ctx(x, nw_all, mods, i, j, w_qkv_bf, caches)
            caches = (new_k, new_v)
            qkv_l = _qkv_lat(x, nw_all, mods, i, j, w_qkv_bf, ropes)
            a_parts = [_attn_ctx(qkv_c, lamv, subln, j, lam_init),
                       _attn_lat(qkv_l, cache_k, cache_v, lamv, subln, j, lam_init)]
            w_out, b_out = w_at_out_bf, None
        x = _proj_out(a_parts, x_parts, w_out, b_out, nw_all, mods, i, j)
        y = _ffn(x, nw_all, mods, i, w_up_bf, ffn_w_dw, row(ffn_b_dw), w_down_bf,
                 split_out=(i == DEPTH - 1))
        x_parts = list(y) if i == DEPTH - 1 else [y]

    return (x_parts[0].reshape(BP, LP, D), x_parts[1].reshape(BS, LS, D), caches[0], caches[1])
```

```python
import functools
import math

import numpy as np
import jax
import jax.numpy as jnp
from jax import lax
from jax.experimental import pallas as pl
from jax.experimental.pallas import tpu as pltpu

D = 1024
BP, LP = 16, 256
BS, LS = 2, 2048
MP = BP * LP
MS = BS * LS
M = MP + MS
DEPTH = 4
N_ATTN = DEPTH // 2
GRID_W = 64
N_HEADS = 8
HEAD_DIM = 64
V_DIM = 128
ROPE_THETA = 10000.0
ROT_FREQS = 16
EMB_BANDS = 16
EMB_DIM = 33
FILTER_HIDDEN = 64
DECAY_TARGET = 1e-2
DECAY_PCT_SHORT = 0.3
DECAY_PCT_LONG = 1.5
D_FF = 2816
EPS = 1e-6
PAST = 256

CB = 256
NB_S = LS // CB
N_CHUNK = 2 + 2 * NB_S
SUB = 8
HB = 16
HALO = 2 * HB
SLAB = 16
CK = 256
TM = 512
KEY_CHUNK = 512
VMEM_LIMIT = 56 * 1024 * 1024
QSCALE = HEAD_DIM ** -0.5 * math.log2(math.e)

F32 = jnp.float32
BF16 = jnp.bfloat16


def _dot(a, b):
    return jnp.dot(a, b, preferred_element_type=F32)


def _rms(x, w):
    ms = jnp.mean(x * x, axis=-1, keepdims=True)
    return x * lax.rsqrt(ms + EPS) * w


def _silu(x):
    return x / (1.0 + jnp.exp(-x))


def _cp(sem, vmem=VMEM_LIMIT):
    return pltpu.CompilerParams(dimension_semantics=sem, vmem_limit_bytes=vmem)


def _lay(shape, idx):
    return pl.BlockSpec((None,) + tuple(shape), lambda i: (idx,) + (0,) * len(shape),
                        pipeline_mode=pl.Buffered(1))


def _norm_spec(layer, k):
    return _lay((1, D), layer * 4 + k)


N_CTX_TILES = MP // TM
N_LAT_TILES = MS // TM


def _ctx_tile(i):
    return jnp.minimum(i, N_CTX_TILES - 1)


def _lat_tile(i):
    return jnp.maximum(i - N_CTX_TILES, 0)


def _joint_tile(i):
    return i


def _is_ctx():
    return pl.program_id(0) < N_CTX_TILES


def _tile_spec(width, tile_of):
    return pl.BlockSpec((TM, width), lambda i: (tile_of(i), 0))


def _ext_specs(n_rows, tile_of, width=D):
    r = TM // HB
    last_blk = n_rows // HB - 1
    return [
        pl.BlockSpec((TM, width), lambda i: (tile_of(i), 0)),
        pl.BlockSpec((HB, width), lambda i: (jnp.maximum(tile_of(i) * r - 1, 0), 0)),
        pl.BlockSpec((HB, width), lambda i: (jnp.minimum((tile_of(i) + 1) * r, last_blk), 0)),
    ]


def _fill_ext(scr, t_ref, p_ref, n_ref):
    scr[0:TM, :] = t_ref[...]
    scr[TM:TM + HB, :] = p_ref[...]
    scr[TM + HB:TM + HALO, :] = n_ref[...]


def _per_part(n_parts, fn, refs):
    if n_parts == 1:
        fn(*refs)
        return
    k = len(refs) // 2
    pl.when(_is_ctx())(lambda: fn(*refs[:k]))
    pl.when(jnp.logical_not(_is_ctx()))(lambda: fn(*refs[k:]))


def _ada_kernel(c_ref, w_ref, b_ref, o_ref):
    s = _silu(c_ref[...]).astype(BF16)
    o_ref[...] = _dot(s, w_ref[...].astype(BF16)) + b_ref[...]


def _ada(cond8, w_ada, b_ada):
    out = pl.pallas_call(
        _ada_kernel,
        grid=(DEPTH, 6),
        in_specs=[
            pl.BlockSpec((SUB, D), lambda l, k: (0, 0)),
            pl.BlockSpec((None, D, D), lambda l, k: (l, 0, k)),
            pl.BlockSpec((None, 1, D), lambda l, k: (l, 0, k)),
        ],
        out_specs=pl.BlockSpec((None, SUB, D), lambda l, k: (l * 6 + k, 0, 0)),
        out_shape=jax.ShapeDtypeStruct((DEPTH * 6, SUB, D), F32),
        compiler_params=_cp(("arbitrary", "arbitrary")),
        name="ada",
    )(cond8, w_ada, b_ada.reshape(DEPTH, 1, 6 * D))
    return out.reshape(DEPTH * 6 * SUB, 1, D)


def _mod_spec(layer, which, tile_of=_joint_tile):
    base = (layer * 6 + which) * SUB
    per_b = LS // TM

    def imap(i):
        t = tile_of(i)
        r = jnp.where(t < N_CTX_TILES, 0, 1 + (t - N_CTX_TILES) // per_b)
        return (base + r, 0, 0)

    return pl.BlockSpec((None, 1, D), imap)


def _hmod(x, nw_ref, sh_ref, sc_ref):
    return (_rms(x, nw_ref[...]) * (1.0 + sc_ref[...]) + sh_ref[...]).astype(BF16)


def _fill_h(h_scr, nw_ref, sh_ref, sc_ref, x_ref, xp_ref, xn_ref):
    h_scr[0:TM, :] = _hmod(x_ref[...], nw_ref, sh_ref, sc_ref)
    h_scr[TM:TM + HB, :] = _hmod(xp_ref[...], nw_ref, sh_ref, sc_ref)
    h_scr[TM + HB:TM + HALO, :] = _hmod(xn_ref[...], nw_ref, sh_ref, sc_ref)


def _tile_flags():
    i = pl.program_id(0)
    is_ctx = _is_ctx()
    lseq = jnp.where(is_ctx, LP, LS)
    starts = ((i * TM) & (lseq - 1)) == 0
    ends = (((i + 1) * TM) & (lseq - 1)) == 0
    return is_ctx, starts, ends


def _edge_slabs():
    return sorted({b for b in range(0, TM, LP)} | {b + LP - SLAB for b in range(0, TM, LP)})


def _conv3_bulk(u, w, b):
    return (pltpu.roll(u, 1, axis=0) * w[0:1, :] + u * w[1:2, :]
            + pltpu.roll(u, TM - 1, axis=0) * w[2:3, :] + b)


def _conv3_slab(u_ext, s, w, b, flags):
    is_ctx, starts, ends = flags
    us = u_ext[s:s + SLAB, :]
    if s == 0:
        prev = jnp.where(starts, 0.0, u_ext[TM + HB - 1:TM + HB, :])
    else:
        prev = u_ext[s - 1:s, :]
        if s % LP == 0:
            prev = jnp.where(is_ctx, 0.0, prev)
    if s + SLAB == TM:
        nxt = jnp.where(ends, 0.0, u_ext[TM + HB:TM + HB + 1, :])
    else:
        nxt = u_ext[s + SLAB:s + SLAB + 1, :]
        if (s + SLAB) % LP == 0:
            nxt = jnp.where(is_ctx, 0.0, nxt)
    rows = lax.broadcasted_iota(jnp.int32, us.shape, 0)
    up = jnp.where(rows == 0, prev, pltpu.roll(us, 1, axis=0))
    dn = jnp.where(rows == SLAB - 1, nxt, pltpu.roll(us, SLAB - 1, axis=0))
    return up * w[0:1, :] + us * w[1:2, :] + dn * w[2:3, :] + b


def _tail_kernel(n_a, n_x, has_bias, split_out, *refs):
    a_refs, x_refs = refs[:3 * n_a], refs[3 * n_a:3 * (n_a + n_x)]
    rest = list(refs[3 * (n_a + n_x):])
    w_ref = rest.pop(0)
    b_ref = rest.pop(0) if has_bias else None
    (nw1_ref, g1_ref, nw2_ref, sh_ref, sc_ref, wup_ref, wdw_ref, bdw_ref, wdn_ref, nw3_ref,
     g2_ref) = rest[:11]
    n_out = 2 if split_out else 1
    out_refs = rest[11:11 + n_out]
    a_scr, x_scr, h_scr, act_scr = rest[11 + n_out:]

    _per_part(n_a, functools.partial(_fill_ext, a_scr), a_refs)
    _per_part(n_x, functools.partial(_fill_ext, x_scr), x_refs)
    y = _dot(a_scr[...], w_ref[...])
    if has_bias:
        y = y + b_ref[...]
    x1 = x_scr[...] + g1_ref[...] * _rms(y, nw1_ref[...])
    x_scr[...] = x1
    h_scr[...] = _hmod(x1, nw2_ref, sh_ref, sc_ref)

    flags = _tile_flags()
    h = h_scr[...]
    for c in range(D_FF // CK):
        cg = slice(c * CK, (c + 1) * CK)
        cv = slice(D_FF + c * CK, D_FF + (c + 1) * CK)
        g_ext = _dot(h, wup_ref[:, cg])
        v_ext = _dot(h, wup_ref[:, cv])
        wg, bg, wv, bv = wdw_ref[:, cg], bdw_ref[:, cg], wdw_ref[:, cv], bdw_ref[:, cv]
        g = _conv3_bulk(g_ext[0:TM, :], wg, bg)
        val = _conv3_bulk(v_ext[0:TM, :], wv, bv)
        act_scr[:, cg] = (_silu(g) * val).astype(BF16)
        for s in _edge_slabs():
            g = _conv3_slab(g_ext, s, wg, bg, flags)
            val = _conv3_slab(v_ext, s, wv, bv, flags)
            act_scr[s:s + SLAB, cg] = (_silu(g) * val).astype(BF16)
    y = _dot(act_scr[...], wdn_ref[...])
    res = x_scr[0:TM, :] + g2_ref[...] * _rms(y, nw3_ref[...])
    if split_out:
        is_ctx = flags[0]

        @pl.when(is_ctx)
        def _():
            out_refs[0][...] = res

        @pl.when(jnp.logical_not(is_ctx))
        def _():
            out_refs[1][...] = res
    else:
        out_refs[0][...] = res


def _tail(a_parts, x_parts, nw_all, mods, layer, j, w_out_bf, b_out, wup_bf, w_dw, b_dw, wdn_bf,
          split_out):
    k = w_out_bf.shape[1]
    has_bias = b_out is not None

    def specs(parts, width):
        if len(parts) == 1:
            return _ext_specs(M, _joint_tile, width)
        return _ext_specs(MP, _ctx_tile, width) + _ext_specs(MS, _lat_tile, width)

    in_specs = specs(a_parts, k) + specs(x_parts, D) + [_lay((k, D), j)]
    args = [p for p in a_parts for _ in range(3)] + [p for p in x_parts for _ in range(3)]
    args.append(w_out_bf)
    if has_bias:
        in_specs.append(_lay((1, D), j))
        args.append(b_out)
    in_specs += [
        _norm_spec(layer, 1), _mod_spec(layer, 2),
        _norm_spec(layer, 2), _mod_spec(layer, 3), _mod_spec(layer, 4),
        _lay((D, 2 * D_FF), layer), _lay((3, 2 * D_FF), layer), _lay((1, 2 * D_FF), layer),
        _lay((D_FF, D), layer),
        _norm_spec(layer, 3), _mod_spec(layer, 5),
    ]
    args += [nw_all, mods, nw_all, mods, mods, wup_bf, w_dw, b_dw, wdn_bf, nw_all, mods]
    if split_out:
        out_specs = [_tile_spec(D, _ctx_tile), _tile_spec(D, _lat_tile)]
        out_shape = [jax.ShapeDtypeStruct((MP, D), F32), jax.ShapeDtypeStruct((MS, D), F32)]
    else:
        out_specs = _tile_spec(D, _joint_tile)
        out_shape = jax.ShapeDtypeStruct((M, D), F32)
    return pl.pallas_call(
        functools.partial(_tail_kernel, len(a_parts), len(x_parts), has_bias, split_out),
        grid=(M // TM,),
        in_specs=in_specs,
        out_specs=out_specs,
        out_shape=out_shape,
        scratch_shapes=[pltpu.VMEM((TM + HALO, k), BF16), pltpu.VMEM((TM + HALO, D), F32),
                        pltpu.VMEM((TM + HALO, D), BF16), pltpu.VMEM((TM, D_FF), BF16)],
        compiler_params=_cp(("arbitrary",)),
        name="tail",
    )(*args)


def _hy_in_kernel(n_parts, *refs):
    x_refs = refs[:3 * n_parts]
    (nw_ref, sh_ref, sc_ref, w_ref, b_ref, ws_ref, bs_ref, x0_ref, z_ref,
     h_scr) = refs[3 * n_parts:]
    _per_part(n_parts, functools.partial(_fill_h, h_scr, nw_ref, sh_ref, sc_ref), x_refs)
    flags = _tile_flags()
    h = h_scr[...]
    for c in range(D // CK):
        cc = slice(c * CK, (c + 1) * CK)
        u_ext, ws, bs = [], [], []
        for s in range(3):
            cs = slice(s * D + c * CK, s * D + (c + 1) * CK)
            u_ext.append(_dot(h, w_ref[:, cs]) + b_ref[:, cs])
            ws.append(ws_ref[:, cs])
            bs.append(bs_ref[:, cs])
        out = [_conv3_bulk(u_ext[s][0:TM, :], ws[s], bs[s]) for s in range(3)]
        x0_ref[:, cc] = out[0].astype(BF16)
        z_ref[:, cc] = (out[2] * out[1]).astype(BF16)
        for r in _edge_slabs():
            out = [_conv3_slab(u_ext[s], r, ws[s], bs[s], flags) for s in range(3)]
            x0_ref[r:r + SLAB, cc] = out[0].astype(BF16)
            z_ref[r:r + SLAB, cc] = (out[2] * out[1]).astype(BF16)


def _hy_in(x_parts, nw_all, mods, layer, j, w_bf, b_in, w_short, b_short):
    if len(x_parts) == 1:
        x_specs = _ext_specs(M, _joint_tile)
    else:
        x_specs = _ext_specs(MP, _ctx_tile) + _ext_specs(MS, _lat_tile)
    x_args = [a for a in x_parts for _ in range(3)]
    return pl.pallas_call(
        functools.partial(_hy_in_kernel, len(x_parts)),
        grid=(M // TM,),
        in_specs=x_specs + [
            _norm_spec(layer, 0), _mod_spec(layer, 0), _mod_spec(layer, 1),
            _lay((D, 3 * D), j), _lay((1, 3 * D), j), _lay((3, 3 * D), j), _lay((1, 3 * D), j),
        ],
        out_specs=[_tile_spec(D, _joint_tile)] * 2,
        out_shape=[jax.ShapeDtypeStruct((M, D), BF16)] * 2,
        scratch_shapes=[pltpu.VMEM((TM + HALO, D), BF16)],
        compiler_params=_cp(("arbitrary",)),
        name="hy_in",
    )(*x_args, nw_all, mods, mods, w_bf, b_in, w_short, b_short)


def _dft_mats():
    n = np.arange(CB, dtype=np.float64)
    f = np.arange(CB, dtype=np.float64)
    ang = 2.0 * np.pi * np.outer(f, n) / (2 * CB)
    fwd = np.concatenate([np.cos(ang), -np.sin(ang)], axis=0)
    fwd[CB] = np.cos(np.pi * n)
    scale = np.full((2 * CB, 1), 2.0 / (2 * CB))
    scale[0] = scale[CB] = 1.0 / (2 * CB)
    inv = (fwd * scale).T
    sgn = np.where(np.arange(CB) % 2 == 0, 1.0, -1.0)
    sgn2 = np.concatenate([sgn, sgn])[:, None]
    sgn2[CB] = 1.0
    return (jnp.asarray(fwd, F32).astype(BF16), jnp.asarray(inv, F32).astype(BF16),
            jnp.asarray(sgn2, F32))


def _filter_features(L):
    t = np.linspace(0.0, 1.0, L)[:, None]
    w = 2.0 * np.pi * np.arange(L)[:, None] / L
    bands = np.linspace(1e-4, EMB_BANDS - 1, EMB_BANDS)
    z = np.concatenate([t, np.cos(bands * w), -np.sin(bands * w)], axis=-1)
    return t, z


def _filter_tables():
    zs, ts, ms = [], [], []
    for L in (LP, LS):
        t, z = _filter_features(L)
        z = np.pad(z, ((0, 0), (0, FILTER_HIDDEN - EMB_DIM)))
        idx = np.abs(np.arange(2 * L) - L) % L
        zs.append(z[idx])
        ts.append(t[idx])
        ms.append((np.arange(2 * L) != 0).astype(np.float64)[:, None])
    return tuple(jnp.asarray(np.concatenate(a), F32) for a in (zs, ts, ms))


N_WIN = 1 + 2 * NB_S - 1


def _filt_kernel(z_ref, t_ref, m_ref, w1_ref, b1_ref, fr_ref, w2_ref, b2_ref, w3_ref,
                 ad_ref, f_ref, sg_ref, g_ref, nyq_ref, prev_scr):
    c = pl.program_id(0)

    @pl.when(c == 0)
    def _():
        prev_scr[...] = jnp.zeros_like(prev_scr)

    fr = fr_ref[...]
    hid = jnp.sin(fr * (_dot(z_ref[...].astype(BF16), w1_ref[...].astype(BF16)) + b1_ref[...]))
    hid = jnp.sin(fr * (_dot(hid.astype(BF16), w2_ref[...].astype(BF16)) + b2_ref[...]))
    h = _dot(hid.astype(BF16), w3_ref[...].astype(BF16))
    taps = h * jnp.exp(-t_ref[...] * ad_ref[...]) * m_ref[...]
    a = _dot(f_ref[...], taps.astype(BF16))
    g = a + sg_ref[...] * prev_scr[...]
    prev_scr[...] = a
    g_ref[...] = g
    nyq_ref[...] = g[CB:CB + 1, :]
    g_ref[CB:CB + 1, :] = jnp.zeros((1, D), F32)


def _is_bwd_chunk(c):
    return jnp.logical_or(c == 0, jnp.logical_and(c >= 2, c < 2 + NB_S))


def _window_of_chunk(c):
    return jnp.where(c <= 1, 0, jnp.maximum(c - 2, 1))


def _filter_spectra(tabs, fwd, sgn, j, w1, b1, freq, w2, b2, w3, absd):
    z_tab, t_tab, m_tab = tabs
    small = lambda shape: pl.BlockSpec(shape, lambda c: (0, 0))
    hid = _lay((1, FILTER_HIDDEN), j)
    return pl.pallas_call(
        _filt_kernel,
        grid=(N_CHUNK,),
        in_specs=[
            pl.BlockSpec((CB, FILTER_HIDDEN), lambda c: (c, 0)),
            pl.BlockSpec((CB, 1), lambda c: (c, 0)),
            pl.BlockSpec((CB, 1), lambda c: (c, 0)),
            _lay((FILTER_HIDDEN, FILTER_HIDDEN), j), hid, hid,
            _lay((FILTER_HIDDEN, FILTER_HIDDEN), j), hid,
            pl.BlockSpec((None, FILTER_HIDDEN, D),
                         lambda c: (j, 0, jnp.where(_is_bwd_chunk(c), 1, 0))),
            small((1, D)),
            small((2 * CB, CB)),
            small((2 * CB, 1)),
        ],
        out_specs=[pl.BlockSpec((None, 2 * CB, D), lambda c: (_window_of_chunk(c), 0, 0)),
                   pl.BlockSpec((None, 1, D), lambda c: (_window_of_chunk(c), 0, 0))],
        out_shape=[jax.ShapeDtypeStruct((N_WIN, 2 * CB, D), F32),
                   jax.ShapeDtypeStruct((N_WIN, 1, D), F32)],
        scratch_shapes=[pltpu.VMEM((2 * CB, D), F32)],
        compiler_params=_cp(("arbitrary",)),
        name="hy_filter",
    )(z_tab, t_tab, m_tab, w1, b1, freq, w2, b2, w3, absd, fwd, sgn)


def _conv_kernel(tc, z_ref, x0_ref, g_ref, nyq_ref, f_ref, gi_ref, db_ref, o_ref, u_scr):
    unit = pl.program_id(1)
    fmat = f_ref[...]
    gmat = gi_ref[...]
    row0 = lax.broadcasted_iota(jnp.int32, (CB, tc), 0) == 0
    db = db_ref[...]

    def emit(blk, yre, yim, ynyq):
        rows = pl.ds(blk * CB, CB)
        yspec = jnp.concatenate([yre, jnp.where(row0, ynyq, yim)], axis=0)
        y = _dot(gmat, yspec.astype(BF16))
        o_ref[rows, :] = (x0_ref[rows, :] * (y + z_ref[rows, :] * db)).astype(BF16)

    @pl.when(unit < MP // LS)
    def _():
        gre, gim, gnyq = g_ref[0, :CB, :], g_ref[0, CB:, :], nyq_ref[0]
        for s in range(LS // LP):
            u = _dot(fmat, z_ref[pl.ds(s * CB, CB), :])
            ure, unyq = u[:CB], u[CB:CB + 1]
            uim = jnp.where(row0, 0.0, u[CB:])
            emit(s, gre * ure - gim * uim, gre * uim + gim * ure, gnyq * unyq)

    @pl.when(unit >= MP // LS)
    def _():
        unyq = []
        for j in range(NB_S):
            u = _dot(fmat, z_ref[pl.ds(j * CB, CB), :])
            unyq.append(u[CB:CB + 1])
            u_scr[j] = u
            u_scr[j, CB:CB + 1, :] = jnp.zeros((1, tc), F32)
        for i in range(NB_S):
            yre = jnp.zeros((CB, tc), F32)
            yim = jnp.zeros((CB, tc), F32)
            ynyq = jnp.zeros((1, tc), F32)
            for j in range(NB_S):
                w = NB_S + i - j
                gre, gim = g_ref[w, :CB, :], g_ref[w, CB:, :]
                ure, uim = u_scr[j, :CB, :], u_scr[j, CB:, :]
                yre = yre + gre * ure - gim * uim
                yim = yim + gre * uim + gim * ure
                ynyq = ynyq + nyq_ref[w] * unyq[j]
            emit(i, yre, yim, ynyq)


def _long_conv(z, x0, spectra, fwd, inv, d_bias, j, tc=256):
    g_win, g_nyq = spectra
    return pl.pallas_call(
        functools.partial(_conv_kernel, tc),
        grid=(D // tc, M // LS),
        in_specs=[
            pl.BlockSpec((LS, tc), lambda c, u: (u, c)),
            pl.BlockSpec((LS, tc), lambda c, u: (u, c)),
            pl.BlockSpec((N_WIN, 2 * CB, tc), lambda c, u: (0, 0, c)),
            pl.BlockSpec((N_WIN, 1, tc), lambda c, u: (0, 0, c)),
            pl.BlockSpec((2 * CB, CB), lambda c, u: (0, 0)),
            pl.BlockSpec((CB, 2 * CB), lambda c, u: (0, 0)),
            pl.BlockSpec((None, 1, tc), lambda c, u: (j, 0, c)),
        ],
        out_specs=pl.BlockSpec((LS, tc), lambda c, u: (u, c)),
        out_shape=jax.ShapeDtypeStruct((M, D), BF16),
        scratch_shapes=[pltpu.VMEM((NB_S, 2 * CB, tc), F32)],
        compiler_params=_cp(("arbitrary", "arbitrary")),
        name="hy_conv",
    )(z, x0, g_win, g_nyq, fwd, inv, d_bias)


def _rope_tables(L):
    rows = L // GRID_W
    r = np.repeat(np.arange(rows, dtype=np.float64), GRID_W)
    cidx = np.tile(np.arange(GRID_W, dtype=np.float64), rows)
    inv = ROPE_THETA ** (-np.arange(ROT_FREQS, dtype=np.float64) / ROT_FREQS)
    ar = r[:, None] * inv
    ac = cidx[:, None] * inv
    cos = np.concatenate([np.cos(ar), np.cos(ar), np.cos(ac), np.cos(ac)] * 2, axis=-1)
    sin = np.concatenate([np.sin(ar), np.sin(ar), np.sin(ac), np.sin(ac)] * 2, axis=-1)
    first_half = (np.arange(V_DIM) % (2 * ROT_FREQS)) < ROT_FREQS
    sin_a = np.where(first_half, -sin, 0.0)
    sin_b = np.where(first_half, 0.0, sin)
    return tuple(jnp.asarray(a, F32) for a in (cos, sin_a, sin_b))


def _rope(x, cos, sin_a, sin_b):
    return (x * cos + pltpu.roll(x, V_DIM - ROT_FREQS, axis=1) * sin_a
            + pltpu.roll(x, ROT_FREQS, axis=1) * sin_b)


HEADS_PER_CHUNK = CK // V_DIM


def _qkv_chunks():
    per_part = D // CK
    return [(slice(c * CK, (c + 1) * CK), c // per_part, (c % per_part) * HEADS_PER_CHUNK)
            for c in range(3 * per_part)]


def _qkv_c_kernel(last, x_ref, nw_ref, sh_ref, sc_ref, w_ref, *rest):
    if last:
        pk_ref, pv_ref, qkv_ref, nk_ref, nv_ref = rest
        nk_ref[:, 0] = pk_ref[...]
        nv_ref[:, 0] = pv_ref[...]
    else:
        qkv_ref, nk_ref, nv_ref = rest
    h = _hmod(x_ref[...], nw_ref, sh_ref, sc_ref)
    for cs, part, head0 in _qkv_chunks():
        u = _dot(h, w_ref[:, cs])
        qkv_ref[:, cs] = (u * QSCALE if part == 0 else u).astype(BF16)
        if part > 0:
            cache = nk_ref if part == 1 else nv_ref
            for s in range(TM // LP):
                for hh in range(HEADS_PER_CHUNK):
                    blk = u[s * LP:(s + 1) * LP, hh * V_DIM:(hh + 1) * V_DIM]
                    if last:
                        cache[s, N_ATTN - 1, head0 + hh] = blk
                    else:
                        cache[s, head0 + hh] = blk


def _qkv_ctx(x, nw_all, mods, layer, j, w_bf, prev):
    seqs = TM // LP
    one = pl.BlockSpec((seqs, N_HEADS, LP, V_DIM), lambda i: (i, 0, 0, 0))
    one_shape = jax.ShapeDtypeStruct((BP, N_HEADS, LP, V_DIM), F32)
    in_specs = [_tile_spec(D, _joint_tile), _norm_spec(layer, 0), _mod_spec(layer, 0),
                _mod_spec(layer, 1), _lay((D, 3 * D), j)]
    args = [x, nw_all, mods, mods, w_bf]
    if prev is None:
        cache, cache_shape = one, one_shape
    else:
        assert j == N_ATTN - 1 == 1
        in_specs += [one, one]
        args += list(prev)
        cache = pl.BlockSpec((seqs, N_ATTN, N_HEADS, LP, V_DIM), lambda i: (i, 0, 0, 0, 0))
        cache_shape = jax.ShapeDtypeStruct((BP, N_ATTN, N_HEADS, LP, V_DIM), F32)
    return pl.pallas_call(
        functools.partial(_qkv_c_kernel, prev is not None),
        grid=(N_CTX_TILES,),
        in_specs=in_specs,
        out_specs=[_tile_spec(3 * D, _joint_tile), cache, cache],
        out_shape=[jax.ShapeDtypeStruct((MP, 3 * D), BF16), cache_shape, cache_shape],
        compiler_params=_cp(("arbitrary",)),
        name="qkv_ctx",
    )(*args)


def _qkv_l_kernel(x_ref, nw_ref, sh_ref, sc_ref, w_ref, cos_ref, sa_ref, sb_ref, qkv_ref):
    h = _hmod(x_ref[...], nw_ref, sh_ref, sc_ref)
    for cs, part, _ in _qkv_chunks():
        u = _dot(h, w_ref[:, cs])
        if part == 2:
            qkv_ref[:, cs] = u.astype(BF16)
            continue
        for hh in range(HEADS_PER_CHUNK):
            r = _rope(u[:, hh * V_DIM:(hh + 1) * V_DIM], cos_ref[...], sa_ref[...], sb_ref[...])
            if part == 0:
                r = r * QSCALE
            qkv_ref[:, cs.start + hh * V_DIM:cs.start + (hh + 1) * V_DIM] = r.astype(BF16)


def _qkv_lat(x, nw_all, mods, layer, j, w_bf, ropes):
    tile = lambda i: i + N_CTX_TILES
    tab = pl.BlockSpec((TM, V_DIM), lambda i: (i % (LS // TM), 0))
    return pl.pallas_call(
        _qkv_l_kernel,
        grid=(N_LAT_TILES,),
        in_specs=[_tile_spec(D, tile), _norm_spec(layer, 0), _mod_spec(layer, 0, tile),
                  _mod_spec(layer, 1, tile), _lay((D, 3 * D), j), tab, tab, tab],
        out_specs=_tile_spec(3 * D, _joint_tile),
        out_shape=jax.ShapeDtypeStruct((MS, 3 * D), BF16),
        compiler_params=_cp(("arbitrary",)),
        name="qkv_lat",
    )(x, nw_all, mods, mods, w_bf, *ropes)


def _lambda(lv, lam_init):
    a = jnp.exp(jnp.sum(lv[0:1, :] * lv[1:2, :], axis=-1, keepdims=True))
    b = jnp.exp(jnp.sum(lv[2:3, :] * lv[3:4, :], axis=-1, keepdims=True))
    return a - b + lam_init


def _diff_attn(q, chunks, lam, lam_init, subln):
    t = q.shape[0]
    lane = lax.broadcasted_iota(jnp.int32, q.shape, 1)
    zero = jnp.zeros_like(q)
    q2 = jnp.concatenate([jnp.where(lane < HEAD_DIM, q, zero),
                          jnp.where(lane < HEAD_DIM, zero, q)], axis=0)
    m = l = acc = None
    for k, v in chunks:
        s = lax.dot_general(q2, k, (((1,), (1,)), ((), ())), preferred_element_type=F32)
        mc = jnp.max(s, axis=-1, keepdims=True)
        m_new = mc if m is None else jnp.maximum(m, mc)
        e = jnp.exp2(s - m_new)
        lc = jnp.sum(e, axis=-1, keepdims=True)
        pv = _dot(e.astype(BF16), v)
        if m is None:
            l, acc = lc, pv
        else:
            alpha = jnp.exp2(m - m_new)
            l = alpha * l + lc
            acc = alpha * acc + pv
        m = m_new
    o = acc[:t] * (1.0 / l[:t]) - acc[t:] * (lam / l[t:])
    return _rms(o, subln) * (1.0 - lam_init)


def _attn_c_kernel(lam_init, q_ref, k_ref, v_ref, lv_ref, sub_ref, o_ref):
    lam = _lambda(lv_ref[...], lam_init)
    t = q_ref.shape[0]
    heads = [slice(h * V_DIM, (h + 1) * V_DIM) for h in range(N_HEADS)]
    lane = lax.broadcasted_iota(jnp.int32, (t, V_DIM), 1)
    zero = jnp.zeros((t, V_DIM), BF16)
    dn = (((1,), (1,)), ((), ()))
    s = []
    for hs in heads:
        q = q_ref[:, hs]
        q2 = jnp.concatenate([jnp.where(lane < HEAD_DIM, q, zero),
                              jnp.where(lane < HEAD_DIM, zero, q)], axis=0)
        s.append(lax.dot_general(q2, k_ref[:, hs], dn, preferred_element_type=F32))
    e = [jnp.exp2(x - jnp.max(x, axis=-1, keepdims=True)) for x in s]
    l = [jnp.sum(x, axis=-1, keepdims=True) for x in e]
    pv = [_dot(x.astype(BF16), v_ref[:, hs]) for x, hs in zip(e, heads)]
    for hs, acc, lh in zip(heads, pv, l):
        o = acc[:t] * (1.0 / lh[:t]) - acc[t:] * (lam / lh[t:])
        o_ref[:, hs] = (_rms(o, sub_ref[...]) * (1.0 - lam_init)).astype(BF16)


def _attn_ctx(qkv_c, lamv, subln, j, lam_init):
    part = lambda p: pl.BlockSpec((LP, D), lambda b: (b, p))
    return pl.pallas_call(
        functools.partial(_attn_c_kernel, lam_init),
        grid=(BP,),
        in_specs=[part(0), part(1), part(2),
                  pl.BlockSpec((None, 4, HEAD_DIM), lambda b: (j, 0, 0)),
                  pl.BlockSpec((None, 1, V_DIM), lambda b: (j, 0, 0))],
        out_specs=pl.BlockSpec((LP, D), lambda b: (b, 0)),
        out_shape=jax.ShapeDtypeStruct((MP, D), BF16),
        compiler_params=_cp(("arbitrary",)),
        name="attn_ctx",
    )(qkv_c, qkv_c, qkv_c, lamv, subln)


def _attn_l_kernel(lam_init, q_ref, k_ref, v_ref, ck_ref, cv_ref, lv_ref, sub_ref, o_ref):
    chunks = [(ck_ref[...].astype(BF16), cv_ref[...].astype(BF16))]
    for c in range(LS // KEY_CHUNK):
        rows = pl.ds(c * KEY_CHUNK, KEY_CHUNK)
        chunks.append((k_ref[rows, :], v_ref[rows, :]))
    lam = _lambda(lv_ref[...], lam_init)
    o = _diff_attn(q_ref[...], chunks, lam, lam_init, sub_ref[...])
    o_ref[...] = o.astype(BF16)


def _attn_lat(qkv_l, cache_k, cache_v, lamv, subln, j, lam_init, tq=512):
    nq = LS // tq
    seq = lambda off: pl.BlockSpec((LS, V_DIM), lambda b, h, q: (b, off + h))
    ctx = pl.BlockSpec((None, None, None, PAST, V_DIM), lambda b, h, q: (b, j, h, 0, 0))
    return pl.pallas_call(
        functools.partial(_attn_l_kernel, lam_init),
        grid=(BS, N_HEADS, nq),
        in_specs=[pl.BlockSpec((tq, V_DIM), lambda b, h, q: (b * nq + q, h)),
                  seq(N_HEADS), seq(2 * N_HEADS), ctx, ctx,
                  pl.BlockSpec((None, 4, HEAD_DIM), lambda b, h, q: (j, 0, 0)),
                  pl.BlockSpec((None, 1, V_DIM), lambda b, h, q: (j, 0, 0))],
        out_specs=pl.BlockSpec((tq, V_DIM), lambda b, h, q: (b * nq + q, h)),
        out_shape=jax.ShapeDtypeStruct((MS, D), BF16),
        compiler_params=_cp(("arbitrary", "arbitrary", "arbitrary")),
        name="attn_lat",
    )(qkv_l, qkv_l, qkv_l, cache_k, cache_v, lamv, subln)


def kernel(x_prompt, x_sample, cache_k, cache_v, c, c_ctx, w_ada, b_ada, norm_w, hy_w_in, hy_b_in, hy_w_short, hy_b_short, hy_f_w1, hy_f_b1, hy_f_freq, hy_f_w2, hy_f_b2, hy_f_w3, hy_d_bias, hy_w_out, hy_b_out, at_w_qkv, at_w_out, at_lambda_q1, at_lambda_k1, at_lambda_q2, at_lambda_k2, at_subln, ffn_w_up, ffn_w_dw, ffn_b_dw, ffn_w_down):
    cond8 = jnp.concatenate([c_ctx[None, :], c, jnp.zeros((SUB - 1 - BS, D), F32)], axis=0)
    mods = _ada(cond8, w_ada, b_ada)

    fwd, inv, sgn = _dft_mats()
    tabs = _filter_tables()
    min_decay = math.log(DECAY_TARGET) / DECAY_PCT_LONG
    max_decay = math.log(DECAY_TARGET) / DECAY_PCT_SHORT
    absd = jnp.asarray(np.abs(np.linspace(min_decay, max_decay, D))[None, :], F32)
    ropes = _rope_tables(LS)

    row = lambda a: a.reshape(a.shape[0], 1, a.shape[1])
    nw_all = norm_w.reshape(DEPTH * 4, 1, D)
    w_in_bf, w_hy_out_bf = hy_w_in.astype(BF16), hy_w_out.astype(BF16)
    w_qkv_bf, w_at_out_bf = at_w_qkv.astype(BF16), at_w_out.astype(BF16)
    w_up_bf, w_down_bf = ffn_w_up.astype(BF16), ffn_w_down.astype(BF16)
    w1_pad = jnp.pad(hy_f_w1, ((0, 0), (0, FILTER_HIDDEN - EMB_DIM), (0, 0)))
    lamv = jnp.stack([at_lambda_q1, at_lambda_k1, at_lambda_q2, at_lambda_k2], axis=1)
    subln = row(at_subln)

    x_parts = [x_prompt.reshape(MP, D), x_sample.reshape(MS, D)]
    caches = None
    for i in range(DEPTH):
        j = i // 2
        if i % 2 == 0:
            x0, z = _hy_in(x_parts, nw_all, mods, i, j, w_in_bf, row(hy_b_in), hy_w_short,
                           row(hy_b_short))
            spectra = _filter_spectra(tabs, fwd, sgn, j, w1_pad, row(hy_f_b1), row(hy_f_freq),
                                      hy_f_w2, row(hy_f_b2), hy_f_w3, absd)
            a_parts = [_long_conv(z, x0, spectra, fwd, inv, row(hy_d_bias), j)]
            w_out, b_out = w_hy_out_bf, row(hy_b_out)
        else:
            lam_init = 0.8 - 0.6 * math.exp(-0.3 * i)
            x = x_parts[0]
            qkv_c, new_k, new_v = _qkv_ctx(x, nw_all, mods, i, j, w_qkv_bf, caches)
            caches = (new_k, new_v)
            qkv_l = _qkv_lat(x, nw_all, mods, i, j, w_qkv_bf, ropes)
            a_parts = [_attn_ctx(qkv_c, lamv, subln, j, lam_init),
                       _attn_lat(qkv_l, cache_k, cache_v, lamv, subln, j, lam_init)]
            w_out, b_out = w_at_out_bf, None
        y = _tail(a_parts, x_parts, nw_all, mods, i, j, w_out, b_out, w_up_bf, ffn_w_dw,
                  row(ffn_b_dw), w_down_bf, split_out=(i == DEPTH - 1))
        x_parts = list(y) if i == DEPTH - 1 else [y]

    return (x_parts[0].reshape(BP, LP, D), x_parts[1].reshape(BS, LS, D), caches[0], caches[1])
```

```python
import functools
import math

import numpy as np
import jax
import jax.numpy as jnp
from jax import lax
from jax.experimental import pallas as pl
from jax.experimental.pallas import tpu as pltpu

D = 1024
BP, LP = 16, 256
BS, LS = 2, 2048
MP = BP * LP
MS = BS * LS
M = MP + MS
DEPTH = 4
N_ATTN = DEPTH // 2
GRID_W = 64
N_HEADS = 8
HEAD_DIM = 64
V_DIM = 128
ROPE_THETA = 10000.0
ROT_FREQS = 16
EMB_BANDS = 16
EMB_DIM = 33
FILTER_HIDDEN = 64
DECAY_TARGET = 1e-2
DECAY_PCT_SHORT = 0.3
DECAY_PCT_LONG = 1.5
D_FF = 2816
EPS = 1e-6
PAST = 256

CB = 256
NB_S = LS // CB
N_CHUNK = 2 + 2 * NB_S
SUB = 8
HB = 16
HALO = 2 * HB
SLAB = 16
CK = 256
TM = 512
KEY_CHUNK = 512
ATT_HEADS = 4
VMEM_LIMIT = 56 * 1024 * 1024
QSCALE = HEAD_DIM ** -0.5 * math.log2(math.e)

F32 = jnp.float32
BF16 = jnp.bfloat16


def _dot(a, b):
    return jnp.dot(a, b, preferred_element_type=F32)


def _rms(x, w):
    ms = jnp.mean(x * x, axis=-1, keepdims=True)
    return x * lax.rsqrt(ms + EPS) * w


def _silu(x):
    return x / (1.0 + jnp.exp(-x))


def _cp(sem, vmem=VMEM_LIMIT):
    return pltpu.CompilerParams(dimension_semantics=sem, vmem_limit_bytes=vmem)


def _lay(shape, idx):
    return pl.BlockSpec((None,) + tuple(shape), lambda i: (idx,) + (0,) * len(shape),
                        pipeline_mode=pl.Buffered(1))


def _norm_spec(layer, k):
    return _lay((1, D), layer * 4 + k)


N_CTX_TILES = MP // TM
N_LAT_TILES = MS // TM


def _ctx_tile(i):
    return jnp.minimum(i, N_CTX_TILES - 1)


def _lat_tile(i):
    return jnp.maximum(i - N_CTX_TILES, 0)


def _joint_tile(i):
    return i


def _is_ctx():
    return pl.program_id(0) < N_CTX_TILES


def _tile_spec(width, tile_of):
    return pl.BlockSpec((TM, width), lambda i: (tile_of(i), 0))


def _ext_specs(n_rows, tile_of, width=D):
    r = TM // HB
    last_blk = n_rows // HB - 1
    return [
        pl.BlockSpec((TM, width), lambda i: (tile_of(i), 0)),
        pl.BlockSpec((HB, width), lambda i: (jnp.maximum(tile_of(i) * r - 1, 0), 0)),
        pl.BlockSpec((HB, width), lambda i: (jnp.minimum((tile_of(i) + 1) * r, last_blk), 0)),
    ]


def _fill_ext(scr, t_ref, p_ref, n_ref):
    scr[0:TM, :] = t_ref[...]
    scr[TM:TM + HB, :] = p_ref[...]
    scr[TM + HB:TM + HALO, :] = n_ref[...]


def _per_part(n_parts, fn, refs):
    if n_parts == 1:
        fn(*refs)
        return
    k = len(refs) // 2
    pl.when(_is_ctx())(lambda: fn(*refs[:k]))
    pl.when(jnp.logical_not(_is_ctx()))(lambda: fn(*refs[k:]))


def _ada_kernel(c_ref, w_ref, b_ref, o_ref):
    s = _silu(c_ref[...]).astype(BF16)
    o_ref[...] = _dot(s, w_ref[...].astype(BF16)) + b_ref[...]


def _ada(cond8, w_ada, b_ada):
    out = pl.pallas_call(
        _ada_kernel,
        grid=(DEPTH, 6),
        in_specs=[
            pl.BlockSpec((SUB, D), lambda l, k: (0, 0)),
            pl.BlockSpec((None, D, D), lambda l, k: (l, 0, k)),
            pl.BlockSpec((None, 1, D), lambda l, k: (l, 0, k)),
        ],
        out_specs=pl.BlockSpec((None, SUB, D), lambda l, k: (l * 6 + k, 0, 0)),
        out_shape=jax.ShapeDtypeStruct((DEPTH * 6, SUB, D), F32),
        compiler_params=_cp(("arbitrary", "arbitrary")),
        name="ada",
    )(cond8, w_ada, b_ada.reshape(DEPTH, 1, 6 * D))
    return out.reshape(DEPTH * 6 * SUB, 1, D)


def _mod_spec(layer, which, tile_of=_joint_tile):
    base = (layer * 6 + which) * SUB
    per_b = LS // TM

    def imap(i):
        t = tile_of(i)
        r = jnp.where(t < N_CTX_TILES, 0, 1 + (t - N_CTX_TILES) // per_b)
        return (base + r, 0, 0)

    return pl.BlockSpec((None, 1, D), imap)


def _hmod(x, nw_ref, sh_ref, sc_ref):
    return (_rms(x, nw_ref[...]) * (1.0 + sc_ref[...]) + sh_ref[...]).astype(BF16)


def _fill_h(h_scr, nw_ref, sh_ref, sc_ref, x_ref, xp_ref, xn_ref):
    h_scr[0:TM, :] = _hmod(x_ref[...], nw_ref, sh_ref, sc_ref)
    h_scr[TM:TM + HB, :] = _hmod(xp_ref[...], nw_ref, sh_ref, sc_ref)
    h_scr[TM + HB:TM + HALO, :] = _hmod(xn_ref[...], nw_ref, sh_ref, sc_ref)


def _tile_flags():
    i = pl.program_id(0)
    is_ctx = _is_ctx()
    lseq = jnp.where(is_ctx, LP, LS)
    starts = ((i * TM) & (lseq - 1)) == 0
    ends = (((i + 1) * TM) & (lseq - 1)) == 0
    return is_ctx, starts, ends


def _edge_slabs():
    return sorted({b for b in range(0, TM, LP)} | {b + LP - SLAB for b in range(0, TM, LP)})


def _conv3_bulk(u, w, b):
    return (pltpu.roll(u, 1, axis=0) * w[0:1, :] + u * w[1:2, :]
            + pltpu.roll(u, TM - 1, axis=0) * w[2:3, :] + b)


def _conv3_slab(u_ext, s, w, b, flags):
    is_ctx, starts, ends = flags
    us = u_ext[s:s + SLAB, :]
    if s == 0:
        prev = jnp.where(starts, 0.0, u_ext[TM + HB - 1:TM + HB, :])
    else:
        prev = u_ext[s - 1:s, :]
        if s % LP == 0:
            prev = jnp.where(is_ctx, 0.0, prev)
    if s + SLAB == TM:
        nxt = jnp.where(ends, 0.0, u_ext[TM + HB:TM + HB + 1, :])
    else:
        nxt = u_ext[s + SLAB:s + SLAB + 1, :]
        if (s + SLAB) % LP == 0:
            nxt = jnp.where(is_ctx, 0.0, nxt)
    rows = lax.broadcasted_iota(jnp.int32, us.shape, 0)
    up = jnp.where(rows == 0, prev, pltpu.roll(us, 1, axis=0))
    dn = jnp.where(rows == SLAB - 1, nxt, pltpu.roll(us, SLAB - 1, axis=0))
    return up * w[0:1, :] + us * w[1:2, :] + dn * w[2:3, :] + b


def _tail_kernel(n_a, n_x, has_bias, split_out, *refs):
    a_refs, x_refs = refs[:3 * n_a], refs[3 * n_a:3 * (n_a + n_x)]
    rest = list(refs[3 * (n_a + n_x):])
    w_ref = rest.pop(0)
    b_ref = rest.pop(0) if has_bias else None
    (nw1_ref, g1_ref, nw2_ref, sh_ref, sc_ref, wup_ref, wdw_ref, bdw_ref, wdn_ref, nw3_ref,
     g2_ref) = rest[:11]
    n_out = 2 if split_out else 1
    out_refs = rest[11:11 + n_out]
    a_scr, x_scr, h_scr, act_scr = rest[11 + n_out:]

    _per_part(n_a, functools.partial(_fill_ext, a_scr), a_refs)
    _per_part(n_x, functools.partial(_fill_ext, x_scr), x_refs)
    y = _dot(a_scr[...], w_ref[...])
    if has_bias:
        y = y + b_ref[...]
    x1 = x_scr[...] + g1_ref[...] * _rms(y, nw1_ref[...])
    x_scr[...] = x1
    h_scr[...] = _hmod(x1, nw2_ref, sh_ref, sc_ref)

    flags = _tile_flags()
    h = h_scr[...]
    for c in range(D_FF // CK):
        cg = slice(c * CK, (c + 1) * CK)
        cv = slice(D_FF + c * CK, D_FF + (c + 1) * CK)
        g_ext = _dot(h, wup_ref[:, cg])
        v_ext = _dot(h, wup_ref[:, cv])
        wg, bg, wv, bv = wdw_ref[:, cg], bdw_ref[:, cg], wdw_ref[:, cv], bdw_ref[:, cv]
        g = _conv3_bulk(g_ext[0:TM, :], wg, bg)
        val = _conv3_bulk(v_ext[0:TM, :], wv, bv)
        act_scr[:, cg] = (_silu(g) * val).astype(BF16)
        for s in _edge_slabs():
            g = _conv3_slab(g_ext, s, wg, bg, flags)
            val = _conv3_slab(v_ext, s, wv, bv, flags)
            act_scr[s:s + SLAB, cg] = (_silu(g) * val).astype(BF16)
    y = _dot(act_scr[...], wdn_ref[...])
    res = x_scr[0:TM, :] + g2_ref[...] * _rms(y, nw3_ref[...])
    if split_out:
        is_ctx = flags[0]

        @pl.when(is_ctx)
        def _():
            out_refs[0][...] = res

        @pl.when(jnp.logical_not(is_ctx))
        def _():
            out_refs[1][...] = res
    else:
        out_refs[0][...] = res


def _tail(a_parts, x_parts, nw_all, mods, layer, j, w_out_bf, b_out, wup_bf, w_dw, b_dw, wdn_bf,
          split_out):
    k = w_out_bf.shape[1]
    has_bias = b_out is not None

    def specs(parts, width):
        if len(parts) == 1:
            return _ext_specs(M, _joint_tile, width)
        return _ext_specs(MP, _ctx_tile, width) + _ext_specs(MS, _lat_tile, width)

    in_specs = specs(a_parts, k) + specs(x_parts, D) + [_lay((k, D), j)]
    args = [p for p in a_parts for _ in range(3)] + [p for p in x_parts for _ in range(3)]
    args.append(w_out_bf)
    if has_bias:
        in_specs.append(_lay((1, D), j))
        args.append(b_out)
    in_specs += [
        _norm_spec(layer, 1), _mod_spec(layer, 2),
        _norm_spec(layer, 2), _mod_spec(layer, 3), _mod_spec(layer, 4),
        _lay((D, 2 * D_FF), layer), _lay((3, 2 * D_FF), layer), _lay((1, 2 * D_FF), layer),
        _lay((D_FF, D), layer),
        _norm_spec(layer, 3), _mod_spec(layer, 5),
    ]
    args += [nw_all, mods, nw_all, mods, mods, wup_bf, w_dw, b_dw, wdn_bf, nw_all, mods]
    if split_out:
        out_specs = [_tile_spec(D, _ctx_tile), _tile_spec(D, _lat_tile)]
        out_shape = [jax.ShapeDtypeStruct((MP, D), F32), jax.ShapeDtypeStruct((MS, D), F32)]
    else:
        out_specs = _tile_spec(D, _joint_tile)
        out_shape = jax.ShapeDtypeStruct((M, D), F32)
    return pl.pallas_call(
        functools.partial(_tail_kernel, len(a_parts), len(x_parts), has_bias, split_out),
        grid=(M // TM,),
        in_specs=in_specs,
        out_specs=out_specs,
        out_shape=out_shape,
        scratch_shapes=[pltpu.VMEM((TM + HALO, k), BF16), pltpu.VMEM((TM + HALO, D), F32),
                        pltpu.VMEM((TM + HALO, D), BF16), pltpu.VMEM((TM, D_FF), BF16)],
        compiler_params=_cp(("arbitrary",)),
        name="tail",
    )(*args)


def _hy_in_kernel(n_parts, *refs):
    x_refs = refs[:3 * n_parts]
    (nw_ref, sh_ref, sc_ref, w_ref, b_ref, ws_ref, bs_ref, x0_ref, z_ref,
     h_scr) = refs[3 * n_parts:]
    _per_part(n_parts, functools.partial(_fill_h, h_scr, nw_ref, sh_ref, sc_ref), x_refs)
    flags = _tile_flags()
    h = h_scr[...]
    for c in range(D // CK):
        cc = slice(c * CK, (c + 1) * CK)
        u_ext, ws, bs = [], [], []
        for s in range(3):
            cs = slice(s * D + c * CK, s * D + (c + 1) * CK)
            u_ext.append(_dot(h, w_ref[:, cs]) + b_ref[:, cs])
            ws.append(ws_ref[:, cs])
            bs.append(bs_ref[:, cs])
        out = [_conv3_bulk(u_ext[s][0:TM, :], ws[s], bs[s]) for s in range(3)]
        x0_ref[:, cc] = out[0].astype(BF16)
        z_ref[:, cc] = (out[2] * out[1]).astype(BF16)
        for r in _edge_slabs():
            out = [_conv3_slab(u_ext[s], r, ws[s], bs[s], flags) for s in range(3)]
            x0_ref[r:r + SLAB, cc] = out[0].astype(BF16)
            z_ref[r:r + SLAB, cc] = (out[2] * out[1]).astype(BF16)


def _hy_in(x_parts, nw_all, mods, layer, j, w_bf, b_in, w_short, b_short):
    if len(x_parts) == 1:
        x_specs = _ext_specs(M, _joint_tile)
    else:
        x_specs = _ext_specs(MP, _ctx_tile) + _ext_specs(MS, _lat_tile)
    x_args = [a for a in x_parts for _ in range(3)]
    return pl.pallas_call(
        functools.partial(_hy_in_kernel, len(x_parts)),
        grid=(M // TM,),
        in_specs=x_specs + [
            _norm_spec(layer, 0), _mod_spec(layer, 0), _mod_spec(layer, 1),
            _lay((D, 3 * D), j), _lay((1, 3 * D), j), _lay((3, 3 * D), j), _lay((1, 3 * D), j),
        ],
        out_specs=[_tile_spec(D, _joint_tile)] * 2,
        out_shape=[jax.ShapeDtypeStruct((M, D), BF16)] * 2,
        scratch_shapes=[pltpu.VMEM((TM + HALO, D), BF16)],
        compiler_params=_cp(("arbitrary",)),
        name="hy_in",
    )(*x_args, nw_all, mods, mods, w_bf, b_in, w_short, b_short)


def _dft_mats():
    n = np.arange(CB, dtype=np.float64)
    f = np.arange(CB, dtype=np.float64)
    ang = 2.0 * np.pi * np.outer(f, n) / (2 * CB)
    fwd = np.concatenate([np.cos(ang), -np.sin(ang)], axis=0)
    fwd[CB] = np.cos(np.pi * n)
    scale = np.full((2 * CB, 1), 2.0 / (2 * CB))
    scale[0] = scale[CB] = 1.0 / (2 * CB)
    inv = (fwd * scale).T
    sgn = np.where(np.arange(CB) % 2 == 0, 1.0, -1.0)
    sgn2 = np.concatenate([sgn, sgn])[:, None]
    sgn2[CB] = 1.0
    return (jnp.asarray(fwd, F32).astype(BF16), jnp.asarray(inv, F32).astype(BF16),
            jnp.asarray(sgn2, F32))


def _filter_features(L):
    t = np.linspace(0.0, 1.0, L)[:, None]
    w = 2.0 * np.pi * np.arange(L)[:, None] / L
    bands = np.linspace(1e-4, EMB_BANDS - 1, EMB_BANDS)
    z = np.concatenate([t, np.cos(bands * w), -np.sin(bands * w)], axis=-1)
    return t, z


def _filter_tables():
    zs, ts, ms = [], [], []
    for L in (LP, LS):
        t, z = _filter_features(L)
        z = np.pad(z, ((0, 0), (0, FILTER_HIDDEN - EMB_DIM)))
        idx = np.abs(np.arange(2 * L) - L) % L
        zs.append(z[idx])
        ts.append(t[idx])
        ms.append((np.arange(2 * L) != 0).astype(np.float64)[:, None])
    return tuple(jnp.asarray(np.concatenate(a), F32) for a in (zs, ts, ms))


N_WIN = 1 + 2 * NB_S - 1


def _filt_kernel(z_ref, t_ref, m_ref, w1_ref, b1_ref, fr_ref, w2_ref, b2_ref, w3_ref,
                 ad_ref, f_ref, sg_ref, g_ref, nyq_ref, prev_scr):
    c = pl.program_id(0)

    @pl.when(c == 0)
    def _():
        prev_scr[...] = jnp.zeros_like(prev_scr)

    fr = fr_ref[...]
    hid = jnp.sin(fr * (_dot(z_ref[...].astype(BF16), w1_ref[...].astype(BF16)) + b1_ref[...]))
    hid = jnp.sin(fr * (_dot(hid.astype(BF16), w2_ref[...].astype(BF16)) + b2_ref[...]))
    h = _dot(hid.astype(BF16), w3_ref[...].astype(BF16))
    taps = h * jnp.exp(-t_ref[...] * ad_ref[...]) * m_ref[...]
    a = _dot(f_ref[...], taps.astype(BF16))
    g = a + sg_ref[...] * prev_scr[...]
    prev_scr[...] = a
    g_ref[...] = g
    nyq_ref[...] = g[CB:CB + 1, :]
    g_ref[CB:CB + 1, :] = jnp.zeros((1, D), F32)


def _is_bwd_chunk(c):
    return jnp.logical_or(c == 0, jnp.logical_and(c >= 2, c < 2 + NB_S))


def _window_of_chunk(c):
    return jnp.where(c <= 1, 0, jnp.maximum(c - 2, 1))


def _filter_spectra(tabs, fwd, sgn, j, w1, b1, freq, w2, b2, w3, absd):
    z_tab, t_tab, m_tab = tabs
    small = lambda shape: pl.BlockSpec(shape, lambda c: (0, 0))
    hid = _lay((1, FILTER_HIDDEN), j)
    return pl.pallas_call(
        _filt_kernel,
        grid=(N_CHUNK,),
        in_specs=[
            pl.BlockSpec((CB, FILTER_HIDDEN), lambda c: (c, 0)),
            pl.BlockSpec((CB, 1), lambda c: (c, 0)),
            pl.BlockSpec((CB, 1), lambda c: (c, 0)),
            _lay((FILTER_HIDDEN, FILTER_HIDDEN), j), hid, hid,
            _lay((FILTER_HIDDEN, FILTER_HIDDEN), j), hid,
            pl.BlockSpec((None, FILTER_HIDDEN, D),
                         lambda c: (j, 0, jnp.where(_is_bwd_chunk(c), 1, 0))),
            small((1, D)),
            small((2 * CB, CB)),
            small((2 * CB, 1)),
        ],
        out_specs=[pl.BlockSpec((None, 2 * CB, D), lambda c: (_window_of_chunk(c), 0, 0)),
                   pl.BlockSpec((None, 1, D), lambda c: (_window_of_chunk(c), 0, 0))],
        out_shape=[jax.ShapeDtypeStruct((N_WIN, 2 * CB, D), F32),
                   jax.ShapeDtypeStruct((N_WIN, 1, D), F32)],
        scratch_shapes=[pltpu.VMEM((2 * CB, D), F32)],
        compiler_params=_cp(("arbitrary",)),
        name="hy_filter",
    )(z_tab, t_tab, m_tab, w1, b1, freq, w2, b2, w3, absd, fwd, sgn)


def _conv_kernel(tc, z_ref, x0_ref, g_ref, nyq_ref, f_ref, gi_ref, db_ref, o_ref, u_scr):
    unit = pl.program_id(1)
    fmat = f_ref[...]
    gmat = gi_ref[...]
    row0 = lax.broadcasted_iota(jnp.int32, (CB, tc), 0) == 0
    db = db_ref[...]

    def emit(blk, yre, yim, ynyq):
        rows = pl.ds(blk * CB, CB)
        yspec = jnp.concatenate([yre, jnp.where(row0, ynyq, yim)], axis=0)
        y = _dot(gmat, yspec.astype(BF16))
        o_ref[rows, :] = (x0_ref[rows, :] * (y + z_ref[rows, :] * db)).astype(BF16)

    @pl.when(unit < MP // LS)
    def _():
        gre, gim, gnyq = g_ref[0, :CB, :], g_ref[0, CB:, :], nyq_ref[0]
        for s in range(LS // LP):
            u = _dot(fmat, z_ref[pl.ds(s * CB, CB), :])
            ure, unyq = u[:CB], u[CB:CB + 1]
            uim = jnp.where(row0, 0.0, u[CB:])
            emit(s, gre * ure - gim * uim, gre * uim + gim * ure, gnyq * unyq)

    @pl.when(unit >= MP // LS)
    def _():
        unyq = []
        for j in range(NB_S):
            u = _dot(fmat, z_ref[pl.ds(j * CB, CB), :])
            unyq.append(u[CB:CB + 1])
            u_scr[j] = u
            u_scr[j, CB:CB + 1, :] = jnp.zeros((1, tc), F32)
        for i in range(NB_S):
            yre = jnp.zeros((CB, tc), F32)
            yim = jnp.zeros((CB, tc), F32)
            ynyq = jnp.zeros((1, tc), F32)
            for j in range(NB_S):
                w = NB_S + i - j
                gre, gim = g_ref[w, :CB, :], g_ref[w, CB:, :]
                ure, uim = u_scr[j, :CB, :], u_scr[j, CB:, :]
                yre = yre + gre * ure - gim * uim
                yim = yim + gre * uim + gim * ure
                ynyq = ynyq + nyq_ref[w] * unyq[j]
            emit(i, yre, yim, ynyq)


def _long_conv(z, x0, spectra, fwd, inv, d_bias, j, tc=256):
    g_win, g_nyq = spectra
    return pl.pallas_call(
        functools.partial(_conv_kernel, tc),
        grid=(D // tc, M // LS),
        in_specs=[
            pl.BlockSpec((LS, tc), lambda c, u: (u, c)),
            pl.BlockSpec((LS, tc), lambda c, u: (u, c)),
            pl.BlockSpec((N_WIN, 2 * CB, tc), lambda c, u: (0, 0, c)),
            pl.BlockSpec((N_WIN, 1, tc), lambda c, u: (0, 0, c)),
            pl.BlockSpec((2 * CB, CB), lambda c, u: (0, 0)),
            pl.BlockSpec((CB, 2 * CB), lambda c, u: (0, 0)),
            pl.BlockSpec((None, 1, tc), lambda c, u: (j, 0, c)),
        ],
        out_specs=pl.BlockSpec((LS, tc), lambda c, u: (u, c)),
        out_shape=jax.ShapeDtypeStruct((M, D), BF16),
        scratch_shapes=[pltpu.VMEM((NB_S, 2 * CB, tc), F32)],
        compiler_params=_cp(("arbitrary", "arbitrary")),
        name="hy_conv",
    )(z, x0, g_win, g_nyq, fwd, inv, d_bias)


def _rope_tables(L):
    rows = L // GRID_W
    r = np.repeat(np.arange(rows, dtype=np.float64), GRID_W)
    cidx = np.tile(np.arange(GRID_W, dtype=np.float64), rows)
    inv = ROPE_THETA ** (-np.arange(ROT_FREQS, dtype=np.float64) / ROT_FREQS)
    ar = r[:, None] * inv
    ac = cidx[:, None] * inv
    cos = np.concatenate([np.cos(ar), np.cos(ar), np.cos(ac), np.cos(ac)] * 2, axis=-1)
    sin = np.concatenate([np.sin(ar), np.sin(ar), np.sin(ac), np.sin(ac)] * 2, axis=-1)
    first_half = (np.arange(V_DIM) % (2 * ROT_FREQS)) < ROT_FREQS
    sin_a = np.where(first_half, -sin, 0.0)
    sin_b = np.where(first_half, 0.0, sin)
    return tuple(jnp.asarray(a, F32) for a in (cos, sin_a, sin_b))


def _rope(x, cos, sin_a, sin_b):
    return (x * cos + pltpu.roll(x, V_DIM - ROT_FREQS, axis=1) * sin_a
            + pltpu.roll(x, ROT_FREQS, axis=1) * sin_b)


HEADS_PER_CHUNK = CK // V_DIM


def _qkv_chunks():
    per_part = D // CK
    return [(slice(c * CK, (c + 1) * CK), c // per_part, (c % per_part) * HEADS_PER_CHUNK)
            for c in range(3 * per_part)]


def _qkv_c_kernel(last, x_ref, nw_ref, sh_ref, sc_ref, w_ref, *rest):
    if last:
        pk_ref, pv_ref, qkv_ref, nk_ref, nv_ref = rest
        nk_ref[:, 0] = pk_ref[...]
        nv_ref[:, 0] = pv_ref[...]
    else:
        qkv_ref, nk_ref, nv_ref = rest
    h = _hmod(x_ref[...], nw_ref, sh_ref, sc_ref)
    for cs, part, head0 in _qkv_chunks():
        u = _dot(h, w_ref[:, cs])
        qkv_ref[:, cs] = (u * QSCALE if part == 0 else u).astype(BF16)
        if part > 0:
            cache = nk_ref if part == 1 else nv_ref
            for s in range(TM // LP):
                for hh in range(HEADS_PER_CHUNK):
                    blk = u[s * LP:(s + 1) * LP, hh * V_DIM:(hh + 1) * V_DIM]
                    if last:
                        cache[s, N_ATTN - 1, head0 + hh] = blk
                    else:
                        cache[s, head0 + hh] = blk


def _qkv_ctx(x, nw_all, mods, layer, j, w_bf, prev):
    seqs = TM // LP
    one = pl.BlockSpec((seqs, N_HEADS, LP, V_DIM), lambda i: (i, 0, 0, 0))
    one_shape = jax.ShapeDtypeStruct((BP, N_HEADS, LP, V_DIM), F32)
    in_specs = [_tile_spec(D, _joint_tile), _norm_spec(layer, 0), _mod_spec(layer, 0),
                _mod_spec(layer, 1), _lay((D, 3 * D), j)]
    args = [x, nw_all, mods, mods, w_bf]
    if prev is None:
        cache, cache_shape = one, one_shape
    else:
        assert j == N_ATTN - 1 == 1
        in_specs += [one, one]
        args += list(prev)
        cache = pl.BlockSpec((seqs, N_ATTN, N_HEADS, LP, V_DIM), lambda i: (i, 0, 0, 0, 0))
        cache_shape = jax.ShapeDtypeStruct((BP, N_ATTN, N_HEADS, LP, V_DIM), F32)
    return pl.pallas_call(
        functools.partial(_qkv_c_kernel, prev is not None),
        grid=(N_CTX_TILES,),
        in_specs=in_specs,
        out_specs=[_tile_spec(3 * D, _joint_tile), cache, cache],
        out_shape=[jax.ShapeDtypeStruct((MP, 3 * D), BF16), cache_shape, cache_shape],
        compiler_params=_cp(("arbitrary",)),
        name="qkv_ctx",
    )(*args)


def _qkv_l_kernel(x_ref, nw_ref, sh_ref, sc_ref, w_ref, cos_ref, sa_ref, sb_ref, qkv_ref):
    h = _hmod(x_ref[...], nw_ref, sh_ref, sc_ref)
    for cs, part, _ in _qkv_chunks():
        u = _dot(h, w_ref[:, cs])
        if part == 2:
            qkv_ref[:, cs] = u.astype(BF16)
            continue
        for hh in range(HEADS_PER_CHUNK):
            r = _rope(u[:, hh * V_DIM:(hh + 1) * V_DIM], cos_ref[...], sa_ref[...], sb_ref[...])
            if part == 0:
                r = r * QSCALE
            qkv_ref[:, cs.start + hh * V_DIM:cs.start + (hh + 1) * V_DIM] = r.astype(BF16)


def _qkv_lat(x, nw_all, mods, layer, j, w_bf, ropes):
    tile = lambda i: i + N_CTX_TILES
    tab = pl.BlockSpec((TM, V_DIM), lambda i: (i % (LS // TM), 0))
    return pl.pallas_call(
        _qkv_l_kernel,
        grid=(N_LAT_TILES,),
        in_specs=[_tile_spec(D, tile), _norm_spec(layer, 0), _mod_spec(layer, 0, tile),
                  _mod_spec(layer, 1, tile), _lay((D, 3 * D), j), tab, tab, tab],
        out_specs=_tile_spec(3 * D, _joint_tile),
        out_shape=jax.ShapeDtypeStruct((MS, 3 * D), BF16),
        compiler_params=_cp(("arbitrary",)),
        name="qkv_lat",
    )(x, nw_all, mods, mods, w_bf, *ropes)


def _lambda(lv, lam_init):
    a = jnp.exp(jnp.sum(lv[0:1, :] * lv[1:2, :], axis=-1, keepdims=True))
    b = jnp.exp(jnp.sum(lv[2:3, :] * lv[3:4, :], axis=-1, keepdims=True))
    return a - b + lam_init


def _diff_attn(q, chunks, lam, lam_init, subln):
    t = q.shape[0]
    lane = lax.broadcasted_iota(jnp.int32, q.shape, 1)
    zero = jnp.zeros_like(q)
    q2 = jnp.concatenate([jnp.where(lane < HEAD_DIM, q, zero),
                          jnp.where(lane < HEAD_DIM, zero, q)], axis=0)
    m = l = acc = None
    for k, v in chunks:
        s = lax.dot_general(q2, k, (((1,), (1,)), ((), ())), preferred_element_type=F32)
        mc = jnp.max(s, axis=-1, keepdims=True)
        m_new = mc if m is None else jnp.maximum(m, mc)
        e = jnp.exp2(s - m_new)
        lc = jnp.sum(e, axis=-1, keepdims=True)
        pv = _dot(e.astype(BF16), v)
        if m is None:
            l, acc = lc, pv
        else:
            alpha = jnp.exp2(m - m_new)
            l = alpha * l + lc
            acc = alpha * acc + pv
        m = m_new
    o = acc[:t] * (1.0 / l[:t]) - acc[t:] * (lam / l[t:])
    return _rms(o, subln) * (1.0 - lam_init)


def _attn_c_kernel(lam_init, q_ref, k_ref, v_ref, lv_ref, sub_ref, o_ref):
    lam = _lambda(lv_ref[...], lam_init)
    t = q_ref.shape[0]
    heads = [slice(h * V_DIM, (h + 1) * V_DIM) for h in range(N_HEADS)]
    lane = lax.broadcasted_iota(jnp.int32, (t, V_DIM), 1)
    zero = jnp.zeros((t, V_DIM), BF16)
    dn = (((1,), (1,)), ((), ()))
    s = []
    for hs in heads:
        q = q_ref[:, hs]
        q2 = jnp.concatenate([jnp.where(lane < HEAD_DIM, q, zero),
                              jnp.where(lane < HEAD_DIM, zero, q)], axis=0)
        s.append(lax.dot_general(q2, k_ref[:, hs], dn, preferred_element_type=F32))
    e = [jnp.exp2(x - jnp.max(x, axis=-1, keepdims=True)) for x in s]
    l = [jnp.sum(x, axis=-1, keepdims=True) for x in e]
    pv = [_dot(x.astype(BF16), v_ref[:, hs]) for x, hs in zip(e, heads)]
    for hs, acc, lh in zip(heads, pv, l):
        o = acc[:t] * (1.0 / lh[:t]) - acc[t:] * (lam / lh[t:])
        o_ref[:, hs] = (_rms(o, sub_ref[...]) * (1.0 - lam_init)).astype(BF16)


def _attn_ctx(qkv_c, lamv, subln, j, lam_init):
    part = lambda p: pl.BlockSpec((LP, D), lambda b: (b, p))
    return pl.pallas_call(
        functools.partial(_attn_c_kernel, lam_init),
        grid=(BP,),
        in_specs=[part(0), part(1), part(2),
                  pl.BlockSpec((None, 4, HEAD_DIM), lambda b: (j, 0, 0)),
                  pl.BlockSpec((None, 1, V_DIM), lambda b: (j, 0, 0))],
        out_specs=pl.BlockSpec((LP, D), lambda b: (b, 0)),
        out_shape=jax.ShapeDtypeStruct((MP, D), BF16),
        compiler_params=_cp(("arbitrary",)),
        name="attn_ctx",
    )(qkv_c, qkv_c, qkv_c, lamv, subln)


def _attn_l_kernel(lam_init, q_ref, k_ref, v_ref, ck_ref, cv_ref, lv_ref, sub_ref, o_ref):
    lam = _lambda(lv_ref[...], lam_init)
    for h in range(ATT_HEADS):
        hs = slice(h * V_DIM, (h + 1) * V_DIM)
        chunks = [(ck_ref[h].astype(BF16), cv_ref[h].astype(BF16))]
        for c in range(LS // KEY_CHUNK):
            rows = pl.ds(c * KEY_CHUNK, KEY_CHUNK)
            chunks.append((k_ref[rows, hs], v_ref[rows, hs]))
        o = _diff_attn(q_ref[:, hs], chunks, lam, lam_init, sub_ref[...])
        o_ref[:, hs] = o.astype(BF16)


def _attn_lat(qkv_l, cache_k, cache_v, lamv, subln, j, lam_init, tq=512):
    nq = LS // tq
    width = ATT_HEADS * V_DIM
    groups = N_HEADS // ATT_HEADS
    seq = lambda part: pl.BlockSpec((LS, width), lambda b, g, q: (b, part * groups + g))
    ctx = pl.BlockSpec((None, None, ATT_HEADS, PAST, V_DIM), lambda b, g, q: (b, j, g, 0, 0))
    return pl.pallas_call(
        functools.partial(_attn_l_kernel, lam_init),
        grid=(BS, groups, nq),
        in_specs=[pl.BlockSpec((tq, width), lambda b, g, q: (b * nq + q, g)),
                  seq(1), seq(2), ctx, ctx,
                  pl.BlockSpec((None, 4, HEAD_DIM), lambda b, g, q: (j, 0, 0)),
                  pl.BlockSpec((None, 1, V_DIM), lambda b, g, q: (j, 0, 0))],
        out_specs=pl.BlockSpec((tq, width), lambda b, g, q: (b * nq + q, g)),
        out_shape=jax.ShapeDtypeStruct((MS, D), BF16),
        compiler_params=_cp(("arbitrary", "arbitrary", "arbitrary")),
        name="attn_lat",
    )(qkv_l, qkv_l, qkv_l, cache_k, cache_v, lamv, subln)


def kernel(x_prompt, x_sample, cache_k, cache_v, c, c_ctx, w_ada, b_ada, norm_w, hy_w_in, hy_b_in, hy_w_short, hy_b_short, hy_f_w1, hy_f_b1, hy_f_freq, hy_f_w2, hy_f_b2, hy_f_w3, hy_d_bias, hy_w_out, hy_b_out, at_w_qkv, at_w_out, at_lambda_q1, at_lambda_k1, at_lambda_q2, at_lambda_k2, at_subln, ffn_w_up, ffn_w_dw, ffn_b_dw, ffn_w_down):
    cond8 = jnp.concatenate([c_ctx[None, :], c, jnp.zeros((SUB - 1 - BS, D), F32)], axis=0)
    mods = _ada(cond8, w_ada, b_ada)

    fwd, inv, sgn = _dft_mats()
    tabs = _filter_tables()
    min_decay = math.log(DECAY_TARGET) / DECAY_PCT_LONG
    max_decay = math.log(DECAY_TARGET) / DECAY_PCT_SHORT
    absd = jnp.asarray(np.abs(np.linspace(min_decay, max_decay, D))[None, :], F32)
    ropes = _rope_tables(LS)

    row = lambda a: a.reshape(a.shape[0], 1, a.shape[1])
    nw_all = norm_w.reshape(DEPTH * 4, 1, D)
    w_in_bf, w_hy_out_bf = hy_w_in.astype(BF16), hy_w_out.astype(BF16)
    w_qkv_bf, w_at_out_bf = at_w_qkv.astype(BF16), at_w_out.astype(BF16)
    w_up_bf, w_down_bf = ffn_w_up.astype(BF16), ffn_w_down.astype(BF16)
    w1_pad = jnp.pad(hy_f_w1, ((0, 0), (0, FILTER_HIDDEN - EMB_DIM), (0, 0)))
    lamv = jnp.stack([at_lambda_q1, at_lambda_k1, at_lambda_q2, at_lambda_k2], axis=1)
    subln = row(at_subln)

    x_parts = [x_prompt.reshape(MP, D), x_sample.reshape(MS, D)]
    caches = None
    for i in range(DEPTH):
        j = i // 2
        if i % 2 == 0:
            x0, z = _hy_in(x_parts, nw_all, mods, i, j, w_in_bf, row(hy_b_in), hy_w_short,
                           row(hy_b_short))
            spectra = _filter_spectra(tabs, fwd, sgn, j, w1_pad, row(hy_f_b1), row(hy_f_freq),
                                      hy_f_w2, row(hy_f_b2), hy_f_w3, absd)
            a_parts = [_long_conv(z, x0, spectra, fwd, inv, row(hy_d_bias), j)]
            w_out, b_out = w_hy_out_bf, row(hy_b_out)
        else:
            lam_init = 0.8 - 0.6 * math.exp(-0.3 * i)
            x = x_parts[0]
            qkv_c, new_k, new_v = _qkv_ctx(x, nw_all, mods, i, j, w_qkv_bf, caches)
            caches = (new_k, new_v)
            qkv_l = _qkv_lat(x, nw_all, mods, i, j, w_qkv_bf, ropes)
            a_parts = [_attn_ctx(qkv_c, lamv, subln, j, lam_init),
                       _attn_lat(qkv_l, cache_k, cache_v, lamv, subln, j, lam_init)]
            w_out, b_out = w_at_out_bf, None
        y = _tail(a_parts, x_parts, nw_all, mods, i, j, w_out, b_out, w_up_bf, ffn_w_dw,
                  row(ffn_b_dw), w_down_bf, split_out=(i == DEPTH - 1))
        x_parts = list(y) if i == DEPTH - 1 else [y]

    return (x_parts[0].reshape(BP, LP, D), x_parts[1].reshape(BS, LS, D), caches[0], caches[1])
```

```python
import functools
import math

import numpy as np
import jax
import jax.numpy as jnp
from jax import lax
from jax.experimental import pallas as pl
from jax.experimental.pallas import tpu as pltpu

D = 1024
BP, LP = 16, 256
BS, LS = 2, 2048
MP = BP * LP
MS = BS * LS
M = MP + MS
DEPTH = 4
N_ATTN = DEPTH // 2
GRID_W = 64
N_HEADS = 8
HEAD_DIM = 64
V_DIM = 128
ROPE_THETA = 10000.0
ROT_FREQS = 16
EMB_BANDS = 16
EMB_DIM = 33
FILTER_HIDDEN = 64
DECAY_TARGET = 1e-2
DECAY_PCT_SHORT = 0.3
DECAY_PCT_LONG = 1.5
D_FF = 2816
EPS = 1e-6
PAST = 256

CB = 256
NB_S = LS // CB
N_CHUNK = 2 + 2 * NB_S
SUB = 8
HB = 16
HALO = 2 * HB
SLAB = 16
CK = 256
TM = 512
KEY_CHUNK = 2048
ATT_HEADS = 4
CTX_SEQS = 2
MAC_ROWS = 32
VMEM_LIMIT = 56 * 1024 * 1024
QSCALE = HEAD_DIM ** -0.5 * math.log2(math.e)

F32 = jnp.float32
BF16 = jnp.bfloat16


def _dot(a, b):
    return jnp.dot(a, b, preferred_element_type=F32)


def _rms(x, w):
    ms = jnp.mean(x * x, axis=-1, keepdims=True)
    return x * lax.rsqrt(ms + EPS) * w


def _silu(x):
    return x / (1.0 + jnp.exp(-x))


def _cp(sem, vmem=VMEM_LIMIT):
    return pltpu.CompilerParams(dimension_semantics=sem, vmem_limit_bytes=vmem)


def _lay(shape, idx):
    return pl.BlockSpec((None,) + tuple(shape), lambda i: (idx,) + (0,) * len(shape),
                        pipeline_mode=pl.Buffered(1))


def _norm_spec(layer, k):
    return _lay((1, D), layer * 4 + k)


N_CTX_TILES = MP // TM
N_LAT_TILES = MS // TM


def _ctx_tile(i):
    return jnp.minimum(i, N_CTX_TILES - 1)


def _lat_tile(i):
    return jnp.maximum(i - N_CTX_TILES, 0)


def _joint_tile(i):
    return i


def _is_ctx():
    return pl.program_id(0) < N_CTX_TILES


def _tile_spec(width, tile_of):
    return pl.BlockSpec((TM, width), lambda i: (tile_of(i), 0))


def _ext_specs(n_rows, tile_of, width=D):
    r = TM // HB
    last_blk = n_rows // HB - 1
    return [
        pl.BlockSpec((TM, width), lambda i: (tile_of(i), 0)),
        pl.BlockSpec((HB, width), lambda i: (jnp.maximum(tile_of(i) * r - 1, 0), 0)),
        pl.BlockSpec((HB, width), lambda i: (jnp.minimum((tile_of(i) + 1) * r, last_blk), 0)),
    ]


def _fill_ext(scr, t_ref, p_ref, n_ref):
    scr[0:TM, :] = t_ref[...]
    scr[TM:TM + HB, :] = p_ref[...]
    scr[TM + HB:TM + HALO, :] = n_ref[...]


def _per_part(n_parts, fn, refs):
    if n_parts == 1:
        fn(*refs)
        return
    k = len(refs) // 2
    pl.when(_is_ctx())(lambda: fn(*refs[:k]))
    pl.when(jnp.logical_not(_is_ctx()))(lambda: fn(*refs[k:]))


def _ada_kernel(c_ref, w_ref, b_ref, o_ref):
    s = _silu(c_ref[...]).astype(BF16)
    o_ref[...] = _dot(s, w_ref[...].astype(BF16)) + b_ref[...]


def _ada(cond8, w_ada, b_ada):
    out = pl.pallas_call(
        _ada_kernel,
        grid=(DEPTH, 6),
        in_specs=[
            pl.BlockSpec((SUB, D), lambda l, k: (0, 0)),
            pl.BlockSpec((None, D, D), lambda l, k: (l, 0, k)),
            pl.BlockSpec((None, 1, D), lambda l, k: (l, 0, k)),
        ],
        out_specs=pl.BlockSpec((None, SUB, D), lambda l, k: (l * 6 + k, 0, 0)),
        out_shape=jax.ShapeDtypeStruct((DEPTH * 6, SUB, D), F32),
        compiler_params=_cp(("arbitrary", "arbitrary")),
        name="ada",
    )(cond8, w_ada, b_ada.reshape(DEPTH, 1, 6 * D))
    return out.reshape(DEPTH * 6 * SUB, 1, D)


def _mod_spec(layer, which, tile_of=_joint_tile):
    base = (layer * 6 + which) * SUB
    per_b = LS // TM

    def imap(i):
        t = tile_of(i)
        r = jnp.where(t < N_CTX_TILES, 0, 1 + (t - N_CTX_TILES) // per_b)
        return (base + r, 0, 0)

    return pl.BlockSpec((None, 1, D), imap)


def _hmod(x, nw_ref, sh_ref, sc_ref):
    return (_rms(x, nw_ref[...]) * (1.0 + sc_ref[...]) + sh_ref[...]).astype(BF16)


def _fill_h(h_scr, nw_ref, sh_ref, sc_ref, x_ref, xp_ref, xn_ref):
    h_scr[0:TM, :] = _hmod(x_ref[...], nw_ref, sh_ref, sc_ref)
    h_scr[TM:TM + HB, :] = _hmod(xp_ref[...], nw_ref, sh_ref, sc_ref)
    h_scr[TM + HB:TM + HALO, :] = _hmod(xn_ref[...], nw_ref, sh_ref, sc_ref)


def _tile_flags():
    i = pl.program_id(0)
    is_ctx = _is_ctx()
    lseq = jnp.where(is_ctx, LP, LS)
    starts = ((i * TM) & (lseq - 1)) == 0
    ends = (((i + 1) * TM) & (lseq - 1)) == 0
    return is_ctx, starts, ends


def _edge_slabs():
    return sorted({b for b in range(0, TM, LP)} | {b + LP - SLAB for b in range(0, TM, LP)})


def _conv3_bulk(u, w, b):
    return (pltpu.roll(u, 1, axis=0) * w[0:1, :] + u * w[1:2, :]
            + pltpu.roll(u, TM - 1, axis=0) * w[2:3, :] + b)


def _conv3_slab(u_ext, s, w, b, flags):
    is_ctx, starts, ends = flags
    us = u_ext[s:s + SLAB, :]
    if s == 0:
        prev = jnp.where(starts, 0.0, u_ext[TM + HB - 1:TM + HB, :])
    else:
        prev = u_ext[s - 1:s, :]
        if s % LP == 0:
            prev = jnp.where(is_ctx, 0.0, prev)
    if s + SLAB == TM:
        nxt = jnp.where(ends, 0.0, u_ext[TM + HB:TM + HB + 1, :])
    else:
        nxt = u_ext[s + SLAB:s + SLAB + 1, :]
        if (s + SLAB) % LP == 0:
            nxt = jnp.where(is_ctx, 0.0, nxt)
    rows = lax.broadcasted_iota(jnp.int32, us.shape, 0)
    up = jnp.where(rows == 0, prev, pltpu.roll(us, 1, axis=0))
    dn = jnp.where(rows == SLAB - 1, nxt, pltpu.roll(us, SLAB - 1, axis=0))
    return up * w[0:1, :] + us * w[1:2, :] + dn * w[2:3, :] + b


def _tail_kernel(n_a, n_x, has_bias, split_out, *refs):
    a_refs, x_refs = refs[:3 * n_a], refs[3 * n_a:3 * (n_a + n_x)]
    rest = list(refs[3 * (n_a + n_x):])
    w_ref = rest.pop(0)
    b_ref = rest.pop(0) if has_bias else None
    (nw1_ref, g1_ref, nw2_ref, sh_ref, sc_ref, wup_ref, wdw_ref, bdw_ref, wdn_ref, nw3_ref,
     g2_ref) = rest[:11]
    n_out = 2 if split_out else 1
    out_refs = rest[11:11 + n_out]
    a_scr, x_scr, h_scr, act_scr = rest[11 + n_out:]

    _per_part(n_a, functools.partial(_fill_ext, a_scr), a_refs)
    _per_part(n_x, functools.partial(_fill_ext, x_scr), x_refs)
    y = _dot(a_scr[...], w_ref[...])
    if has_bias:
        y = y + b_ref[...]
    x1 = x_scr[...] + g1_ref[...] * _rms(y, nw1_ref[...])
    x_scr[...] = x1
    h_scr[...] = _hmod(x1, nw2_ref, sh_ref, sc_ref)

    flags = _tile_flags()
    h = h_scr[...]
    for c in range(D_FF // CK):
        cg = slice(c * CK, (c + 1) * CK)
        cv = slice(D_FF + c * CK, D_FF + (c + 1) * CK)
        g_ext = _dot(h, wup_ref[:, cg])
        v_ext = _dot(h, wup_ref[:, cv])
        wg, bg, wv, bv = wdw_ref[:, cg], bdw_ref[:, cg], wdw_ref[:, cv], bdw_ref[:, cv]
        g = _conv3_bulk(g_ext[0:TM, :], wg, bg)
        val = _conv3_bulk(v_ext[0:TM, :], wv, bv)
        act_scr[:, cg] = (_silu(g) * val).astype(BF16)
        for s in _edge_slabs():
            g = _conv3_slab(g_ext, s, wg, bg, flags)
            val = _conv3_slab(v_ext, s, wv, bv, flags)
            act_scr[s:s + SLAB, cg] = (_silu(g) * val).astype(BF16)
    y = _dot(act_scr[...], wdn_ref[...])
    res = x_scr[0:TM, :] + g2_ref[...] * _rms(y, nw3_ref[...])
    if split_out:
        is_ctx = flags[0]

        @pl.when(is_ctx)
        def _():
            out_refs[0][...] = res

        @pl.when(jnp.logical_not(is_ctx))
        def _():
            out_refs[1][...] = res
    else:
        out_refs[0][...] = res


def _tail(a_parts, x_parts, nw_all, mods, layer, j, w_out_bf, b_out, wup_bf, w_dw, b_dw, wdn_bf,
          split_out):
    k = w_out_bf.shape[1]
    has_bias = b_out is not None

    def specs(parts, width):
        if len(parts) == 1:
            return _ext_specs(M, _joint_tile, width)
        return _ext_specs(MP, _ctx_tile, width) + _ext_specs(MS, _lat_tile, width)

    in_specs = specs(a_parts, k) + specs(x_parts, D) + [_lay((k, D), j)]
    args = [p for p in a_parts for _ in range(3)] + [p for p in x_parts for _ in range(3)]
    args.append(w_out_bf)
    if has_bias:
        in_specs.append(_lay((1, D), j))
        args.append(b_out)
    in_specs += [
        _norm_spec(layer, 1), _mod_spec(layer, 2),
        _norm_spec(layer, 2), _mod_spec(layer, 3), _mod_spec(layer, 4),
        _lay((D, 2 * D_FF), layer), _lay((3, 2 * D_FF), layer), _lay((1, 2 * D_FF), layer),
        _lay((D_FF, D), layer),
        _norm_spec(layer, 3), _mod_spec(layer, 5),
    ]
    args += [nw_all, mods, nw_all, mods, mods, wup_bf, w_dw, b_dw, wdn_bf, nw_all, mods]
    if split_out:
        out_specs = [_tile_spec(D, _ctx_tile), _tile_spec(D, _lat_tile)]
        out_shape = [jax.ShapeDtypeStruct((MP, D), F32), jax.ShapeDtypeStruct((MS, D), F32)]
    else:
        out_specs = _tile_spec(D, _joint_tile)
        out_shape = jax.ShapeDtypeStruct((M, D), F32)
    return pl.pallas_call(
        functools.partial(_tail_kernel, len(a_parts), len(x_parts), has_bias, split_out),
        grid=(M // TM,),
        in_specs=in_specs,
        out_specs=out_specs,
        out_shape=out_shape,
        scratch_shapes=[pltpu.VMEM((TM + HALO, k), BF16), pltpu.VMEM((TM + HALO, D), F32),
                        pltpu.VMEM((TM + HALO, D), BF16), pltpu.VMEM((TM, D_FF), BF16)],
        compiler_params=_cp(("arbitrary",)),
        name="tail",
    )(*args)


def _hy_in_kernel(n_parts, *refs):
    x_refs = refs[:3 * n_parts]
    (nw_ref, sh_ref, sc_ref, w_ref, b_ref, ws_ref, bs_ref, x0_ref, z_ref,
     h_scr) = refs[3 * n_parts:]
    _per_part(n_parts, functools.partial(_fill_h, h_scr, nw_ref, sh_ref, sc_ref), x_refs)
    flags = _tile_flags()
    h = h_scr[...]
    for c in range(D // CK):
        cc = slice(c * CK, (c + 1) * CK)
        u_ext, ws, bs = [], [], []
        for s in range(3):
            cs = slice(s * D + c * CK, s * D + (c + 1) * CK)
            u_ext.append(_dot(h, w_ref[:, cs]) + b_ref[:, cs])
            ws.append(ws_ref[:, cs])
            bs.append(bs_ref[:, cs])
        out = [_conv3_bulk(u_ext[s][0:TM, :], ws[s], bs[s]) for s in range(3)]
        x0_ref[:, cc] = out[0].astype(BF16)
        z_ref[:, cc] = (out[2] * out[1]).astype(BF16)
        for r in _edge_slabs():
            out = [_conv3_slab(u_ext[s], r, ws[s], bs[s], flags) for s in range(3)]
            x0_ref[r:r + SLAB, cc] = out[0].astype(BF16)
            z_ref[r:r + SLAB, cc] = (out[2] * out[1]).astype(BF16)


def _hy_in(x_parts, nw_all, mods, layer, j, w_bf, b_in, w_short, b_short):
    if len(x_parts) == 1:
        x_specs = _ext_specs(M, _joint_tile)
    else:
        x_specs = _ext_specs(MP, _ctx_tile) + _ext_specs(MS, _lat_tile)
    x_args = [a for a in x_parts for _ in range(3)]
    return pl.pallas_call(
        functools.partial(_hy_in_kernel, len(x_parts)),
        grid=(M // TM,),
        in_specs=x_specs + [
            _norm_spec(layer, 0), _mod_spec(layer, 0), _mod_spec(layer, 1),
            _lay((D, 3 * D), j), _lay((1, 3 * D), j), _lay((3, 3 * D), j), _lay((1, 3 * D), j),
        ],
        out_specs=[_tile_spec(D, _joint_tile)] * 2,
        out_shape=[jax.ShapeDtypeStruct((M, D), BF16)] * 2,
        scratch_shapes=[pltpu.VMEM((TM + HALO, D), BF16)],
        compiler_params=_cp(("arbitrary",)),
        name="hy_in",
    )(*x_args, nw_all, mods, mods, w_bf, b_in, w_short, b_short)


def _dft_mats():
    n = np.arange(CB, dtype=np.float64)
    f = np.arange(CB, dtype=np.float64)
    ang = 2.0 * np.pi * np.outer(f, n) / (2 * CB)
    fwd = np.concatenate([np.cos(ang), -np.sin(ang)], axis=0)
    fwd[CB] = np.cos(np.pi * n)
    scale = np.full((2 * CB, 1), 2.0 / (2 * CB))
    scale[0] = scale[CB] = 1.0 / (2 * CB)
    inv = (fwd * scale).T
    sgn = np.where(np.arange(CB) % 2 == 0, 1.0, -1.0)
    sgn2 = np.concatenate([sgn, sgn])[:, None]
    sgn2[CB] = 1.0
    return (jnp.asarray(fwd, F32).astype(BF16), jnp.asarray(inv, F32).astype(BF16),
            jnp.asarray(sgn2, F32))


def _filter_features(L):
    t = np.linspace(0.0, 1.0, L)[:, None]
    w = 2.0 * np.pi * np.arange(L)[:, None] / L
    bands = np.linspace(1e-4, EMB_BANDS - 1, EMB_BANDS)
    z = np.concatenate([t, np.cos(bands * w), -np.sin(bands * w)], axis=-1)
    return t, z


def _filter_tables():
    zs, ts, ms = [], [], []
    for L in (LP, LS):
        t, z = _filter_features(L)
        z = np.pad(z, ((0, 0), (0, FILTER_HIDDEN - EMB_DIM)))
        idx = np.abs(np.arange(2 * L) - L) % L
        zs.append(z[idx])
        ts.append(t[idx])
        ms.append((np.arange(2 * L) != 0).astype(np.float64)[:, None])
    return tuple(jnp.asarray(np.concatenate(a), F32) for a in (zs, ts, ms))


N_WIN = 1 + 2 * NB_S - 1


def _filt_kernel(z_ref, t_ref, m_ref, w1_ref, b1_ref, fr_ref, w2_ref, b2_ref, w3_ref,
                 ad_ref, f_ref, sg_ref, g_ref, nyq_ref, prev_scr):
    c = pl.program_id(0)

    @pl.when(c == 0)
    def _():
        prev_scr[...] = jnp.zeros_like(prev_scr)

    fr = fr_ref[...]
    hid = jnp.sin(fr * (_dot(z_ref[...].astype(BF16), w1_ref[...].astype(BF16)) + b1_ref[...]))
    hid = jnp.sin(fr * (_dot(hid.astype(BF16), w2_ref[...].astype(BF16)) + b2_ref[...]))
    h = _dot(hid.astype(BF16), w3_ref[...].astype(BF16))
    taps = h * jnp.exp(-t_ref[...] * ad_ref[...]) * m_ref[...]
    a = _dot(f_ref[...], taps.astype(BF16))
    g = a + sg_ref[...] * prev_scr[...]
    prev_scr[...] = a
    g_ref[...] = g
    nyq_ref[...] = g[CB:CB + 1, :]
    g_ref[CB:CB + 1, :] = jnp.zeros((1, D), F32)


def _is_bwd_chunk(c):
    return jnp.logical_or(c == 0, jnp.logical_and(c >= 2, c < 2 + NB_S))


def _window_of_chunk(c):
    return jnp.where(c <= 1, 0, jnp.maximum(c - 2, 1))


def _filter_spectra(tabs, fwd, sgn, j, w1, b1, freq, w2, b2, w3, absd):
    z_tab, t_tab, m_tab = tabs
    small = lambda shape: pl.BlockSpec(shape, lambda c: (0, 0))
    hid = _lay((1, FILTER_HIDDEN), j)
    return pl.pallas_call(
        _filt_kernel,
        grid=(N_CHUNK,),
        in_specs=[
            pl.BlockSpec((CB, FILTER_HIDDEN), lambda c: (c, 0)),
            pl.BlockSpec((CB, 1), lambda c: (c, 0)),
            pl.BlockSpec((CB, 1), lambda c: (c, 0)),
            _lay((FILTER_HIDDEN, FILTER_HIDDEN), j), hid, hid,
            _lay((FILTER_HIDDEN, FILTER_HIDDEN), j), hid,
            pl.BlockSpec((None, FILTER_HIDDEN, D),
                         lambda c: (j, 0, jnp.where(_is_bwd_chunk(c), 1, 0))),
            small((1, D)),
            small((2 * CB, CB)),
            small((2 * CB, 1)),
        ],
        out_specs=[pl.BlockSpec((None, 2 * CB, D), lambda c: (_window_of_chunk(c), 0, 0)),
                   pl.BlockSpec((None, 1, D), lambda c: (_window_of_chunk(c), 0, 0))],
        out_shape=[jax.ShapeDtypeStruct((N_WIN, 2 * CB, D), F32),
                   jax.ShapeDtypeStruct((N_WIN, 1, D), F32)],
        scratch_shapes=[pltpu.VMEM((2 * CB, D), F32)],
        compiler_params=_cp(("arbitrary",)),
        name="hy_filter",
    )(z_tab, t_tab, m_tab, w1, b1, freq, w2, b2, w3, absd, fwd, sgn)


def _conv_kernel(tc, z_ref, x0_ref, g_ref, nyq_ref, f_ref, gi_ref, db_ref, o_ref, u_scr):
    unit = pl.program_id(1)
    fmat = f_ref[...]
    gmat = gi_ref[...]
    row0 = lax.broadcasted_iota(jnp.int32, (CB, tc), 0) == 0
    db = db_ref[...]

    def emit(blk, yre, yim, ynyq):
        rows = pl.ds(blk * CB, CB)
        yspec = jnp.concatenate([yre, jnp.where(row0, ynyq, yim)], axis=0)
        y = _dot(gmat, yspec.astype(BF16))
        o_ref[rows, :] = (x0_ref[rows, :] * (y + z_ref[rows, :] * db)).astype(BF16)

    @pl.when(unit < MP // LS)
    def _():
        gre, gim, gnyq = g_ref[0, :CB, :], g_ref[0, CB:, :], nyq_ref[0]
        for s in range(LS // LP):
            u = _dot(fmat, z_ref[pl.ds(s * CB, CB), :])
            ure, unyq = u[:CB], u[CB:CB + 1]
            uim = jnp.where(row0, 0.0, u[CB:])
            emit(s, gre * ure - gim * uim, gre * uim + gim * ure, gnyq * unyq)

    @pl.when(unit >= MP // LS)
    def _():
        unyq = []
        for j in range(NB_S):
            u = _dot(fmat, z_ref[pl.ds(j * CB, CB), :])
            unyq.append(u[CB:CB + 1])
            u_scr[j] = u
            u_scr[j, CB:CB + 1, :] = jnp.zeros((1, tc), F32)
        for i in range(NB_S):
            win = [NB_S + i - j for j in range(NB_S)]
            parts_re, parts_im = [], []
            for r in range(0, CB, MAC_ROWS):
                re, im = slice(r, r + MAC_ROWS), slice(CB + r, CB + r + MAC_ROWS)
                yre = yim = None
                for j, w in enumerate(win):
                    gre, gim = g_ref[w, re, :], g_ref[w, im, :]
                    ure, uim = u_scr[j, re, :], u_scr[j, im, :]
                    pre, pim = gre * ure - gim * uim, gre * uim + gim * ure
                    yre = pre if yre is None else yre + pre
                    yim = pim if yim is None else yim + pim
                parts_re.append(yre)
                parts_im.append(yim)
            ynyq = nyq_ref[win[0]] * unyq[0]
            for j in range(1, NB_S):
                ynyq = ynyq + nyq_ref[win[j]] * unyq[j]
            emit(i, jnp.concatenate(parts_re, axis=0), jnp.concatenate(parts_im, axis=0), ynyq)


def _long_conv(z, x0, spectra, fwd, inv, d_bias, j, tc=256):
    g_win, g_nyq = spectra
    return pl.pallas_call(
        functools.partial(_conv_kernel, tc),
        grid=(D // tc, M // LS),
        in_specs=[
            pl.BlockSpec((LS, tc), lambda c, u: (u, c)),
            pl.BlockSpec((LS, tc), lambda c, u: (u, c)),
            pl.BlockSpec((N_WIN, 2 * CB, tc), lambda c, u: (0, 0, c)),
            pl.BlockSpec((N_WIN, 1, tc), lambda c, u: (0, 0, c)),
            pl.BlockSpec((2 * CB, CB), lambda c, u: (0, 0)),
            pl.BlockSpec((CB, 2 * CB), lambda c, u: (0, 0)),
            pl.BlockSpec((None, 1, tc), lambda c, u: (j, 0, c)),
        ],
        out_specs=pl.BlockSpec((LS, tc), lambda c, u: (u, c)),
        out_shape=jax.ShapeDtypeStruct((M, D), BF16),
        scratch_shapes=[pltpu.VMEM((NB_S, 2 * CB, tc), F32)],
        compiler_params=_cp(("arbitrary", "arbitrary")),
        name="hy_conv",
    )(z, x0, g_win, g_nyq, fwd, inv, d_bias)


def _rope_tables(L):
    rows = L // GRID_W
    r = np.repeat(np.arange(rows, dtype=np.float64), GRID_W)
    cidx = np.tile(np.arange(GRID_W, dtype=np.float64), rows)
    inv = ROPE_THETA ** (-np.arange(ROT_FREQS, dtype=np.float64) / ROT_FREQS)
    ar = r[:, None] * inv
    ac = cidx[:, None] * inv
    cos = np.concatenate([np.cos(ar), np.cos(ar), np.cos(ac), np.cos(ac)] * 2, axis=-1)
    sin = np.concatenate([np.sin(ar), np.sin(ar), np.sin(ac), np.sin(ac)] * 2, axis=-1)
    first_half = (np.arange(V_DIM) % (2 * ROT_FREQS)) < ROT_FREQS
    sin_a = np.where(first_half, -sin, 0.0)
    sin_b = np.where(first_half, 0.0, sin)
    return tuple(jnp.asarray(a, F32) for a in (cos, sin_a, sin_b))


def _rope(x, cos, sin_a, sin_b):
    return (x * cos + pltpu.roll(x, V_DIM - ROT_FREQS, axis=1) * sin_a
            + pltpu.roll(x, ROT_FREQS, axis=1) * sin_b)


HEADS_PER_CHUNK = CK // V_DIM


def _qkv_chunks():
    per_part = D // CK
    return [(slice(c * CK, (c + 1) * CK), c // per_part, (c % per_part) * HEADS_PER_CHUNK)
            for c in range(3 * per_part)]


def _qkv_c_kernel(last, x_ref, nw_ref, sh_ref, sc_ref, w_ref, *rest):
    if last:
        pk_ref, pv_ref, qkv_ref, nk_ref, nv_ref = rest
        nk_ref[:, 0] = pk_ref[...]
        nv_ref[:, 0] = pv_ref[...]
    else:
        qkv_ref, nk_ref, nv_ref = rest
    h = _hmod(x_ref[...], nw_ref, sh_ref, sc_ref)
    for cs, part, head0 in _qkv_chunks():
        u = _dot(h, w_ref[:, cs])
        qkv_ref[:, cs] = (u * QSCALE if part == 0 else u).astype(BF16)
        if part > 0:
            cache = nk_ref if part == 1 else nv_ref
            for s in range(TM // LP):
                for hh in range(HEADS_PER_CHUNK):
                    blk = u[s * LP:(s + 1) * LP, hh * V_DIM:(hh + 1) * V_DIM]
                    if last:
                        cache[s, N_ATTN - 1, head0 + hh] = blk
                    else:
                        cache[s, head0 + hh] = blk


def _qkv_ctx(x, nw_all, mods, layer, j, w_bf, prev):
    seqs = TM // LP
    one = pl.BlockSpec((seqs, N_HEADS, LP, V_DIM), lambda i: (i, 0, 0, 0))
    one_shape = jax.ShapeDtypeStruct((BP, N_HEADS, LP, V_DIM), F32)
    in_specs = [_tile_spec(D, _joint_tile), _norm_spec(layer, 0), _mod_spec(layer, 0),
                _mod_spec(layer, 1), _lay((D, 3 * D), j)]
    args = [x, nw_all, mods, mods, w_bf]
    if prev is None:
        cache, cache_shape = one, one_shape
    else:
        assert j == N_ATTN - 1 == 1
        in_specs += [one, one]
        args += list(prev)
        cache = pl.BlockSpec((seqs, N_ATTN, N_HEADS, LP, V_DIM), lambda i: (i, 0, 0, 0, 0))
        cache_shape = jax.ShapeDtypeStruct((BP, N_ATTN, N_HEADS, LP, V_DIM), F32)
    return pl.pallas_call(
        functools.partial(_qkv_c_kernel, prev is not None),
        grid=(N_CTX_TILES,),
        in_specs=in_specs,
        out_specs=[_tile_spec(3 * D, _joint_tile), cache, cache],
        out_shape=[jax.ShapeDtypeStruct((MP, 3 * D), BF16), cache_shape, cache_shape],
        compiler_params=_cp(("arbitrary",)),
        name="qkv_ctx",
    )(*args)


def _qkv_l_kernel(x_ref, nw_ref, sh_ref, sc_ref, w_ref, cos_ref, sa_ref, sb_ref, qkv_ref):
    h = _hmod(x_ref[...], nw_ref, sh_ref, sc_ref)
    for cs, part, _ in _qkv_chunks():
        u = _dot(h, w_ref[:, cs])
        if part == 2:
            qkv_ref[:, cs] = u.astype(BF16)
            continue
        for hh in range(HEADS_PER_CHUNK):
            r = _rope(u[:, hh * V_DIM:(hh + 1) * V_DIM], cos_ref[...], sa_ref[...], sb_ref[...])
            if part == 0:
                r = r * QSCALE
            qkv_ref[:, cs.start + hh * V_DIM:cs.start + (hh + 1) * V_DIM] = r.astype(BF16)


def _qkv_lat(x, nw_all, mods, layer, j, w_bf, ropes):
    tile = lambda i: i + N_CTX_TILES
    tab = pl.BlockSpec((TM, V_DIM), lambda i: (i % (LS // TM), 0))
    return pl.pallas_call(
        _qkv_l_kernel,
        grid=(N_LAT_TILES,),
        in_specs=[_tile_spec(D, tile), _norm_spec(layer, 0), _mod_spec(layer, 0, tile),
                  _mod_spec(layer, 1, tile), _lay((D, 3 * D), j), tab, tab, tab],
        out_specs=_tile_spec(3 * D, _joint_tile),
        out_shape=jax.ShapeDtypeStruct((MS, 3 * D), BF16),
        compiler_params=_cp(("arbitrary",)),
        name="qkv_lat",
    )(x, nw_all, mods, mods, w_bf, *ropes)


def _lambda(lv, lam_init):
    a = jnp.exp(jnp.sum(lv[0:1, :] * lv[1:2, :], axis=-1, keepdims=True))
    b = jnp.exp(jnp.sum(lv[2:3, :] * lv[3:4, :], axis=-1, keepdims=True))
    return a - b + lam_init


def _diff_attn(q, chunks, lam, lam_init, subln):
    t = q.shape[0]
    lane = lax.broadcasted_iota(jnp.int32, q.shape, 1)
    zero = jnp.zeros_like(q)
    q2 = jnp.concatenate([jnp.where(lane < HEAD_DIM, q, zero),
                          jnp.where(lane < HEAD_DIM, zero, q)], axis=0)
    m = l = acc = None
    for k, v in chunks:
        s = lax.dot_general(q2, k, (((1,), (1,)), ((), ())), preferred_element_type=F32)
        mc = jnp.max(s, axis=-1, keepdims=True)
        m_new = mc if m is None else jnp.maximum(m, mc)
        e = jnp.exp2(s - m_new)
        lc = jnp.sum(e, axis=-1, keepdims=True)
        pv = _dot(e.astype(BF16), v)
        if m is None:
            l, acc = lc, pv
        else:
            alpha = jnp.exp2(m - m_new)
            l = alpha * l + lc
            acc = alpha * acc + pv
        m = m_new
    o = acc[:t] * (1.0 / l[:t]) - acc[t:] * (lam / l[t:])
    return _rms(o, subln) * (1.0 - lam_init)


def _attn_c_kernel(lam_init, q_ref, k_ref, v_ref, lv_ref, sub_ref, o_ref):
    lam = _lambda(lv_ref[...], lam_init)
    t = LP
    probs = [(slice(s * LP, (s + 1) * LP), slice(h * V_DIM, (h + 1) * V_DIM))
             for s in range(CTX_SEQS) for h in range(N_HEADS)]
    lane = lax.broadcasted_iota(jnp.int32, (t, V_DIM), 1)
    zero = jnp.zeros((t, V_DIM), BF16)
    dn = (((1,), (1,)), ((), ()))
    s = []
    for rs, hs in probs:
        q = q_ref[rs, hs]
        q2 = jnp.concatenate([jnp.where(lane < HEAD_DIM, q, zero),
                              jnp.where(lane < HEAD_DIM, zero, q)], axis=0)
        s.append(lax.dot_general(q2, k_ref[rs, hs], dn, preferred_element_type=F32))
    e = [jnp.exp2(x - jnp.max(x, axis=-1, keepdims=True)) for x in s]
    l = [jnp.sum(x, axis=-1, keepdims=True) for x in e]
    pv = [_dot(x.astype(BF16), v_ref[rs, hs]) for x, (rs, hs) in zip(e, probs)]
    for (rs, hs), acc, lh in zip(probs, pv, l):
        o = acc[:t] * (1.0 / lh[:t]) - acc[t:] * (lam / lh[t:])
        o_ref[rs, hs] = (_rms(o, sub_ref[...]) * (1.0 - lam_init)).astype(BF16)


def _attn_ctx(qkv_c, lamv, subln, j, lam_init):
    part = lambda p: pl.BlockSpec((CTX_SEQS * LP, D), lambda b: (b, p))
    return pl.pallas_call(
        functools.partial(_attn_c_kernel, lam_init),
        grid=(BP // CTX_SEQS,),
        in_specs=[part(0), part(1), part(2),
                  pl.BlockSpec((None, 4, HEAD_DIM), lambda b: (j, 0, 0)),
                  pl.BlockSpec((None, 1, V_DIM), lambda b: (j, 0, 0))],
        out_specs=pl.BlockSpec((CTX_SEQS * LP, D), lambda b: (b, 0)),
        out_shape=jax.ShapeDtypeStruct((MP, D), BF16),
        compiler_params=_cp(("arbitrary",)),
        name="attn_ctx",
    )(qkv_c, qkv_c, qkv_c, lamv, subln)


def _attn_l_kernel(lam_init, q_ref, k_ref, v_ref, ck_ref, cv_ref, lv_ref, sub_ref, o_ref):
    lam = _lambda(lv_ref[...], lam_init)
    for h in range(ATT_HEADS):
        hs = slice(h * V_DIM, (h + 1) * V_DIM)
        chunks = [(ck_ref[h].astype(BF16), cv_ref[h].astype(BF16))]
        for c in range(LS // KEY_CHUNK):
            rows = pl.ds(c * KEY_CHUNK, KEY_CHUNK)
            chunks.append((k_ref[rows, hs], v_ref[rows, hs]))
        o = _diff_attn(q_ref[:, hs], chunks, lam, lam_init, sub_ref[...])
        o_ref[:, hs] = o.astype(BF16)


def _attn_lat(qkv_l, cache_k, cache_v, lamv, subln, j, lam_init, tq=512):
    nq = LS // tq
    width = ATT_HEADS * V_DIM
    groups = N_HEADS // ATT_HEADS
    seq = lambda part: pl.BlockSpec((LS, width), lambda b, g, q: (b, part * groups + g))
    ctx = pl.BlockSpec((None, None, ATT_HEADS, PAST, V_DIM), lambda b, g, q: (b, j, g, 0, 0))
    return pl.pallas_call(
        functools.partial(_attn_l_kernel, lam_init),
        grid=(BS, groups, nq),
        in_specs=[pl.BlockSpec((tq, width), lambda b, g, q: (b * nq + q, g)),
                  seq(1), seq(2), ctx, ctx,
                  pl.BlockSpec((None, 4, HEAD_DIM), lambda b, g, q: (j, 0, 0)),
                  pl.BlockSpec((None, 1, V_DIM), lambda b, g, q: (j, 0, 0))],
        out_specs=pl.BlockSpec((tq, width), lambda b, g, q: (b * nq + q, g)),
        out_shape=jax.ShapeDtypeStruct((MS, D), BF16),
        compiler_params=_cp(("arbitrary", "arbitrary", "arbitrary")),
        name="attn_lat",
    )(qkv_l, qkv_l, qkv_l, cache_k, cache_v, lamv, subln)


def kernel(x_prompt, x_sample, cache_k, cache_v, c, c_ctx, w_ada, b_ada, norm_w, hy_w_in, hy_b_in, hy_w_short, hy_b_short, hy_f_w1, hy_f_b1, hy_f_freq, hy_f_w2, hy_f_b2, hy_f_w3, hy_d_bias, hy_w_out, hy_b_out, at_w_qkv, at_w_out, at_lambda_q1, at_lambda_k1, at_lambda_q2, at_lambda_k2, at_subln, ffn_w_up, ffn_w_dw, ffn_b_dw, ffn_w_down):
    cond8 = jnp.concatenate([c_ctx[None, :], c, jnp.zeros((SUB - 1 - BS, D), F32)], axis=0)
    mods = _ada(cond8, w_ada, b_ada)

    fwd, inv, sgn = _dft_mats()
    tabs = _filter_tables()
    min_decay = math.log(DECAY_TARGET) / DECAY_PCT_LONG
    max_decay = math.log(DECAY_TARGET) / DECAY_PCT_SHORT
    absd = jnp.asarray(np.abs(np.linspace(min_decay, max_decay, D))[None, :], F32)
    ropes = _rope_tables(LS)

    row = lambda a: a.reshape(a.shape[0], 1, a.shape[1])
    nw_all = norm_w.reshape(DEPTH * 4, 1, D)
    w_in_bf, w_hy_out_bf = hy_w_in.astype(BF16), hy_w_out.astype(BF16)
    w_qkv_bf, w_at_out_bf = at_w_qkv.astype(BF16), at_w_out.astype(BF16)
    w_up_bf, w_down_bf = ffn_w_up.astype(BF16), ffn_w_down.astype(BF16)
    w1_pad = jnp.pad(hy_f_w1, ((0, 0), (0, FILTER_HIDDEN - EMB_DIM), (0, 0)))
    lamv = jnp.stack([at_lambda_q1, at_lambda_k1, at_lambda_q2, at_lambda_k2], axis=1)
    subln = row(at_subln)

    x_parts = [x_prompt.reshape(MP, D), x_sample.reshape(MS, D)]
    caches = None
    for i in range(DEPTH):
        j = i // 2
        if i % 2 == 0:
            x0, z = _hy_in(x_parts, nw_all, mods, i, j, w_in_bf, row(hy_b_in), hy_w_short,
                           row(hy_b_short))
            spectra = _filter_spectra(tabs, fwd, sgn, j, w1_pad, row(hy_f_b1), row(hy_f_freq),
                                      hy_f_w2, row(hy_f_b2), hy_f_w3, absd)
            a_parts = [_long_conv(z, x0, spectra, fwd, inv, row(hy_d_bias), j)]
            w_out, b_out = w_hy_out_bf, row(hy_b_out)
        else:
            lam_init = 0.8 - 0.6 * math.exp(-0.3 * i)
            x = x_parts[0]
            qkv_c, new_k, new_v = _qkv_ctx(x, nw_all, mods, i, j, w_qkv_bf, caches)
            caches = (new_k, new_v)
            qkv_l = _qkv_lat(x, nw_all, mods, i, j, w_qkv_bf, ropes)
            a_parts = [_attn_ctx(qkv_c, lamv, subln, j, lam_init),
                       _attn_lat(qkv_l, cache_k, cache_v, lamv, subln, j, lam_init)]
            w_out, b_out = w_at_out_bf, None
        y = _tail(a_parts, x_parts, nw_all, mods, i, j, w_out, b_out, w_up_bf, ffn_w_dw,
                  row(ffn_b_dw), w_down_bf, split_out=(i == DEPTH - 1))
        x_parts = list(y) if i == DEPTH - 1 else [y]

    return (x_parts[0].reshape(BP, LP, D), x_parts[1].reshape(BS, LS, D), caches[0], caches[1])
```

```python
import functools
import math

import numpy as np
import jax
import jax.numpy as jnp
from jax import lax
from jax.experimental import pallas as pl
from jax.experimental.pallas import tpu as pltpu

D = 1024
BP, LP = 16, 256
BS, LS = 2, 2048
MP = BP * LP
MS = BS * LS
M = MP + MS
DEPTH = 4
N_ATTN = DEPTH // 2
GRID_W = 64
N_HEADS = 8
HEAD_DIM = 64
V_DIM = 128
ROPE_THETA = 10000.0
ROT_FREQS = 16
EMB_BANDS = 16
EMB_DIM = 33
FILTER_HIDDEN = 64
DECAY_TARGET = 1e-2
DECAY_PCT_SHORT = 0.3
DECAY_PCT_LONG = 1.5
D_FF = 2816
EPS = 1e-6
PAST = 256

CB = 256
NB_S = LS // CB
N_CHUNK = 2 + 2 * NB_S
SUB = 8
HB = 16
HALO = 2 * HB
SLAB = 16
CK = 256
TM = 512
KEY_CHUNK = 2048
ATT_HEADS = 4
CTX_SEQS = 2
MAC_ROWS = 32
VMEM_LIMIT = 56 * 1024 * 1024
QSCALE = HEAD_DIM ** -0.5 * math.log2(math.e)

F32 = jnp.float32
BF16 = jnp.bfloat16


def _dot(a, b):
    return jnp.dot(a, b, preferred_element_type=F32)


def _rms(x, w):
    ms = jnp.mean(x * x, axis=-1, keepdims=True)
    return x * lax.rsqrt(ms + EPS) * w


def _silu(x):
    return x / (1.0 + jnp.exp(-x))


def _cp(sem, vmem=VMEM_LIMIT):
    return pltpu.CompilerParams(dimension_semantics=sem, vmem_limit_bytes=vmem)


def _lay(shape, idx):
    if idx is None:
        return pl.BlockSpec(tuple(shape), lambda i: (0,) * len(shape),
                            pipeline_mode=pl.Buffered(1))
    return pl.BlockSpec((None,) + tuple(shape), lambda i: (idx,) + (0,) * len(shape),
                        pipeline_mode=pl.Buffered(1))


CAST_STEPS = 16


def _ffn_cast_plumbing(layer, step_of):
    up_rows, dn_rows = D // CAST_STEPS, D_FF // CAST_STEPS
    in_specs = [pl.BlockSpec((None, up_rows, 2 * D_FF), lambda *g: (layer, step_of(*g), 0)),
                pl.BlockSpec((None, dn_rows, D), lambda *g: (layer, step_of(*g), 0))]
    out_specs = [pl.BlockSpec((up_rows, 2 * D_FF), lambda *g: (step_of(*g), 0)),
                 pl.BlockSpec((dn_rows, D), lambda *g: (step_of(*g), 0))]
    out_shape = [jax.ShapeDtypeStruct((D, 2 * D_FF), BF16),
                 jax.ShapeDtypeStruct((D_FF, D), BF16)]
    return in_specs, out_specs, out_shape


def _cast_blocks(in_refs, out_refs):
    for i_ref, o_ref in zip(in_refs, out_refs):
        o_ref[...] = i_ref[...].astype(BF16)


def _norm_spec(layer, k):
    return _lay((1, D), layer * 4 + k)


N_CTX_TILES = MP // TM
N_LAT_TILES = MS // TM


def _ctx_tile(i):
    return jnp.minimum(i, N_CTX_TILES - 1)


def _lat_tile(i):
    return jnp.maximum(i - N_CTX_TILES, 0)


def _joint_tile(i):
    return i


def _is_ctx():
    return pl.program_id(0) < N_CTX_TILES


def _tile_spec(width, tile_of):
    return pl.BlockSpec((TM, width), lambda i: (tile_of(i), 0))


def _ext_specs(n_rows, tile_of, width=D):
    r = TM // HB
    last_blk = n_rows // HB - 1
    return [
        pl.BlockSpec((TM, width), lambda i: (tile_of(i), 0)),
        pl.BlockSpec((HB, width), lambda i: (jnp.maximum(tile_of(i) * r - 1, 0), 0)),
        pl.BlockSpec((HB, width), lambda i: (jnp.minimum((tile_of(i) + 1) * r, last_blk), 0)),
    ]


def _fill_ext(scr, t_ref, p_ref, n_ref):
    scr[0:TM, :] = t_ref[...]
    scr[TM:TM + HB, :] = p_ref[...]
    scr[TM + HB:TM + HALO, :] = n_ref[...]


def _per_part(n_parts, fn, refs):
    if n_parts == 1:
        fn(*refs)
        return
    k = len(refs) // 2
    pl.when(_is_ctx())(lambda: fn(*refs[:k]))
    pl.when(jnp.logical_not(_is_ctx()))(lambda: fn(*refs[k:]))


def _ada_kernel(c_ref, w_ref, b_ref, o_ref):
    s = _silu(c_ref[...]).astype(BF16)
    o_ref[...] = _dot(s, w_ref[...].astype(BF16)) + b_ref[...]


def _ada(cond8, w_ada, b_ada):
    out = pl.pallas_call(
        _ada_kernel,
        grid=(DEPTH, 6),
        in_specs=[
            pl.BlockSpec((SUB, D), lambda l, k: (0, 0)),
            pl.BlockSpec((None, D, D), lambda l, k: (l, 0, k)),
            pl.BlockSpec((None, 1, D), lambda l, k: (l, 0, k)),
        ],
        out_specs=pl.BlockSpec((None, SUB, D), lambda l, k: (l * 6 + k, 0, 0)),
        out_shape=jax.ShapeDtypeStruct((DEPTH * 6, SUB, D), F32),
        compiler_params=_cp(("arbitrary", "arbitrary")),
        name="ada",
    )(cond8, w_ada, b_ada.reshape(DEPTH, 1, 6 * D))
    return out.reshape(DEPTH * 6 * SUB, 1, D)


def _mod_spec(layer, which, tile_of=_joint_tile):
    base = (layer * 6 + which) * SUB
    per_b = LS // TM

    def imap(i):
        t = tile_of(i)
        r = jnp.where(t < N_CTX_TILES, 0, 1 + (t - N_CTX_TILES) // per_b)
        return (base + r, 0, 0)

    return pl.BlockSpec((None, 1, D), imap)


def _hmod(x, nw_ref, sh_ref, sc_ref):
    return (_rms(x, nw_ref[...]) * (1.0 + sc_ref[...]) + sh_ref[...]).astype(BF16)


def _fill_h(h_scr, nw_ref, sh_ref, sc_ref, x_ref, xp_ref, xn_ref):
    h_scr[0:TM, :] = _hmod(x_ref[...], nw_ref, sh_ref, sc_ref)
    h_scr[TM:TM + HB, :] = _hmod(xp_ref[...], nw_ref, sh_ref, sc_ref)
    h_scr[TM + HB:TM + HALO, :] = _hmod(xn_ref[...], nw_ref, sh_ref, sc_ref)


def _tile_flags():
    i = pl.program_id(0)
    is_ctx = _is_ctx()
    lseq = jnp.where(is_ctx, LP, LS)
    starts = ((i * TM) & (lseq - 1)) == 0
    ends = (((i + 1) * TM) & (lseq - 1)) == 0
    return is_ctx, starts, ends


def _edge_slabs():
    return sorted({b for b in range(0, TM, LP)} | {b + LP - SLAB for b in range(0, TM, LP)})


def _conv3_bulk(u, w, b):
    return (pltpu.roll(u, 1, axis=0) * w[0:1, :] + u * w[1:2, :]
            + pltpu.roll(u, TM - 1, axis=0) * w[2:3, :] + b)


def _conv3_slab(u_ext, s, w, b, flags):
    is_ctx, starts, ends = flags
    us = u_ext[s:s + SLAB, :]
    if s == 0:
        prev = jnp.where(starts, 0.0, u_ext[TM + HB - 1:TM + HB, :])
    else:
        prev = u_ext[s - 1:s, :]
        if s % LP == 0:
            prev = jnp.where(is_ctx, 0.0, prev)
    if s + SLAB == TM:
        nxt = jnp.where(ends, 0.0, u_ext[TM + HB:TM + HB + 1, :])
    else:
        nxt = u_ext[s + SLAB:s + SLAB + 1, :]
        if (s + SLAB) % LP == 0:
            nxt = jnp.where(is_ctx, 0.0, nxt)
    rows = lax.broadcasted_iota(jnp.int32, us.shape, 0)
    up = jnp.where(rows == 0, prev, pltpu.roll(us, 1, axis=0))
    dn = jnp.where(rows == SLAB - 1, nxt, pltpu.roll(us, SLAB - 1, axis=0))
    return up * w[0:1, :] + us * w[1:2, :] + dn * w[2:3, :] + b


def _tail_kernel(n_a, n_x, has_bias, split_out, *refs):
    a_refs, x_refs = refs[:3 * n_a], refs[3 * n_a:3 * (n_a + n_x)]
    rest = list(refs[3 * (n_a + n_x):])
    w_ref = rest.pop(0)
    b_ref = rest.pop(0) if has_bias else None
    (nw1_ref, g1_ref, nw2_ref, sh_ref, sc_ref, wup_ref, wdw_ref, bdw_ref, wdn_ref, nw3_ref,
     g2_ref) = rest[:11]
    n_out = 2 if split_out else 1
    out_refs = rest[11:11 + n_out]
    a_scr, x_scr, h_scr, act_scr = rest[11 + n_out:]

    _per_part(n_a, functools.partial(_fill_ext, a_scr), a_refs)
    _per_part(n_x, functools.partial(_fill_ext, x_scr), x_refs)
    y = _dot(a_scr[...], w_ref[...])
    if has_bias:
        y = y + b_ref[...]
    x1 = x_scr[...] + g1_ref[...] * _rms(y, nw1_ref[...])
    x_scr[...] = x1
    h_scr[...] = _hmod(x1, nw2_ref, sh_ref, sc_ref)

    flags = _tile_flags()
    h = h_scr[...]
    for c in range(D_FF // CK):
        cg = slice(c * CK, (c + 1) * CK)
        cv = slice(D_FF + c * CK, D_FF + (c + 1) * CK)
        g_ext = _dot(h, wup_ref[:, cg])
        v_ext = _dot(h, wup_ref[:, cv])
        wg, bg, wv, bv = wdw_ref[:, cg], bdw_ref[:, cg], wdw_ref[:, cv], bdw_ref[:, cv]
        g = _conv3_bulk(g_ext[0:TM, :], wg, bg)
        val = _conv3_bulk(v_ext[0:TM, :], wv, bv)
        act_scr[:, cg] = (_silu(g) * val).astype(BF16)
        for s in _edge_slabs():
            g = _conv3_slab(g_ext, s, wg, bg, flags)
            val = _conv3_slab(v_ext, s, wv, bv, flags)
            act_scr[s:s + SLAB, cg] = (_silu(g) * val).astype(BF16)
    y = _dot(act_scr[...], wdn_ref[...])
    res = x_scr[0:TM, :] + g2_ref[...] * _rms(y, nw3_ref[...])
    if split_out:
        is_ctx = flags[0]

        @pl.when(is_ctx)
        def _():
            out_refs[0][...] = res

        @pl.when(jnp.logical_not(is_ctx))
        def _():
            out_refs[1][...] = res
    else:
        out_refs[0][...] = res


def _tail(a_parts, x_parts, nw_all, mods, layer, j, w_out_bf, b_out, wup_bf, w_dw, b_dw, wdn_bf,
          split_out):
    k = w_out_bf.shape[1]
    has_bias = b_out is not None

    def specs(parts, width):
        if len(parts) == 1:
            return _ext_specs(M, _joint_tile, width)
        return _ext_specs(MP, _ctx_tile, width) + _ext_specs(MS, _lat_tile, width)

    in_specs = specs(a_parts, k) + specs(x_parts, D) + [_lay((k, D), j)]
    args = [p for p in a_parts for _ in range(3)] + [p for p in x_parts for _ in range(3)]
    args.append(w_out_bf)
    if has_bias:
        in_specs.append(_lay((1, D), j))
        args.append(b_out)
    in_specs += [
        _norm_spec(layer, 1), _mod_spec(layer, 2),
        _norm_spec(layer, 2), _mod_spec(layer, 3), _mod_spec(layer, 4),
        _lay((D, 2 * D_FF), None), _lay((3, 2 * D_FF), layer), _lay((1, 2 * D_FF), layer),
        _lay((D_FF, D), None),
        _norm_spec(layer, 3), _mod_spec(layer, 5),
    ]
    args += [nw_all, mods, nw_all, mods, mods, wup_bf, w_dw, b_dw, wdn_bf, nw_all, mods]
    if split_out:
        out_specs = [_tile_spec(D, _ctx_tile), _tile_spec(D, _lat_tile)]
        out_shape = [jax.ShapeDtypeStruct((MP, D), F32), jax.ShapeDtypeStruct((MS, D), F32)]
    else:
        out_specs = _tile_spec(D, _joint_tile)
        out_shape = jax.ShapeDtypeStruct((M, D), F32)
    return pl.pallas_call(
        functools.partial(_tail_kernel, len(a_parts), len(x_parts), has_bias, split_out),
        grid=(M // TM,),
        in_specs=in_specs,
        out_specs=out_specs,
        out_shape=out_shape,
        scratch_shapes=[pltpu.VMEM((TM + HALO, k), BF16), pltpu.VMEM((TM + HALO, D), F32),
                        pltpu.VMEM((TM + HALO, D), BF16), pltpu.VMEM((TM, D_FF), BF16)],
        compiler_params=_cp(("arbitrary",)),
        name="tail",
    )(*args)


def _hy_in_kernel(n_parts, *refs):
    x_refs = refs[:3 * n_parts]
    (nw_ref, sh_ref, sc_ref, w_ref, b_ref, ws_ref, bs_ref, x0_ref, z_ref,
     h_scr) = refs[3 * n_parts:]
    _per_part(n_parts, functools.partial(_fill_h, h_scr, nw_ref, sh_ref, sc_ref), x_refs)
    flags = _tile_flags()
    h = h_scr[...]
    for c in range(D // CK):
        cc = slice(c * CK, (c + 1) * CK)
        u_ext, ws, bs = [], [], []
        for s in range(3):
            cs = slice(s * D + c * CK, s * D + (c + 1) * CK)
            u_ext.append(_dot(h, w_ref[:, cs]) + b_ref[:, cs])
            ws.append(ws_ref[:, cs])
            bs.append(bs_ref[:, cs])
        out = [_conv3_bulk(u_ext[s][0:TM, :], ws[s], bs[s]) for s in range(3)]
        x0_ref[:, cc] = out[0].astype(BF16)
        z_ref[:, cc] = (out[2] * out[1]).astype(BF16)
        for r in _edge_slabs():
            out = [_conv3_slab(u_ext[s], r, ws[s], bs[s], flags) for s in range(3)]
            x0_ref[r:r + SLAB, cc] = out[0].astype(BF16)
            z_ref[r:r + SLAB, cc] = (out[2] * out[1]).astype(BF16)


def _hy_in(x_parts, nw_all, mods, layer, j, w_bf, b_in, w_short, b_short):
    if len(x_parts) == 1:
        x_specs = _ext_specs(M, _joint_tile)
    else:
        x_specs = _ext_specs(MP, _ctx_tile) + _ext_specs(MS, _lat_tile)
    x_args = [a for a in x_parts for _ in range(3)]
    return pl.pallas_call(
        functools.partial(_hy_in_kernel, len(x_parts)),
        grid=(M // TM,),
        in_specs=x_specs + [
            _norm_spec(layer, 0), _mod_spec(layer, 0), _mod_spec(layer, 1),
            _lay((D, 3 * D), j), _lay((1, 3 * D), j), _lay((3, 3 * D), j), _lay((1, 3 * D), j),
        ],
        out_specs=[_tile_spec(D, _joint_tile)] * 2,
        out_shape=[jax.ShapeDtypeStruct((M, D), BF16)] * 2,
        scratch_shapes=[pltpu.VMEM((TM + HALO, D), BF16)],
        compiler_params=_cp(("arbitrary",)),
        name="hy_in",
    )(*x_args, nw_all, mods, mods, w_bf, b_in, w_short, b_short)


def _dft_mats():
    n = np.arange(CB, dtype=np.float64)
    f = np.arange(CB, dtype=np.float64)
    ang = 2.0 * np.pi * np.outer(f, n) / (2 * CB)
    fwd = np.concatenate([np.cos(ang), -np.sin(ang)], axis=0)
    fwd[CB] = np.cos(np.pi * n)
    scale = np.full((2 * CB, 1), 2.0 / (2 * CB))
    scale[0] = scale[CB] = 1.0 / (2 * CB)
    inv = (fwd * scale).T
    sgn = np.where(np.arange(CB) % 2 == 0, 1.0, -1.0)
    sgn2 = np.concatenate([sgn, sgn])[:, None]
    sgn2[CB] = 1.0
    return (jnp.asarray(fwd, F32).astype(BF16), jnp.asarray(inv, F32).astype(BF16),
            jnp.asarray(sgn2, F32))


def _filter_features(L):
    t = np.linspace(0.0, 1.0, L)[:, None]
    w = 2.0 * np.pi * np.arange(L)[:, None] / L
    bands = np.linspace(1e-4, EMB_BANDS - 1, EMB_BANDS)
    z = np.concatenate([t, np.cos(bands * w), -np.sin(bands * w)], axis=-1)
    return t, z


def _filter_tables():
    zs, ts, ms = [], [], []
    for L in (LP, LS):
        t, z = _filter_features(L)
        z = np.pad(z, ((0, 0), (0, FILTER_HIDDEN - EMB_DIM)))
        idx = np.abs(np.arange(2 * L) - L) % L
        zs.append(z[idx])
        ts.append(t[idx])
        ms.append((np.arange(2 * L) != 0).astype(np.float64)[:, None])
    return tuple(jnp.asarray(np.concatenate(a), F32) for a in (zs, ts, ms))


N_WIN = 1 + 2 * NB_S - 1


def _filt_kernel(z_ref, t_ref, m_ref, w1_ref, b1_ref, fr_ref, w2_ref, b2_ref, w3_ref,
                 ad_ref, f_ref, sg_ref, g_ref, nyq_ref, prev_scr):
    c = pl.program_id(0)

    @pl.when(c == 0)
    def _():
        prev_scr[...] = jnp.zeros_like(prev_scr)

    fr = fr_ref[...]
    hid = jnp.sin(fr * (_dot(z_ref[...].astype(BF16), w1_ref[...].astype(BF16)) + b1_ref[...]))
    hid = jnp.sin(fr * (_dot(hid.astype(BF16), w2_ref[...].astype(BF16)) + b2_ref[...]))
    h = _dot(hid.astype(BF16), w3_ref[...].astype(BF16))
    taps = h * jnp.exp(-t_ref[...] * ad_ref[...]) * m_ref[...]
    a = _dot(f_ref[...], taps.astype(BF16))
    g = a + sg_ref[...] * prev_scr[...]
    prev_scr[...] = a
    g_ref[...] = g
    nyq_ref[...] = g[CB:CB + 1, :]
    g_ref[CB:CB + 1, :] = jnp.zeros((1, D), F32)


def _is_bwd_chunk(c):
    return jnp.logical_or(c == 0, jnp.logical_and(c >= 2, c < 2 + NB_S))


def _window_of_chunk(c):
    return jnp.where(c <= 1, 0, jnp.maximum(c - 2, 1))


def _filter_spectra(tabs, fwd, sgn, j, w1, b1, freq, w2, b2, w3, absd):
    z_tab, t_tab, m_tab = tabs
    small = lambda shape: pl.BlockSpec(shape, lambda c: (0, 0))
    hid = _lay((1, FILTER_HIDDEN), j)
    return pl.pallas_call(
        _filt_kernel,
        grid=(N_CHUNK,),
        in_specs=[
            pl.BlockSpec((CB, FILTER_HIDDEN), lambda c: (c, 0)),
            pl.BlockSpec((CB, 1), lambda c: (c, 0)),
            pl.BlockSpec((CB, 1), lambda c: (c, 0)),
            _lay((FILTER_HIDDEN, FILTER_HIDDEN), j), hid, hid,
            _lay((FILTER_HIDDEN, FILTER_HIDDEN), j), hid,
            pl.BlockSpec((None, FILTER_HIDDEN, D),
                         lambda c: (j, 0, jnp.where(_is_bwd_chunk(c), 1, 0))),
            small((1, D)),
            small((2 * CB, CB)),
            small((2 * CB, 1)),
        ],
        out_specs=[pl.BlockSpec((None, 2 * CB, D), lambda c: (_window_of_chunk(c), 0, 0)),
                   pl.BlockSpec((None, 1, D), lambda c: (_window_of_chunk(c), 0, 0))],
        out_shape=[jax.ShapeDtypeStruct((N_WIN, 2 * CB, D), F32),
                   jax.ShapeDtypeStruct((N_WIN, 1, D), F32)],
        scratch_shapes=[pltpu.VMEM((2 * CB, D), F32)],
        compiler_params=_cp(("arbitrary",)),
        name="hy_filter",
    )(z_tab, t_tab, m_tab, w1, b1, freq, w2, b2, w3, absd, fwd, sgn)


def _conv_kernel(tc, n_cast, z_ref, x0_ref, g_ref, nyq_ref, f_ref, gi_ref, db_ref, *rest):
    o_ref, u_scr = rest[n_cast], rest[-1]
    _cast_blocks(rest[:n_cast], rest[n_cast + 1:-1])
    unit = pl.program_id(1)
    fmat = f_ref[...]
    gmat = gi_ref[...]
    row0 = lax.broadcasted_iota(jnp.int32, (CB, tc), 0) == 0
    db = db_ref[...]

    def emit(blk, yre, yim, ynyq):
        rows = pl.ds(blk * CB, CB)
        yspec = jnp.concatenate([yre, jnp.where(row0, ynyq, yim)], axis=0)
        y = _dot(gmat, yspec.astype(BF16))
        o_ref[rows, :] = (x0_ref[rows, :] * (y + z_ref[rows, :] * db)).astype(BF16)

    @pl.when(unit < MP // LS)
    def _():
        gre, gim, gnyq = g_ref[0, :CB, :], g_ref[0, CB:, :], nyq_ref[0]
        for s in range(LS // LP):
            u = _dot(fmat, z_ref[pl.ds(s * CB, CB), :])
            ure, unyq = u[:CB], u[CB:CB + 1]
            uim = jnp.where(row0, 0.0, u[CB:])
            emit(s, gre * ure - gim * uim, gre * uim + gim * ure, gnyq * unyq)

    @pl.when(unit >= MP // LS)
    def _():
        unyq = []
        for j in range(NB_S):
            u = _dot(fmat, z_ref[pl.ds(j * CB, CB), :])
            unyq.append(u[CB:CB + 1])
            u_scr[j] = u
            u_scr[j, CB:CB + 1, :] = jnp.zeros((1, tc), F32)
        for i in range(NB_S):
            win = [NB_S + i - j for j in range(NB_S)]
            parts_re, parts_im = [], []
            for r in range(0, CB, MAC_ROWS):
                re, im = slice(r, r + MAC_ROWS), slice(CB + r, CB + r + MAC_ROWS)
                yre = yim = None
                for j, w in enumerate(win):
                    gre, gim = g_ref[w, re, :], g_ref[w, im, :]
                    ure, uim = u_scr[j, re, :], u_scr[j, im, :]
                    pre, pim = gre * ure - gim * uim, gre * uim + gim * ure
                    yre = pre if yre is None else yre + pre
                    yim = pim if yim is None else yim + pim
                parts_re.append(yre)
                parts_im.append(yim)
            ynyq = nyq_ref[win[0]] * unyq[0]
            for j in range(1, NB_S):
                ynyq = ynyq + nyq_ref[win[j]] * unyq[j]
            emit(i, jnp.concatenate(parts_re, axis=0), jnp.concatenate(parts_im, axis=0), ynyq)


def _long_conv(z, x0, spectra, fwd, inv, d_bias, j, cast_layer, ffn_w, tc=256):
    g_win, g_nyq = spectra
    units = M // LS
    assert (D // tc) * units == CAST_STEPS
    c_in, c_out, c_shape = ([], [], []) if cast_layer is None else _ffn_cast_plumbing(
        cast_layer, lambda c, u: c * units + u)
    return pl.pallas_call(
        functools.partial(_conv_kernel, tc, len(c_in)),
        grid=(D // tc, units),
        in_specs=[
            pl.BlockSpec((LS, tc), lambda c, u: (u, c)),
            pl.BlockSpec((LS, tc), lambda c, u: (u, c)),
            pl.BlockSpec((N_WIN, 2 * CB, tc), lambda c, u: (0, 0, c)),
            pl.BlockSpec((N_WIN, 1, tc), lambda c, u: (0, 0, c)),
            pl.BlockSpec((2 * CB, CB), lambda c, u: (0, 0)),
            pl.BlockSpec((CB, 2 * CB), lambda c, u: (0, 0)),
            pl.BlockSpec((None, 1, tc), lambda c, u: (j, 0, c)),
        ] + c_in,
        out_specs=[pl.BlockSpec((LS, tc), lambda c, u: (u, c))] + c_out,
        out_shape=[jax.ShapeDtypeStruct((M, D), BF16)] + c_shape,
        scratch_shapes=[pltpu.VMEM((NB_S, 2 * CB, tc), F32)],
        compiler_params=_cp(("arbitrary", "arbitrary")),
        name="hy_conv",
    )(z, x0, g_win, g_nyq, fwd, inv, d_bias, *(ffn_w if c_in else ()))


def _rope_tables(L):
    rows = L // GRID_W
    r = np.repeat(np.arange(rows, dtype=np.float64), GRID_W)
    cidx = np.tile(np.arange(GRID_W, dtype=np.float64), rows)
    inv = ROPE_THETA ** (-np.arange(ROT_FREQS, dtype=np.float64) / ROT_FREQS)
    ar = r[:, None] * inv
    ac = cidx[:, None] * inv
    cos = np.concatenate([np.cos(ar), np.cos(ar), np.cos(ac), np.cos(ac)] * 2, axis=-1)
    sin = np.concatenate([np.sin(ar), np.sin(ar), np.sin(ac), np.sin(ac)] * 2, axis=-1)
    first_half = (np.arange(V_DIM) % (2 * ROT_FREQS)) < ROT_FREQS
    sin_a = np.where(first_half, -sin, 0.0)
    sin_b = np.where(first_half, 0.0, sin)
    return tuple(jnp.asarray(a, F32) for a in (cos, sin_a, sin_b))


def _rope(x, cos, sin_a, sin_b):
    return (x * cos + pltpu.roll(x, V_DIM - ROT_FREQS, axis=1) * sin_a
            + pltpu.roll(x, ROT_FREQS, axis=1) * sin_b)


HEADS_PER_CHUNK = CK // V_DIM


def _qkv_chunks():
    per_part = D // CK
    return [(slice(c * CK, (c + 1) * CK), c // per_part, (c % per_part) * HEADS_PER_CHUNK)
            for c in range(3 * per_part)]


def _qkv_c_kernel(last, x_ref, nw_ref, sh_ref, sc_ref, w_ref, *rest):
    if last:
        pk_ref, pv_ref, qkv_ref, nk_ref, nv_ref = rest
        nk_ref[:, 0] = pk_ref[...]
        nv_ref[:, 0] = pv_ref[...]
    else:
        qkv_ref, nk_ref, nv_ref = rest
    h = _hmod(x_ref[...], nw_ref, sh_ref, sc_ref)
    for cs, part, head0 in _qkv_chunks():
        u = _dot(h, w_ref[:, cs])
        qkv_ref[:, cs] = (u * QSCALE if part == 0 else u).astype(BF16)
        if part > 0:
            cache = nk_ref if part == 1 else nv_ref
            for s in range(TM // LP):
                for hh in range(HEADS_PER_CHUNK):
                    blk = u[s * LP:(s + 1) * LP, hh * V_DIM:(hh + 1) * V_DIM]
                    if last:
                        cache[s, N_ATTN - 1, head0 + hh] = blk
                    else:
                        cache[s, head0 + hh] = blk


def _qkv_ctx(x, nw_all, mods, layer, j, w_bf, prev):
    seqs = TM // LP
    one = pl.BlockSpec((seqs, N_HEADS, LP, V_DIM), lambda i: (i, 0, 0, 0))
    one_shape = jax.ShapeDtypeStruct((BP, N_HEADS, LP, V_DIM), F32)
    in_specs = [_tile_spec(D, _joint_tile), _norm_spec(layer, 0), _mod_spec(layer, 0),
                _mod_spec(layer, 1), _lay((D, 3 * D), j)]
    args = [x, nw_all, mods, mods, w_bf]
    if prev is None:
        cache, cache_shape = one, one_shape
    else:
        assert j == N_ATTN - 1 == 1
        in_specs += [one, one]
        args += list(prev)
        cache = pl.BlockSpec((seqs, N_ATTN, N_HEADS, LP, V_DIM), lambda i: (i, 0, 0, 0, 0))
        cache_shape = jax.ShapeDtypeStruct((BP, N_ATTN, N_HEADS, LP, V_DIM), F32)
    return pl.pallas_call(
        functools.partial(_qkv_c_kernel, prev is not None),
        grid=(N_CTX_TILES,),
        in_specs=in_specs,
        out_specs=[_tile_spec(3 * D, _joint_tile), cache, cache],
        out_shape=[jax.ShapeDtypeStruct((MP, 3 * D), BF16), cache_shape, cache_shape],
        compiler_params=_cp(("arbitrary",)),
        name="qkv_ctx",
    )(*args)


def _qkv_l_kernel(x_ref, nw_ref, sh_ref, sc_ref, w_ref, cos_ref, sa_ref, sb_ref, qkv_ref):
    h = _hmod(x_ref[...], nw_ref, sh_ref, sc_ref)
    for cs, part, _ in _qkv_chunks():
        u = _dot(h, w_ref[:, cs])
        if part == 2:
            qkv_ref[:, cs] = u.astype(BF16)
            continue
        for hh in range(HEADS_PER_CHUNK):
            r = _rope(u[:, hh * V_DIM:(hh + 1) * V_DIM], cos_ref[...], sa_ref[...], sb_ref[...])
            if part == 0:
                r = r * QSCALE
            qkv_ref[:, cs.start + hh * V_DIM:cs.start + (hh + 1) * V_DIM] = r.astype(BF16)


def _qkv_lat(x, nw_all, mods, layer, j, w_bf, ropes):
    tile = lambda i: i + N_CTX_TILES
    tab = pl.BlockSpec((TM, V_DIM), lambda i: (i % (LS // TM), 0))
    return pl.pallas_call(
        _qkv_l_kernel,
        grid=(N_LAT_TILES,),
        in_specs=[_tile_spec(D, tile), _norm_spec(layer, 0), _mod_spec(layer, 0, tile),
                  _mod_spec(layer, 1, tile), _lay((D, 3 * D), j), tab, tab, tab],
        out_specs=_tile_spec(3 * D, _joint_tile),
        out_shape=jax.ShapeDtypeStruct((MS, 3 * D), BF16),
        compiler_params=_cp(("arbitrary",)),
        name="qkv_lat",
    )(x, nw_all, mods, mods, w_bf, *ropes)


def _lambda(lv, lam_init):
    a = jnp.exp(jnp.sum(lv[0:1, :] * lv[1:2, :], axis=-1, keepdims=True))
    b = jnp.exp(jnp.sum(lv[2:3, :] * lv[3:4, :], axis=-1, keepdims=True))
    return a - b + lam_init


def _diff_attn(q, chunks, lam, lam_init, subln):
    t = q.shape[0]
    lane = lax.broadcasted_iota(jnp.int32, q.shape, 1)
    zero = jnp.zeros_like(q)
    q2 = jnp.concatenate([jnp.where(lane < HEAD_DIM, q, zero),
                          jnp.where(lane < HEAD_DIM, zero, q)], axis=0)
    m = l = acc = None
    for k, v in chunks:
        s = lax.dot_general(q2, k, (((1,), (1,)), ((), ())), preferred_element_type=F32)
        mc = jnp.max(s, axis=-1, keepdims=True)
        m_new = mc if m is None else jnp.maximum(m, mc)
        e = jnp.exp2(s - m_new)
        lc = jnp.sum(e, axis=-1, keepdims=True)
        pv = _dot(e.astype(BF16), v)
        if m is None:
            l, acc = lc, pv
        else:
            alpha = jnp.exp2(m - m_new)
            l = alpha * l + lc
            acc = alpha * acc + pv
        m = m_new
    o = acc[:t] * (1.0 / l[:t]) - acc[t:] * (lam / l[t:])
    return _rms(o, subln) * (1.0 - lam_init)


def _attn_c_kernel(lam_init, q_ref, k_ref, v_ref, lv_ref, sub_ref, o_ref):
    lam = _lambda(lv_ref[...], lam_init)
    t = LP
    probs = [(slice(s * LP, (s + 1) * LP), slice(h * V_DIM, (h + 1) * V_DIM))
             for s in range(CTX_SEQS) for h in range(N_HEADS)]
    lane = lax.broadcasted_iota(jnp.int32, (t, V_DIM), 1)
    zero = jnp.zeros((t, V_DIM), BF16)
    dn = (((1,), (1,)), ((), ()))
    s = []
    for rs, hs in probs:
        q = q_ref[rs, hs]
        q2 = jnp.concatenate([jnp.where(lane < HEAD_DIM, q, zero),
                              jnp.where(lane < HEAD_DIM, zero, q)], axis=0)
        s.append(lax.dot_general(q2, k_ref[rs, hs], dn, preferred_element_type=F32))
    e = [jnp.exp2(x - jnp.max(x, axis=-1, keepdims=True)) for x in s]
    l = [jnp.sum(x, axis=-1, keepdims=True) for x in e]
    pv = [_dot(x.astype(BF16), v_ref[rs, hs]) for x, (rs, hs) in zip(e, probs)]
    for (rs, hs), acc, lh in zip(probs, pv, l):
        o = acc[:t] * (1.0 / lh[:t]) - acc[t:] * (lam / lh[t:])
        o_ref[rs, hs] = (_rms(o, sub_ref[...]) * (1.0 - lam_init)).astype(BF16)


def _attn_ctx(qkv_c, lamv, subln, j, lam_init):
    part = lambda p: pl.BlockSpec((CTX_SEQS * LP, D), lambda b: (b, p))
    return pl.pallas_call(
        functools.partial(_attn_c_kernel, lam_init),
        grid=(BP // CTX_SEQS,),
        in_specs=[part(0), part(1), part(2),
                  pl.BlockSpec((None, 4, HEAD_DIM), lambda b: (j, 0, 0)),
                  pl.BlockSpec((None, 1, V_DIM), lambda b: (j, 0, 0))],
        out_specs=pl.BlockSpec((CTX_SEQS * LP, D), lambda b: (b, 0)),
        out_shape=jax.ShapeDtypeStruct((MP, D), BF16),
        compiler_params=_cp(("arbitrary",)),
        name="attn_ctx",
    )(qkv_c, qkv_c, qkv_c, lamv, subln)


def _attn_l_kernel(lam_init, n_cast, q_ref, k_ref, v_ref, ck_ref, cv_ref, lv_ref, sub_ref, *rest):
    o_ref = rest[n_cast]
    _cast_blocks(rest[:n_cast], rest[n_cast + 1:])
    lam = _lambda(lv_ref[...], lam_init)
    for h in range(ATT_HEADS):
        hs = slice(h * V_DIM, (h + 1) * V_DIM)
        chunks = [(ck_ref[h].astype(BF16), cv_ref[h].astype(BF16))]
        for c in range(LS // KEY_CHUNK):
            rows = pl.ds(c * KEY_CHUNK, KEY_CHUNK)
            chunks.append((k_ref[rows, hs], v_ref[rows, hs]))
        o = _diff_attn(q_ref[:, hs], chunks, lam, lam_init, sub_ref[...])
        o_ref[:, hs] = o.astype(BF16)


def _attn_lat(qkv_l, cache_k, cache_v, lamv, subln, j, lam_init, cast_layer, ffn_w, tq=512):
    nq = LS // tq
    width = ATT_HEADS * V_DIM
    groups = N_HEADS // ATT_HEADS
    assert BS * groups * nq == CAST_STEPS
    c_in, c_out, c_shape = ([], [], []) if cast_layer is None else _ffn_cast_plumbing(
        cast_layer, lambda b, g, q: (b * groups + g) * nq + q)
    seq = lambda part: pl.BlockSpec((LS, width), lambda b, g, q: (b, part * groups + g))
    ctx = pl.BlockSpec((None, None, ATT_HEADS, PAST, V_DIM), lambda b, g, q: (b, j, g, 0, 0))
    return pl.pallas_call(
        functools.partial(_attn_l_kernel, lam_init, len(c_in)),
        grid=(BS, groups, nq),
        in_specs=[pl.BlockSpec((tq, width), lambda b, g, q: (b * nq + q, g)),
                  seq(1), seq(2), ctx, ctx,
                  pl.BlockSpec((None, 4, HEAD_DIM), lambda b, g, q: (j, 0, 0)),
                  pl.BlockSpec((None, 1, V_DIM), lambda b, g, q: (j, 0, 0))] + c_in,
        out_specs=[pl.BlockSpec((tq, width), lambda b, g, q: (b * nq + q, g))] + c_out,
        out_shape=[jax.ShapeDtypeStruct((MS, D), BF16)] + c_shape,
        compiler_params=_cp(("arbitrary", "arbitrary", "arbitrary")),
        name="attn_lat",
    )(qkv_l, qkv_l, qkv_l, cache_k, cache_v, lamv, subln, *(ffn_w if c_in else ()))


def kernel(x_prompt, x_sample, cache_k, cache_v, c, c_ctx, w_ada, b_ada, norm_w, hy_w_in, hy_b_in, hy_w_short, hy_b_short, hy_f_w1, hy_f_b1, hy_f_freq, hy_f_w2, hy_f_b2, hy_f_w3, hy_d_bias, hy_w_out, hy_b_out, at_w_qkv, at_w_out, at_lambda_q1, at_lambda_k1, at_lambda_q2, at_lambda_k2, at_subln, ffn_w_up, ffn_w_dw, ffn_b_dw, ffn_w_down):
    cond8 = jnp.concatenate([c_ctx[None, :], c, jnp.zeros((SUB - 1 - BS, D), F32)], axis=0)
    mods = _ada(cond8, w_ada, b_ada)

    fwd, inv, sgn = _dft_mats()
    tabs = _filter_tables()
    min_decay = math.log(DECAY_TARGET) / DECAY_PCT_LONG
    max_decay = math.log(DECAY_TARGET) / DECAY_PCT_SHORT
    absd = jnp.asarray(np.abs(np.linspace(min_decay, max_decay, D))[None, :], F32)
    ropes = _rope_tables(LS)

    row = lambda a: a.reshape(a.shape[0], 1, a.shape[1])
    nw_all = norm_w.reshape(DEPTH * 4, 1, D)
    w_in_bf, w_hy_out_bf = hy_w_in.astype(BF16), hy_w_out.astype(BF16)
    w_qkv_bf, w_at_out_bf = at_w_qkv.astype(BF16), at_w_out.astype(BF16)
    ffn_bf = (ffn_w_up[0].astype(BF16), ffn_w_down[0].astype(BF16))
    ffn_w = (ffn_w_up, ffn_w_down)
    w1_pad = jnp.pad(hy_f_w1, ((0, 0), (0, FILTER_HIDDEN - EMB_DIM), (0, 0)))
    lamv = jnp.stack([at_lambda_q1, at_lambda_k1, at_lambda_q2, at_lambda_k2], axis=1)
    subln = row(at_subln)

    x_parts = [x_prompt.reshape(MP, D), x_sample.reshape(MS, D)]
    caches = None
    for i in range(DEPTH):
        j = i // 2
        if i % 2 == 0:
            x0, z = _hy_in(x_parts, nw_all, mods, i, j, w_in_bf, row(hy_b_in), hy_w_short,
                           row(hy_b_short))
            spectra = _filter_spectra(tabs, fwd, sgn, j, w1_pad, row(hy_f_b1), row(hy_f_freq),
                                      hy_f_w2, row(hy_f_b2), hy_f_w3, absd)
            a, *next_ffn_bf = _long_conv(z, x0, spectra, fwd, inv, row(hy_d_bias), j,
                                         i + 1 if i + 1 < DEPTH else None, ffn_w)
            a_parts = [a]
            w_out, b_out = w_hy_out_bf, row(hy_b_out)
        else:
            lam_init = 0.8 - 0.6 * math.exp(-0.3 * i)
            x = x_parts[0]
            qkv_c, new_k, new_v = _qkv_ctx(x, nw_all, mods, i, j, w_qkv_bf, caches)
            caches = (new_k, new_v)
            qkv_l = _qkv_lat(x, nw_all, mods, i, j, w_qkv_bf, ropes)
            o_lat, *next_ffn_bf = _attn_lat(qkv_l, cache_k, cache_v, lamv, subln, j, lam_init,
                                            i + 1 if i + 1 < DEPTH else None, ffn_w)
            a_parts = [_attn_ctx(qkv_c, lamv, subln, j, lam_init), o_lat]
            w_out, b_out = w_at_out_bf, None
        y = _tail(a_parts, x_parts, nw_all, mods, i, j, w_out, b_out, ffn_bf[0], ffn_w_dw,
                  row(ffn_b_dw), ffn_bf[1], split_out=(i == DEPTH - 1))
        x_parts = list(y) if i == DEPTH - 1 else [y]
        ffn_bf = tuple(next_ffn_bf)

    return (x_parts[0].reshape(BP, LP, D), x_parts[1].reshape(BS, LS, D), caches[0], caches[1])
```

```python
import functools
import math

import numpy as np
import jax
import jax.numpy as jnp
from jax import lax
from jax.experimental import pallas as pl
from jax.experimental.pallas import tpu as pltpu

D = 1024
BP, LP = 16, 256
BS, LS = 2, 2048
MP = BP * LP
MS = BS * LS
M = MP + MS
DEPTH = 4
N_ATTN = DEPTH // 2
GRID_W = 64
N_HEADS = 8
HEAD_DIM = 64
V_DIM = 128
ROPE_THETA = 10000.0
ROT_FREQS = 16
EMB_BANDS = 16
EMB_DIM = 33
FILTER_HIDDEN = 64
DECAY_TARGET = 1e-2
DECAY_PCT_SHORT = 0.3
DECAY_PCT_LONG = 1.5
D_FF = 2816
EPS = 1e-6
PAST = 256

CB = 256
NB_S = LS // CB
N_CHUNK = 2 + 2 * NB_S
SUB = 8
HB = 16
HALO = 2 * HB
SLAB = 16
CK = 256
TM = 512
KEY_CHUNK = 2048
ATT_HEADS = 4
CTX_SEQS = 2
MAC_ROWS = 32
VMEM_LIMIT = 56 * 1024 * 1024
QSCALE = HEAD_DIM ** -0.5 * math.log2(math.e)

F32 = jnp.float32
BF16 = jnp.bfloat16


def _dot(a, b):
    return jnp.dot(a, b, preferred_element_type=F32)


def _rms(x, w):
    ms = jnp.mean(x * x, axis=-1, keepdims=True)
    return x * lax.rsqrt(ms + EPS) * w


def _silu(x):
    return x / (1.0 + jnp.exp(-x))


def _cp(sem, vmem=VMEM_LIMIT):
    return pltpu.CompilerParams(dimension_semantics=sem, vmem_limit_bytes=vmem)


def _lay(shape, idx):
    if idx is None:
        return pl.BlockSpec(tuple(shape), lambda i: (0,) * len(shape),
                            pipeline_mode=pl.Buffered(1))
    return pl.BlockSpec((None,) + tuple(shape), lambda i: (idx,) + (0,) * len(shape),
                        pipeline_mode=pl.Buffered(1))


CAST_STEPS = 16


def _cast_plumbing(jobs, step_of):
    in_specs, out_specs, out_shape, args = [], [], [], []
    for w, layer in jobs:
        _, rows, cols = w.shape
        blk = rows // CAST_STEPS
        in_specs.append(pl.BlockSpec((None, blk, cols),
                                     lambda *g, layer=layer: (layer, step_of(*g), 0)))
        out_specs.append(pl.BlockSpec((blk, cols), lambda *g: (step_of(*g), 0)))
        out_shape.append(jax.ShapeDtypeStruct((rows, cols), BF16))
        args.append(w)
    return in_specs, out_specs, out_shape, args


def _cast_blocks(in_refs, out_refs):
    for i_ref, o_ref in zip(in_refs, out_refs):
        o_ref[...] = i_ref[...].astype(BF16)


def _norm_spec(layer, k):
    return _lay((1, D), layer * 4 + k)


N_CTX_TILES = MP // TM
N_LAT_TILES = MS // TM


def _ctx_tile(i):
    return jnp.minimum(i, N_CTX_TILES - 1)


def _lat_tile(i):
    return jnp.maximum(i - N_CTX_TILES, 0)


def _joint_tile(i):
    return i


def _is_ctx():
    return pl.program_id(0) < N_CTX_TILES


def _tile_spec(width, tile_of):
    return pl.BlockSpec((TM, width), lambda i: (tile_of(i), 0))


def _ext_specs(n_rows, tile_of, width=D):
    r = TM // HB
    last_blk = n_rows // HB - 1
    return [
        pl.BlockSpec((TM, width), lambda i: (tile_of(i), 0)),
        pl.BlockSpec((HB, width), lambda i: (jnp.maximum(tile_of(i) * r - 1, 0), 0)),
        pl.BlockSpec((HB, width), lambda i: (jnp.minimum((tile_of(i) + 1) * r, last_blk), 0)),
    ]


def _fill_ext(scr, t_ref, p_ref, n_ref):
    scr[0:TM, :] = t_ref[...]
    scr[TM:TM + HB, :] = p_ref[...]
    scr[TM + HB:TM + HALO, :] = n_ref[...]


def _per_part(n_parts, fn, refs):
    if n_parts == 1:
        fn(*refs)
        return
    k = len(refs) // 2
    pl.when(_is_ctx())(lambda: fn(*refs[:k]))
    pl.when(jnp.logical_not(_is_ctx()))(lambda: fn(*refs[k:]))


def _ada_kernel(c_ref, w_ref, b_ref, o_ref):
    s = _silu(c_ref[...]).astype(BF16)
    o_ref[...] = _dot(s, w_ref[...].astype(BF16)) + b_ref[...]


def _ada(cond8, w_ada, b_ada):
    out = pl.pallas_call(
        _ada_kernel,
        grid=(DEPTH, 6),
        in_specs=[
            pl.BlockSpec((SUB, D), lambda l, k: (0, 0)),
            pl.BlockSpec((None, D, D), lambda l, k: (l, 0, k)),
            pl.BlockSpec((None, 1, D), lambda l, k: (l, 0, k)),
        ],
        out_specs=pl.BlockSpec((None, SUB, D), lambda l, k: (l * 6 + k, 0, 0)),
        out_shape=jax.ShapeDtypeStruct((DEPTH * 6, SUB, D), F32),
        compiler_params=_cp(("arbitrary", "arbitrary")),
        name="ada",
    )(cond8, w_ada, b_ada.reshape(DEPTH, 1, 6 * D))
    return out.reshape(DEPTH * 6 * SUB, 1, D)


def _mod_spec(layer, which, tile_of=_joint_tile):
    base = (layer * 6 + which) * SUB
    per_b = LS // TM

    def imap(i):
        t = tile_of(i)
        r = jnp.where(t < N_CTX_TILES, 0, 1 + (t - N_CTX_TILES) // per_b)
        return (base + r, 0, 0)

    return pl.BlockSpec((None, 1, D), imap)


def _hmod(x, nw_ref, sh_ref, sc_ref):
    return (_rms(x, nw_ref[...]) * (1.0 + sc_ref[...]) + sh_ref[...]).astype(BF16)


def _fill_h(h_scr, nw_ref, sh_ref, sc_ref, x_ref, xp_ref, xn_ref):
    h_scr[0:TM, :] = _hmod(x_ref[...], nw_ref, sh_ref, sc_ref)
    h_scr[TM:TM + HB, :] = _hmod(xp_ref[...], nw_ref, sh_ref, sc_ref)
    h_scr[TM + HB:TM + HALO, :] = _hmod(xn_ref[...], nw_ref, sh_ref, sc_ref)


def _tile_flags():
    i = pl.program_id(0)
    is_ctx = _is_ctx()
    lseq = jnp.where(is_ctx, LP, LS)
    starts = ((i * TM) & (lseq - 1)) == 0
    ends = (((i + 1) * TM) & (lseq - 1)) == 0
    return is_ctx, starts, ends


def _edge_slabs():
    return sorted({b for b in range(0, TM, LP)} | {b + LP - SLAB for b in range(0, TM, LP)})


def _conv3_bulk(u, w, b):
    return (pltpu.roll(u, 1, axis=0) * w[0:1, :] + u * w[1:2, :]
            + pltpu.roll(u, TM - 1, axis=0) * w[2:3, :] + b)


def _conv3_slab(u_ext, s, w, b, flags):
    is_ctx, starts, ends = flags
    us = u_ext[s:s + SLAB, :]
    if s == 0:
        prev = jnp.where(starts, 0.0, u_ext[TM + HB - 1:TM + HB, :])
    else:
        prev = u_ext[s - 1:s, :]
        if s % LP == 0:
            prev = jnp.where(is_ctx, 0.0, prev)
    if s + SLAB == TM:
        nxt = jnp.where(ends, 0.0, u_ext[TM + HB:TM + HB + 1, :])
    else:
        nxt = u_ext[s + SLAB:s + SLAB + 1, :]
        if (s + SLAB) % LP == 0:
            nxt = jnp.where(is_ctx, 0.0, nxt)
    rows = lax.broadcasted_iota(jnp.int32, us.shape, 0)
    up = jnp.where(rows == 0, prev, pltpu.roll(us, 1, axis=0))
    dn = jnp.where(rows == SLAB - 1, nxt, pltpu.roll(us, SLAB - 1, axis=0))
    return up * w[0:1, :] + us * w[1:2, :] + dn * w[2:3, :] + b


def _tail_kernel(n_a, n_x, has_bias, split_out, *refs):
    a_refs, x_refs = refs[:3 * n_a], refs[3 * n_a:3 * (n_a + n_x)]
    rest = list(refs[3 * (n_a + n_x):])
    w_ref = rest.pop(0)
    b_ref = rest.pop(0) if has_bias else None
    (nw1_ref, g1_ref, nw2_ref, sh_ref, sc_ref, wup_ref, wdw_ref, bdw_ref, wdn_ref, nw3_ref,
     g2_ref) = rest[:11]
    n_out = 2 if split_out else 1
    out_refs = rest[11:11 + n_out]
    a_scr, x_scr, h_scr, act_scr = rest[11 + n_out:]

    _per_part(n_a, functools.partial(_fill_ext, a_scr), a_refs)
    _per_part(n_x, functools.partial(_fill_ext, x_scr), x_refs)
    y = _dot(a_scr[...], w_ref[...])
    if has_bias:
        y = y + b_ref[...]
    x1 = x_scr[...] + g1_ref[...] * _rms(y, nw1_ref[...])
    x_scr[...] = x1
    h_scr[...] = _hmod(x1, nw2_ref, sh_ref, sc_ref)

    flags = _tile_flags()
    h = h_scr[...]
    for c in range(D_FF // CK):
        cg = slice(c * CK, (c + 1) * CK)
        cv = slice(D_FF + c * CK, D_FF + (c + 1) * CK)
        g_ext = _dot(h, wup_ref[:, cg])
        v_ext = _dot(h, wup_ref[:, cv])
        wg, bg, wv, bv = wdw_ref[:, cg], bdw_ref[:, cg], wdw_ref[:, cv], bdw_ref[:, cv]
        g = _conv3_bulk(g_ext[0:TM, :], wg, bg)
        val = _conv3_bulk(v_ext[0:TM, :], wv, bv)
        act_scr[:, cg] = (_silu(g) * val).astype(BF16)
        for s in _edge_slabs():
            g = _conv3_slab(g_ext, s, wg, bg, flags)
            val = _conv3_slab(v_ext, s, wv, bv, flags)
            act_scr[s:s + SLAB, cg] = (_silu(g) * val).astype(BF16)
    y = _dot(act_scr[...], wdn_ref[...])
    res = x_scr[0:TM, :] + g2_ref[...] * _rms(y, nw3_ref[...])
    if split_out:
        is_ctx = flags[0]

        @pl.when(is_ctx)
        def _():
            out_refs[0][...] = res

        @pl.when(jnp.logical_not(is_ctx))
        def _():
            out_refs[1][...] = res
    else:
        out_refs[0][...] = res


def _tail(a_parts, x_parts, nw_all, mods, layer, j, w_out_bf, b_out, wup_bf, w_dw, b_dw, wdn_bf,
          split_out):
    k = w_out_bf.shape[0]
    has_bias = b_out is not None

    def specs(parts, width):
        if len(parts) == 1:
            return _ext_specs(M, _joint_tile, width)
        return _ext_specs(MP, _ctx_tile, width) + _ext_specs(MS, _lat_tile, width)

    in_specs = specs(a_parts, k) + specs(x_parts, D) + [_lay((k, D), None)]
    args = [p for p in a_parts for _ in range(3)] + [p for p in x_parts for _ in range(3)]
    args.append(w_out_bf)
    if has_bias:
        in_specs.append(_lay((1, D), j))
        args.append(b_out)
    in_specs += [
        _norm_spec(layer, 1), _mod_spec(layer, 2),
        _norm_spec(layer, 2), _mod_spec(layer, 3), _mod_spec(layer, 4),
        _lay((D, 2 * D_FF), None), _lay((3, 2 * D_FF), layer), _lay((1, 2 * D_FF), layer),
        _lay((D_FF, D), None),
        _norm_spec(layer, 3), _mod_spec(layer, 5),
    ]
    args += [nw_all, mods, nw_all, mods, mods, wup_bf, w_dw, b_dw, wdn_bf, nw_all, mods]
    if split_out:
        out_specs = [_tile_spec(D, _ctx_tile), _tile_spec(D, _lat_tile)]
        out_shape = [jax.ShapeDtypeStruct((MP, D), F32), jax.ShapeDtypeStruct((MS, D), F32)]
    else:
        out_specs = _tile_spec(D, _joint_tile)
        out_shape = jax.ShapeDtypeStruct((M, D), F32)
    return pl.pallas_call(
        functools.partial(_tail_kernel, len(a_parts), len(x_parts), has_bias, split_out),
        grid=(M // TM,),
        in_specs=in_specs,
        out_specs=out_specs,
        out_shape=out_shape,
        scratch_shapes=[pltpu.VMEM((TM + HALO, k), BF16), pltpu.VMEM((TM + HALO, D), F32),
                        pltpu.VMEM((TM + HALO, D), BF16), pltpu.VMEM((TM, D_FF), BF16)],
        compiler_params=_cp(("arbitrary",)),
        name="tail",
    )(*args)


def _hy_in_kernel(n_parts, n_cast, *refs):
    x_refs = refs[:3 * n_parts]
    nw_ref, sh_ref, sc_ref, w_ref, b_ref, ws_ref, bs_ref = refs[3 * n_parts:3 * n_parts + 7]
    rest = refs[3 * n_parts + 7:]
    x0_ref, z_ref, h_scr = rest[n_cast], rest[n_cast + 1], rest[-1]
    _cast_blocks(rest[:n_cast], rest[n_cast + 2:-1])
    _per_part(n_parts, functools.partial(_fill_h, h_scr, nw_ref, sh_ref, sc_ref), x_refs)
    flags = _tile_flags()
    h = h_scr[...]
    for c in range(D // CK):
        cc = slice(c * CK, (c + 1) * CK)
        u_ext, ws, bs = [], [], []
        for s in range(3):
            cs = slice(s * D + c * CK, s * D + (c + 1) * CK)
            u_ext.append(_dot(h, w_ref[:, cs]) + b_ref[:, cs])
            ws.append(ws_ref[:, cs])
            bs.append(bs_ref[:, cs])
        out = [_conv3_bulk(u_ext[s][0:TM, :], ws[s], bs[s]) for s in range(3)]
        x0_ref[:, cc] = out[0].astype(BF16)
        z_ref[:, cc] = (out[2] * out[1]).astype(BF16)
        for r in _edge_slabs():
            out = [_conv3_slab(u_ext[s], r, ws[s], bs[s], flags) for s in range(3)]
            x0_ref[r:r + SLAB, cc] = out[0].astype(BF16)
            z_ref[r:r + SLAB, cc] = (out[2] * out[1]).astype(BF16)


def _hy_in(x_parts, nw_all, mods, layer, j, w_bf, b_in, w_short, b_short, cast_jobs):
    assert M // TM == CAST_STEPS
    c_in, c_out, c_shape, c_args = _cast_plumbing(cast_jobs, lambda i: i)
    if len(x_parts) == 1:
        x_specs = _ext_specs(M, _joint_tile)
    else:
        x_specs = _ext_specs(MP, _ctx_tile) + _ext_specs(MS, _lat_tile)
    x_args = [a for a in x_parts for _ in range(3)]
    return pl.pallas_call(
        functools.partial(_hy_in_kernel, len(x_parts), len(c_in)),
        grid=(M // TM,),
        in_specs=x_specs + [
            _norm_spec(layer, 0), _mod_spec(layer, 0), _mod_spec(layer, 1),
            _lay((D, 3 * D), None), _lay((1, 3 * D), j), _lay((3, 3 * D), j), _lay((1, 3 * D), j),
        ] + c_in,
        out_specs=[_tile_spec(D, _joint_tile)] * 2 + c_out,
        out_shape=[jax.ShapeDtypeStruct((M, D), BF16)] * 2 + c_shape,
        scratch_shapes=[pltpu.VMEM((TM + HALO, D), BF16)],
        compiler_params=_cp(("arbitrary",)),
        name="hy_in",
    )(*x_args, nw_all, mods, mods, w_bf, b_in, w_short, b_short, *c_args)


def _dft_mats():
    n = np.arange(CB, dtype=np.float64)
    f = np.arange(CB, dtype=np.float64)
    ang = 2.0 * np.pi * np.outer(f, n) / (2 * CB)
    fwd = np.concatenate([np.cos(ang), -np.sin(ang)], axis=0)
    fwd[CB] = np.cos(np.pi * n)
    scale = np.full((2 * CB, 1), 2.0 / (2 * CB))
    scale[0] = scale[CB] = 1.0 / (2 * CB)
    inv = (fwd * scale).T
    sgn = np.where(np.arange(CB) % 2 == 0, 1.0, -1.0)
    sgn2 = np.concatenate([sgn, sgn])[:, None]
    sgn2[CB] = 1.0
    return (jnp.asarray(fwd, F32).astype(BF16), jnp.asarray(inv, F32).astype(BF16),
            jnp.asarray(sgn2, F32))


def _filter_features(L):
    t = np.linspace(0.0, 1.0, L)[:, None]
    w = 2.0 * np.pi * np.arange(L)[:, None] / L
    bands = np.linspace(1e-4, EMB_BANDS - 1, EMB_BANDS)
    z = np.concatenate([t, np.cos(bands * w), -np.sin(bands * w)], axis=-1)
    return t, z


def _filter_tables():
    zs, ts, ms = [], [], []
    for L in (LP, LS):
        t, z = _filter_features(L)
        z = np.pad(z, ((0, 0), (0, FILTER_HIDDEN - EMB_DIM)))
        idx = np.abs(np.arange(2 * L) - L) % L
        zs.append(z[idx])
        ts.append(t[idx])
        ms.append((np.arange(2 * L) != 0).astype(np.float64)[:, None])
    return tuple(jnp.asarray(np.concatenate(a), F32) for a in (zs, ts, ms))


N_WIN = 1 + 2 * NB_S - 1


def _filt_kernel(z_ref, t_ref, m_ref, w1_ref, b1_ref, fr_ref, w2_ref, b2_ref, w3_ref,
                 ad_ref, f_ref, sg_ref, g_ref, nyq_ref, prev_scr):
    c = pl.program_id(0)

    @pl.when(c == 0)
    def _():
        prev_scr[...] = jnp.zeros_like(prev_scr)

    fr = fr_ref[...]
    hid = jnp.sin(fr * (_dot(z_ref[...].astype(BF16), w1_ref[...].astype(BF16)) + b1_ref[...]))
    hid = jnp.sin(fr * (_dot(hid.astype(BF16), w2_ref[...].astype(BF16)) + b2_ref[...]))
    h = _dot(hid.astype(BF16), w3_ref[...].astype(BF16))
    taps = h * jnp.exp(-t_ref[...] * ad_ref[...]) * m_ref[...]
    a = _dot(f_ref[...], taps.astype(BF16))
    g = a + sg_ref[...] * prev_scr[...]
    prev_scr[...] = a
    g_ref[...] = g
    nyq_ref[...] = g[CB:CB + 1, :]
    g_ref[CB:CB + 1, :] = jnp.zeros((1, D), F32)


def _is_bwd_chunk(c):
    return jnp.logical_or(c == 0, jnp.logical_and(c >= 2, c < 2 + NB_S))


def _window_of_chunk(c):
    return jnp.where(c <= 1, 0, jnp.maximum(c - 2, 1))


def _filter_spectra(tabs, fwd, sgn, j, w1, b1, freq, w2, b2, w3, absd):
    z_tab, t_tab, m_tab = tabs
    small = lambda shape: pl.BlockSpec(shape, lambda c: (0, 0))
    hid = _lay((1, FILTER_HIDDEN), j)
    return pl.pallas_call(
        _filt_kernel,
        grid=(N_CHUNK,),
        in_specs=[
            pl.BlockSpec((CB, FILTER_HIDDEN), lambda c: (c, 0)),
            pl.BlockSpec((CB, 1), lambda c: (c, 0)),
            pl.BlockSpec((CB, 1), lambda c: (c, 0)),
            _lay((FILTER_HIDDEN, FILTER_HIDDEN), j), hid, hid,
            _lay((FILTER_HIDDEN, FILTER_HIDDEN), j), hid,
            pl.BlockSpec((None, FILTER_HIDDEN, D),
                         lambda c: (j, 0, jnp.where(_is_bwd_chunk(c), 1, 0))),
            small((1, D)),
            small((2 * CB, CB)),
            small((2 * CB, 1)),
        ],
        out_specs=[pl.BlockSpec((None, 2 * CB, D), lambda c: (_window_of_chunk(c), 0, 0)),
                   pl.BlockSpec((None, 1, D), lambda c: (_window_of_chunk(c), 0, 0))],
        out_shape=[jax.ShapeDtypeStruct((N_WIN, 2 * CB, D), F32),
                   jax.ShapeDtypeStruct((N_WIN, 1, D), F32)],
        scratch_shapes=[pltpu.VMEM((2 * CB, D), F32)],
        compiler_params=_cp(("arbitrary",)),
        name="hy_filter",
    )(z_tab, t_tab, m_tab, w1, b1, freq, w2, b2, w3, absd, fwd, sgn)


def _conv_kernel(tc, n_cast, z_ref, x0_ref, g_ref, nyq_ref, f_ref, gi_ref, db_ref, *rest):
    o_ref, u_scr = rest[n_cast], rest[-1]
    _cast_blocks(rest[:n_cast], rest[n_cast + 1:-1])
    unit = pl.program_id(1)
    fmat = f_ref[...]
    gmat = gi_ref[...]
    row0 = lax.broadcasted_iota(jnp.int32, (CB, tc), 0) == 0
    db = db_ref[...]

    def emit(blk, yre, yim, ynyq):
        rows = pl.ds(blk * CB, CB)
        yspec = jnp.concatenate([yre, jnp.where(row0, ynyq, yim)], axis=0)
        y = _dot(gmat, yspec.astype(BF16))
        o_ref[rows, :] = (x0_ref[rows, :] * (y + z_ref[rows, :] * db)).astype(BF16)

    @pl.when(unit < MP // LS)
    def _():
        gre, gim, gnyq = g_ref[0, :CB, :], g_ref[0, CB:, :], nyq_ref[0]
        for s in range(LS // LP):
            u = _dot(fmat, z_ref[pl.ds(s * CB, CB), :])
            ure, unyq = u[:CB], u[CB:CB + 1]
            uim = jnp.where(row0, 0.0, u[CB:])
            emit(s, gre * ure - gim * uim, gre * uim + gim * ure, gnyq * unyq)

    @pl.when(unit >= MP // LS)
    def _():
        unyq = []
        for j in range(NB_S):
            u = _dot(fmat, z_ref[pl.ds(j * CB, CB), :])
            unyq.append(u[CB:CB + 1])
            u_scr[j] = u
            u_scr[j, CB:CB + 1, :] = jnp.zeros((1, tc), F32)
        for i in range(NB_S):
            win = [NB_S + i - j for j in range(NB_S)]
            parts_re, parts_im = [], []
            for r in range(0, CB, MAC_ROWS):
                re, im = slice(r, r + MAC_ROWS), slice(CB + r, CB + r + MAC_ROWS)
                yre = yim = None
                for j, w in enumerate(win):
                    gre, gim = g_ref[w, re, :], g_ref[w, im, :]
                    ure, uim = u_scr[j, re, :], u_scr[j, im, :]
                    pre, pim = gre * ure - gim * uim, gre * uim + gim * ure
                    yre = pre if yre is None else yre + pre
                    yim = pim if yim is None else yim + pim
                parts_re.append(yre)
                parts_im.append(yim)
            ynyq = nyq_ref[win[0]] * unyq[0]
            for j in range(1, NB_S):
                ynyq = ynyq + nyq_ref[win[j]] * unyq[j]
            emit(i, jnp.concatenate(parts_re, axis=0), jnp.concatenate(parts_im, axis=0), ynyq)


def _long_conv(z, x0, spectra, fwd, inv, d_bias, j, cast_jobs, tc=256):
    g_win, g_nyq = spectra
    units = M // LS
    assert (D // tc) * units == CAST_STEPS
    c_in, c_out, c_shape, c_args = _cast_plumbing(cast_jobs, lambda c, u: c * units + u)
    return pl.pallas_call(
        functools.partial(_conv_kernel, tc, len(c_in)),
        grid=(D // tc, units),
        in_specs=[
            pl.BlockSpec((LS, tc), lambda c, u: (u, c)),
            pl.BlockSpec((LS, tc), lambda c, u: (u, c)),
            pl.BlockSpec((N_WIN, 2 * CB, tc), lambda c, u: (0, 0, c)),
            pl.BlockSpec((N_WIN, 1, tc), lambda c, u: (0, 0, c)),
            pl.BlockSpec((2 * CB, CB), lambda c, u: (0, 0)),
            pl.BlockSpec((CB, 2 * CB), lambda c, u: (0, 0)),
            pl.BlockSpec((None, 1, tc), lambda c, u: (j, 0, c)),
        ] + c_in,
        out_specs=[pl.BlockSpec((LS, tc), lambda c, u: (u, c))] + c_out,
        out_shape=[jax.ShapeDtypeStruct((M, D), BF16)] + c_shape,
        scratch_shapes=[pltpu.VMEM((NB_S, 2 * CB, tc), F32)],
        compiler_params=_cp(("arbitrary", "arbitrary")),
        name="hy_conv",
    )(z, x0, g_win, g_nyq, fwd, inv, d_bias, *c_args)


def _rope_tables(L):
    rows = L // GRID_W
    r = np.repeat(np.arange(rows, dtype=np.float64), GRID_W)
    cidx = np.tile(np.arange(GRID_W, dtype=np.float64), rows)
    inv = ROPE_THETA ** (-np.arange(ROT_FREQS, dtype=np.float64) / ROT_FREQS)
    ar = r[:, None] * inv
    ac = cidx[:, None] * inv
    cos = np.concatenate([np.cos(ar), np.cos(ar), np.cos(ac), np.cos(ac)] * 2, axis=-1)
    sin = np.concatenate([np.sin(ar), np.sin(ar), np.sin(ac), np.sin(ac)] * 2, axis=-1)
    first_half = (np.arange(V_DIM) % (2 * ROT_FREQS)) < ROT_FREQS
    sin_a = np.where(first_half, -sin, 0.0)
    sin_b = np.where(first_half, 0.0, sin)
    return tuple(jnp.asarray(a, F32) for a in (cos, sin_a, sin_b))


def _rope(x, cos, sin_a, sin_b):
    return (x * cos + pltpu.roll(x, V_DIM - ROT_FREQS, axis=1) * sin_a
            + pltpu.roll(x, ROT_FREQS, axis=1) * sin_b)


HEADS_PER_CHUNK = CK // V_DIM


def _qkv_chunks():
    per_part = D // CK
    return [(slice(c * CK, (c + 1) * CK), c // per_part, (c % per_part) * HEADS_PER_CHUNK)
            for c in range(3 * per_part)]


def _qkv_c_kernel(last, x_ref, nw_ref, sh_ref, sc_ref, w_ref, *rest):
    if last:
        pk_ref, pv_ref, qkv_ref, nk_ref, nv_ref = rest
        nk_ref[:, 0] = pk_ref[...]
        nv_ref[:, 0] = pv_ref[...]
    else:
        qkv_ref, nk_ref, nv_ref = rest
    h = _hmod(x_ref[...], nw_ref, sh_ref, sc_ref)
    for cs, part, head0 in _qkv_chunks():
        u = _dot(h, w_ref[:, cs])
        qkv_ref[:, cs] = (u * QSCALE if part == 0 else u).astype(BF16)
        if part > 0:
            cache = nk_ref if part == 1 else nv_ref
            for s in range(TM // LP):
                for hh in range(HEADS_PER_CHUNK):
                    blk = u[s * LP:(s + 1) * LP, hh * V_DIM:(hh + 1) * V_DIM]
                    if last:
                        cache[s, N_ATTN - 1, head0 + hh] = blk
                    else:
                        cache[s, head0 + hh] = blk


def _qkv_ctx(x, nw_all, mods, layer, j, w_bf, prev):
    seqs = TM // LP
    one = pl.BlockSpec((seqs, N_HEADS, LP, V_DIM), lambda i: (i, 0, 0, 0))
    one_shape = jax.ShapeDtypeStruct((BP, N_HEADS, LP, V_DIM), F32)
    in_specs = [_tile_spec(D, _joint_tile), _norm_spec(layer, 0), _mod_spec(layer, 0),
                _mod_spec(layer, 1), _lay((D, 3 * D), None)]
    args = [x, nw_all, mods, mods, w_bf]
    if prev is None:
        cache, cache_shape = one, one_shape
    else:
        assert j == N_ATTN - 1 == 1
        in_specs += [one, one]
        args += list(prev)
        cache = pl.BlockSpec((seqs, N_ATTN, N_HEADS, LP, V_DIM), lambda i: (i, 0, 0, 0, 0))
        cache_shape = jax.ShapeDtypeStruct((BP, N_ATTN, N_HEADS, LP, V_DIM), F32)
    return pl.pallas_call(
        functools.partial(_qkv_c_kernel, prev is not None),
        grid=(N_CTX_TILES,),
        in_specs=in_specs,
        out_specs=[_tile_spec(3 * D, _joint_tile), cache, cache],
        out_shape=[jax.ShapeDtypeStruct((MP, 3 * D), BF16), cache_shape, cache_shape],
        compiler_params=_cp(("arbitrary",)),
        name="qkv_ctx",
    )(*args)


def _qkv_l_kernel(x_ref, nw_ref, sh_ref, sc_ref, w_ref, cos_ref, sa_ref, sb_ref, qkv_ref):
    h = _hmod(x_ref[...], nw_ref, sh_ref, sc_ref)
    for cs, part, _ in _qkv_chunks():
        u = _dot(h, w_ref[:, cs])
        if part == 2:
            qkv_ref[:, cs] = u.astype(BF16)
            continue
        for hh in range(HEADS_PER_CHUNK):
            r = _rope(u[:, hh * V_DIM:(hh + 1) * V_DIM], cos_ref[...], sa_ref[...], sb_ref[...])
            if part == 0:
                r = r * QSCALE
            qkv_ref[:, cs.start + hh * V_DIM:cs.start + (hh + 1) * V_DIM] = r.astype(BF16)


def _qkv_lat(x, nw_all, mods, layer, j, w_bf, ropes):
    tile = lambda i: i + N_CTX_TILES
    tab = pl.BlockSpec((TM, V_DIM), lambda i: (i % (LS // TM), 0))
    return pl.pallas_call(
        _qkv_l_kernel,
        grid=(N_LAT_TILES,),
        in_specs=[_tile_spec(D, tile), _norm_spec(layer, 0), _mod_spec(layer, 0, tile),
                  _mod_spec(layer, 1, tile), _lay((D, 3 * D), None), tab, tab, tab],
        out_specs=_tile_spec(3 * D, _joint_tile),
        out_shape=jax.ShapeDtypeStruct((MS, 3 * D), BF16),
        compiler_params=_cp(("arbitrary",)),
        name="qkv_lat",
    )(x, nw_all, mods, mods, w_bf, *ropes)


def _lambda(lv, lam_init):
    a = jnp.exp(jnp.sum(lv[0:1, :] * lv[1:2, :], axis=-1, keepdims=True))
    b = jnp.exp(jnp.sum(lv[2:3, :] * lv[3:4, :], axis=-1, keepdims=True))
    return a - b + lam_init


def _diff_attn(q, chunks, lam, lam_init, subln):
    t = q.shape[0]
    lane = lax.broadcasted_iota(jnp.int32, q.shape, 1)
    zero = jnp.zeros_like(q)
    q2 = jnp.concatenate([jnp.where(lane < HEAD_DIM, q, zero),
                          jnp.where(lane < HEAD_DIM, zero, q)], axis=0)
    m = l = acc = None
    for k, v in chunks:
        s = lax.dot_general(q2, k, (((1,), (1,)), ((), ())), preferred_element_type=F32)
        mc = jnp.max(s, axis=-1, keepdims=True)
        m_new = mc if m is None else jnp.maximum(m, mc)
        e = jnp.exp2(s - m_new)
        lc = jnp.sum(e, axis=-1, keepdims=True)
        pv = _dot(e.astype(BF16), v)
        if m is None:
            l, acc = lc, pv
        else:
            alpha = jnp.exp2(m - m_new)
            l = alpha * l + lc
            acc = alpha * acc + pv
        m = m_new
    o = acc[:t] * (1.0 / l[:t]) - acc[t:] * (lam / l[t:])
    return _rms(o, subln) * (1.0 - lam_init)


def _attn_c_kernel(lam_init, q_ref, k_ref, v_ref, lv_ref, sub_ref, o_ref):
    lam = _lambda(lv_ref[...], lam_init)
    t = LP
    probs = [(slice(s * LP, (s + 1) * LP), slice(h * V_DIM, (h + 1) * V_DIM))
             for s in range(CTX_SEQS) for h in range(N_HEADS)]
    lane = lax.broadcasted_iota(jnp.int32, (t, V_DIM), 1)
    zero = jnp.zeros((t, V_DIM), BF16)
    dn = (((1,), (1,)), ((), ()))
    s = []
    for rs, hs in probs:
        q = q_ref[rs, hs]
        q2 = jnp.concatenate([jnp.where(lane < HEAD_DIM, q, zero),
                              jnp.where(lane < HEAD_DIM, zero, q)], axis=0)
        s.append(lax.dot_general(q2, k_ref[rs, hs], dn, preferred_element_type=F32))
    e = [jnp.exp2(x - jnp.max(x, axis=-1, keepdims=True)) for x in s]
    l = [jnp.sum(x, axis=-1, keepdims=True) for x in e]
    pv = [_dot(x.astype(BF16), v_ref[rs, hs]) for x, (rs, hs) in zip(e, probs)]
    for (rs, hs), acc, lh in zip(probs, pv, l):
        o = acc[:t] * (1.0 / lh[:t]) - acc[t:] * (lam / lh[t:])
        o_ref[rs, hs] = (_rms(o, sub_ref[...]) * (1.0 - lam_init)).astype(BF16)


def _attn_ctx(qkv_c, lamv, subln, j, lam_init):
    part = lambda p: pl.BlockSpec((CTX_SEQS * LP, D), lambda b: (b, p))
    return pl.pallas_call(
        functools.partial(_attn_c_kernel, lam_init),
        grid=(BP // CTX_SEQS,),
        in_specs=[part(0), part(1), part(2),
                  pl.BlockSpec((None, 4, HEAD_DIM), lambda b: (j, 0, 0)),
                  pl.BlockSpec((None, 1, V_DIM), lambda b: (j, 0, 0))],
        out_specs=pl.BlockSpec((CTX_SEQS * LP, D), lambda b: (b, 0)),
        out_shape=jax.ShapeDtypeStruct((MP, D), BF16),
        compiler_params=_cp(("arbitrary",)),
        name="attn_ctx",
    )(qkv_c, qkv_c, qkv_c, lamv, subln)


def _attn_l_kernel(lam_init, n_cast, q_ref, k_ref, v_ref, ck_ref, cv_ref, lv_ref, sub_ref, *rest):
    o_ref = rest[n_cast]
    _cast_blocks(rest[:n_cast], rest[n_cast + 1:])
    lam = _lambda(lv_ref[...], lam_init)
    for h in range(ATT_HEADS):
        hs = slice(h * V_DIM, (h + 1) * V_DIM)
        chunks = [(ck_ref[h].astype(BF16), cv_ref[h].astype(BF16))]
        for c in range(LS // KEY_CHUNK):
            rows = pl.ds(c * KEY_CHUNK, KEY_CHUNK)
            chunks.append((k_ref[rows, hs], v_ref[rows, hs]))
        o = _diff_attn(q_ref[:, hs], chunks, lam, lam_init, sub_ref[...])
        o_ref[:, hs] = o.astype(BF16)


def _attn_lat(qkv_l, cache_k, cache_v, lamv, subln, j, lam_init, cast_jobs, tq=512):
    nq = LS // tq
    width = ATT_HEADS * V_DIM
    groups = N_HEADS // ATT_HEADS
    assert BS * groups * nq == CAST_STEPS
    c_in, c_out, c_shape, c_args = _cast_plumbing(
        cast_jobs, lambda b, g, q: (b * groups + g) * nq + q)
    seq = lambda part: pl.BlockSpec((LS, width), lambda b, g, q: (b, part * groups + g))
    ctx = pl.BlockSpec((None, None, ATT_HEADS, PAST, V_DIM), lambda b, g, q: (b, j, g, 0, 0))
    return pl.pallas_call(
        functools.partial(_attn_l_kernel, lam_init, len(c_in)),
        grid=(BS, groups, nq),
        in_specs=[pl.BlockSpec((tq, width), lambda b, g, q: (b * nq + q, g)),
                  seq(1), seq(2), ctx, ctx,
                  pl.BlockSpec((None, 4, HEAD_DIM), lambda b, g, q: (j, 0, 0)),
                  pl.BlockSpec((None, 1, V_DIM), lambda b, g, q: (j, 0, 0))] + c_in,
        out_specs=[pl.BlockSpec((tq, width), lambda b, g, q: (b * nq + q, g))] + c_out,
        out_shape=[jax.ShapeDtypeStruct((MS, D), BF16)] + c_shape,
        compiler_params=_cp(("arbitrary", "arbitrary", "arbitrary")),
        name="attn_lat",
    )(qkv_l, qkv_l, qkv_l, cache_k, cache_v, lamv, subln, *c_args)


def kernel(x_prompt, x_sample, cache_k, cache_v, c, c_ctx, w_ada, b_ada, norm_w, hy_w_in, hy_b_in, hy_w_short, hy_b_short, hy_f_w1, hy_f_b1, hy_f_freq, hy_f_w2, hy_f_b2, hy_f_w3, hy_d_bias, hy_w_out, hy_b_out, at_w_qkv, at_w_out, at_lambda_q1, at_lambda_k1, at_lambda_q2, at_lambda_k2, at_subln, ffn_w_up, ffn_w_dw, ffn_b_dw, ffn_w_down):
    cond8 = jnp.concatenate([c_ctx[None, :], c, jnp.zeros((SUB - 1 - BS, D), F32)], axis=0)
    mods = _ada(cond8, w_ada, b_ada)

    fwd, inv, sgn = _dft_mats()
    tabs = _filter_tables()
    min_decay = math.log(DECAY_TARGET) / DECAY_PCT_LONG
    max_decay = math.log(DECAY_TARGET) / DECAY_PCT_SHORT
    absd = jnp.asarray(np.abs(np.linspace(min_decay, max_decay, D))[None, :], F32)
    ropes = _rope_tables(LS)

    row = lambda a: a.reshape(a.shape[0], 1, a.shape[1])
    nw_all = norm_w.reshape(DEPTH * 4, 1, D)
    w1_pad = jnp.pad(hy_f_w1, ((0, 0), (0, FILTER_HIDDEN - EMB_DIM), (0, 0)))
    lamv = jnp.stack([at_lambda_q1, at_lambda_k1, at_lambda_q2, at_lambda_k2], axis=1)
    subln = row(at_subln)
    ffn = lambda l: [(ffn_w_up, l), (ffn_w_down, l)]
    w_in_bf = hy_w_in[0].astype(BF16)

    x_parts = [x_prompt.reshape(MP, D), x_sample.reshape(MS, D)]
    caches = None
    for i in range(DEPTH):
        j = i // 2
        last = i == DEPTH - 1
        if i % 2 == 0:
            jobs_in = ffn(0) + [(hy_w_out, 0)] if i == 0 else []
            x0, z, *cast = _hy_in(x_parts, nw_all, mods, i, j, w_in_bf, row(hy_b_in), hy_w_short,
                                  row(hy_b_short), jobs_in)
            if i == 0:
                w_up_bf, w_down_bf, w_out_bf = cast
            spectra = _filter_spectra(tabs, fwd, sgn, j, w1_pad, row(hy_f_b1), row(hy_f_freq),
                                      hy_f_w2, row(hy_f_b2), hy_f_w3, absd)
            a, *nxt = _long_conv(z, x0, spectra, fwd, inv, row(hy_d_bias), j,
                                 ffn(i + 1) + [(at_w_qkv, j), (at_w_out, j)])
            a_parts, b_out = [a], row(hy_b_out)
        else:
            lam_init = 0.8 - 0.6 * math.exp(-0.3 * i)
            x = x_parts[0]
            qkv_c, new_k, new_v = _qkv_ctx(x, nw_all, mods, i, j, w_qkv_bf, caches)
            caches = (new_k, new_v)
            qkv_l = _qkv_lat(x, nw_all, mods, i, j, w_qkv_bf, ropes)
            jobs = [] if last else ffn(i + 1) + [(hy_w_in, j + 1), (hy_w_out, j + 1)]
            o_lat, *nxt = _attn_lat(qkv_l, cache_k, cache_v, lamv, subln, j, lam_init, jobs)
            a_parts, b_out = [_attn_ctx(qkv_c, lamv, subln, j, lam_init), o_lat], None
        y = _tail(a_parts, x_parts, nw_all, mods, i, j, w_out_bf, b_out, w_up_bf, ffn_w_dw,
                  row(ffn_b_dw), w_down_bf, split_out=last)
        x_parts = list(y) if last else [y]
        if not last:
            w_up_bf, w_down_bf, w_next_in, w_out_bf = nxt
            if i % 2 == 0:
                w_qkv_bf = w_next_in
            else:
                w_in_bf = w_next_in

    return (x_parts[0].reshape(BP, LP, D), x_parts[1].reshape(BS, LS, D), caches[0], caches[1])
```

```python
import functools
import math

import numpy as np
import jax
import jax.numpy as jnp
from jax import lax
from jax.experimental import pallas as pl
from jax.experimental.pallas import tpu as pltpu

D = 1024
BP, LP = 16, 256
BS, LS = 2, 2048
MP = BP * LP
MS = BS * LS
M = MP + MS
DEPTH = 4
N_ATTN = DEPTH // 2
GRID_W = 64
N_HEADS = 8
HEAD_DIM = 64
V_DIM = 128
ROPE_THETA = 10000.0
ROT_FREQS = 16
EMB_BANDS = 16
EMB_DIM = 33
FILTER_HIDDEN = 64
DECAY_TARGET = 1e-2
DECAY_PCT_SHORT = 0.3
DECAY_PCT_LONG = 1.5
D_FF = 2816
EPS = 1e-6
PAST = 256

CB = 256
NB_S = LS // CB
N_CHUNK = 2 + 2 * NB_S
SUB = 8
HB = 16
HALO = 2 * HB
SLAB = 16
CK = 256
TM = 512
KEY_CHUNK = 2048
ATT_HEADS = 4
CTX_SEQS = 2
MAC_ROWS = 32
VMEM_LIMIT = 56 * 1024 * 1024
QSCALE = HEAD_DIM ** -0.5 * math.log2(math.e)

F32 = jnp.float32
BF16 = jnp.bfloat16


def _dot(a, b):
    return jnp.dot(a, b, preferred_element_type=F32)


def _rms(x, w):
    ms = jnp.mean(x * x, axis=-1, keepdims=True)
    return x * lax.rsqrt(ms + EPS) * w


def _silu(x):
    return x / (1.0 + jnp.exp(-x))


def _cp(sem, vmem=VMEM_LIMIT):
    return pltpu.CompilerParams(dimension_semantics=sem, vmem_limit_bytes=vmem)


def _lay(shape, idx):
    if idx is None:
        return pl.BlockSpec(tuple(shape), lambda i: (0,) * len(shape),
                            pipeline_mode=pl.Buffered(1))
    return pl.BlockSpec((None,) + tuple(shape), lambda i: (idx,) + (0,) * len(shape),
                        pipeline_mode=pl.Buffered(1))


CAST_STEPS = 16


def _cast_plumbing(jobs, step_of):
    in_specs, out_specs, out_shape, args = [], [], [], []
    for w, layer in jobs:
        _, rows, cols = w.shape
        blk = rows // CAST_STEPS
        in_specs.append(pl.BlockSpec((None, blk, cols),
                                     lambda *g, layer=layer: (layer, step_of(*g), 0)))
        out_specs.append(pl.BlockSpec((blk, cols), lambda *g: (step_of(*g), 0)))
        out_shape.append(jax.ShapeDtypeStruct((rows, cols), BF16))
        args.append(w)
    return in_specs, out_specs, out_shape, args


def _cast_blocks(in_refs, out_refs):
    for i_ref, o_ref in zip(in_refs, out_refs):
        o_ref[...] = i_ref[...].astype(BF16)


def _norm_spec(layer, k):
    return _lay((1, D), layer * 4 + k)


N_CTX_TILES = MP // TM
N_LAT_TILES = MS // TM


def _ctx_tile(i):
    return jnp.minimum(i, N_CTX_TILES - 1)


def _lat_tile(i):
    return jnp.maximum(i - N_CTX_TILES, 0)


def _joint_tile(i):
    return i


def _is_ctx():
    return pl.program_id(0) < N_CTX_TILES


def _tile_spec(width, tile_of):
    return pl.BlockSpec((TM, width), lambda i: (tile_of(i), 0))


def _ext_specs(n_rows, tile_of, width=D):
    r = TM // HB
    last_blk = n_rows // HB - 1
    return [
        pl.BlockSpec((TM, width), lambda i: (tile_of(i), 0)),
        pl.BlockSpec((HB, width), lambda i: (jnp.maximum(tile_of(i) * r - 1, 0), 0)),
        pl.BlockSpec((HB, width), lambda i: (jnp.minimum((tile_of(i) + 1) * r, last_blk), 0)),
    ]


def _fill_ext(scr, t_ref, p_ref, n_ref):
    scr[0:TM, :] = t_ref[...]
    scr[TM:TM + HB, :] = p_ref[...]
    scr[TM + HB:TM + HALO, :] = n_ref[...]


def _per_part(n_parts, fn, refs):
    if n_parts == 1:
        fn(*refs)
        return
    k = len(refs) // 2
    pl.when(_is_ctx())(lambda: fn(*refs[:k]))
    pl.when(jnp.logical_not(_is_ctx()))(lambda: fn(*refs[k:]))


def _ada_kernel(c_ref, w_ref, b_ref, o_ref):
    s = _silu(c_ref[...]).astype(BF16)
    o_ref[...] = _dot(s, w_ref[...].astype(BF16)) + b_ref[...]


def _ada(cond8, w_ada, b_ada):
    out = pl.pallas_call(
        _ada_kernel,
        grid=(DEPTH, 6),
        in_specs=[
            pl.BlockSpec((SUB, D), lambda l, k: (0, 0)),
            pl.BlockSpec((None, D, D), lambda l, k: (l, 0, k)),
            pl.BlockSpec((None, 1, D), lambda l, k: (l, 0, k)),
        ],
        out_specs=pl.BlockSpec((None, SUB, D), lambda l, k: (l * 6 + k, 0, 0)),
        out_shape=jax.ShapeDtypeStruct((DEPTH * 6, SUB, D), F32),
        compiler_params=_cp(("arbitrary", "arbitrary")),
        name="ada",
    )(cond8, w_ada, b_ada.reshape(DEPTH, 1, 6 * D))
    return out.reshape(DEPTH * 6 * SUB, 1, D)


def _mod_spec(layer, which, tile_of=_joint_tile):
    base = (layer * 6 + which) * SUB
    per_b = LS // TM

    def imap(i):
        t = tile_of(i)
        r = jnp.where(t < N_CTX_TILES, 0, 1 + (t - N_CTX_TILES) // per_b)
        return (base + r, 0, 0)

    return pl.BlockSpec((None, 1, D), imap)


def _hmod(x, nw_ref, sh_ref, sc_ref):
    return (_rms(x, nw_ref[...]) * (1.0 + sc_ref[...]) + sh_ref[...]).astype(BF16)


def _fill_h(h_scr, nw_ref, sh_ref, sc_ref, x_ref, xp_ref, xn_ref):
    h_scr[0:TM, :] = _hmod(x_ref[...], nw_ref, sh_ref, sc_ref)
    h_scr[TM:TM + HB, :] = _hmod(xp_ref[...], nw_ref, sh_ref, sc_ref)
    h_scr[TM + HB:TM + HALO, :] = _hmod(xn_ref[...], nw_ref, sh_ref, sc_ref)


def _tile_flags():
    i = pl.program_id(0)
    is_ctx = _is_ctx()
    lseq = jnp.where(is_ctx, LP, LS)
    starts = ((i * TM) & (lseq - 1)) == 0
    ends = (((i + 1) * TM) & (lseq - 1)) == 0
    return is_ctx, starts, ends


def _edge_slabs():
    return sorted({b for b in range(0, TM, LP)} | {b + LP - SLAB for b in range(0, TM, LP)})


def _conv3_bulk(u, w, b):
    return (pltpu.roll(u, 1, axis=0) * w[0:1, :] + u * w[1:2, :]
            + pltpu.roll(u, TM - 1, axis=0) * w[2:3, :] + b)


def _conv3_slab(u_ext, s, w, b, flags):
    is_ctx, starts, ends = flags
    us = u_ext[s:s + SLAB, :]
    if s == 0:
        prev = jnp.where(starts, 0.0, u_ext[TM + HB - 1:TM + HB, :])
    else:
        prev = u_ext[s - 1:s, :]
        if s % LP == 0:
            prev = jnp.where(is_ctx, 0.0, prev)
    if s + SLAB == TM:
        nxt = jnp.where(ends, 0.0, u_ext[TM + HB:TM + HB + 1, :])
    else:
        nxt = u_ext[s + SLAB:s + SLAB + 1, :]
        if (s + SLAB) % LP == 0:
            nxt = jnp.where(is_ctx, 0.0, nxt)
    rows = lax.broadcasted_iota(jnp.int32, us.shape, 0)
    up = jnp.where(rows == 0, prev, pltpu.roll(us, 1, axis=0))
    dn = jnp.where(rows == SLAB - 1, nxt, pltpu.roll(us, SLAB - 1, axis=0))
    return up * w[0:1, :] + us * w[1:2, :] + dn * w[2:3, :] + b


def _tail_kernel(n_a, n_x, has_bias, split_out, *refs):
    a_refs, x_refs = refs[:3 * n_a], refs[3 * n_a:3 * (n_a + n_x)]
    rest = list(refs[3 * (n_a + n_x):])
    w_ref = rest.pop(0)
    b_ref = rest.pop(0) if has_bias else None
    (nw1_ref, g1_ref, nw2_ref, sh_ref, sc_ref, wup_ref, wdw_ref, bdw_ref, wdn_ref, nw3_ref,
     g2_ref) = rest[:11]
    n_out = 2 if split_out else 1
    out_refs = rest[11:11 + n_out]
    a_scr, x_scr, h_scr, act_scr = rest[11 + n_out:]

    _per_part(n_a, functools.partial(_fill_ext, a_scr), a_refs)
    _per_part(n_x, functools.partial(_fill_ext, x_scr), x_refs)
    y = _dot(a_scr[...], w_ref[...])
    if has_bias:
        y = y + b_ref[...]
    x1 = x_scr[...] + g1_ref[...] * _rms(y, nw1_ref[...])
    x_scr[...] = x1
    h_scr[...] = _hmod(x1, nw2_ref, sh_ref, sc_ref)

    flags = _tile_flags()
    h = h_scr[...]
    for c in range(D_FF // CK):
        cg = slice(c * CK, (c + 1) * CK)
        cv = slice(D_FF + c * CK, D_FF + (c + 1) * CK)
        g_ext = _dot(h, wup_ref[:, cg])
        v_ext = _dot(h, wup_ref[:, cv])
        wg, bg, wv, bv = wdw_ref[:, cg], bdw_ref[:, cg], wdw_ref[:, cv], bdw_ref[:, cv]
        g = _conv3_bulk(g_ext[0:TM, :], wg, bg)
        val = _conv3_bulk(v_ext[0:TM, :], wv, bv)
        act_scr[:, cg] = (_silu(g) * val).astype(BF16)
        for s in _edge_slabs():
            g = _conv3_slab(g_ext, s, wg, bg, flags)
            val = _conv3_slab(v_ext, s, wv, bv, flags)
            act_scr[s:s + SLAB, cg] = (_silu(g) * val).astype(BF16)
    y = _dot(act_scr[...], wdn_ref[...])
    res = x_scr[0:TM, :] + g2_ref[...] * _rms(y, nw3_ref[...])
    if split_out:
        is_ctx = flags[0]

        @pl.when(is_ctx)
        def _():
            out_refs[0][...] = res

        @pl.when(jnp.logical_not(is_ctx))
        def _():
            out_refs[1][...] = res
    else:
        out_refs[0][...] = res


def _tail(a_parts, x_parts, nw_all, mods, layer, j, w_out_bf, b_out, wup_bf, w_dw, b_dw, wdn_bf,
          split_out):
    k = w_out_bf.shape[0]
    has_bias = b_out is not None

    def specs(parts, width):
        if len(parts) == 1:
            return _ext_specs(M, _joint_tile, width)
        return _ext_specs(MP, _ctx_tile, width) + _ext_specs(MS, _lat_tile, width)

    in_specs = specs(a_parts, k) + specs(x_parts, D) + [_lay((k, D), None)]
    args = [p for p in a_parts for _ in range(3)] + [p for p in x_parts for _ in range(3)]
    args.append(w_out_bf)
    if has_bias:
        in_specs.append(_lay((1, D), j))
        args.append(b_out)
    in_specs += [
        _norm_spec(layer, 1), _mod_spec(layer, 2),
        _norm_spec(layer, 2), _mod_spec(layer, 3), _mod_spec(layer, 4),
        _lay((D, 2 * D_FF), None), _lay((3, 2 * D_FF), layer), _lay((1, 2 * D_FF), layer),
        _lay((D_FF, D), None),
        _norm_spec(layer, 3), _mod_spec(layer, 5),
    ]
    args += [nw_all, mods, nw_all, mods, mods, wup_bf, w_dw, b_dw, wdn_bf, nw_all, mods]
    if split_out:
        out_specs = [_tile_spec(D, _ctx_tile), _tile_spec(D, _lat_tile)]
        out_shape = [jax.ShapeDtypeStruct((MP, D), F32), jax.ShapeDtypeStruct((MS, D), F32)]
    else:
        out_specs = _tile_spec(D, _joint_tile)
        out_shape = jax.ShapeDtypeStruct((M, D), F32)
    return pl.pallas_call(
        functools.partial(_tail_kernel, len(a_parts), len(x_parts), has_bias, split_out),
        grid=(M // TM,),
        in_specs=in_specs,
        out_specs=out_specs,
        out_shape=out_shape,
        scratch_shapes=[pltpu.VMEM((TM + HALO, k), BF16), pltpu.VMEM((TM + HALO, D), F32),
                        pltpu.VMEM((TM + HALO, D), BF16), pltpu.VMEM((TM, D_FF), BF16)],
        compiler_params=_cp(("arbitrary",)),
        name="tail",
    )(*args)


def _hy_in_kernel(n_parts, n_cast, *refs):
    x_refs = refs[:3 * n_parts]
    nw_ref, sh_ref, sc_ref, w_ref, b_ref, ws_ref, bs_ref = refs[3 * n_parts:3 * n_parts + 7]
    rest = refs[3 * n_parts + 7:]
    x0_ref, z_ref, h_scr = rest[n_cast], rest[n_cast + 1], rest[-1]
    _cast_blocks(rest[:n_cast], rest[n_cast + 2:-1])
    _per_part(n_parts, functools.partial(_fill_h, h_scr, nw_ref, sh_ref, sc_ref), x_refs)
    flags = _tile_flags()
    h = h_scr[...]
    for c in range(D // CK):
        cc = slice(c * CK, (c + 1) * CK)
        u_ext, ws, bs = [], [], []
        for s in range(3):
            cs = slice(s * D + c * CK, s * D + (c + 1) * CK)
            u_ext.append(_dot(h, w_ref[:, cs]) + b_ref[:, cs])
            ws.append(ws_ref[:, cs])
            bs.append(bs_ref[:, cs])
        out = [_conv3_bulk(u_ext[s][0:TM, :], ws[s], bs[s]) for s in range(3)]
        x0_ref[:, cc] = out[0].astype(BF16)
        z_ref[:, cc] = (out[2] * out[1]).astype(BF16)
        for r in _edge_slabs():
            out = [_conv3_slab(u_ext[s], r, ws[s], bs[s], flags) for s in range(3)]
            x0_ref[r:r + SLAB, cc] = out[0].astype(BF16)
            z_ref[r:r + SLAB, cc] = (out[2] * out[1]).astype(BF16)


def _hy_in(x_parts, nw_all, mods, layer, j, w_bf, b_in, w_short, b_short, cast_jobs):
    assert M // TM == CAST_STEPS
    c_in, c_out, c_shape, c_args = _cast_plumbing(cast_jobs, lambda i: i)
    if len(x_parts) == 1:
        x_specs = _ext_specs(M, _joint_tile)
    else:
        x_specs = _ext_specs(MP, _ctx_tile) + _ext_specs(MS, _lat_tile)
    x_args = [a for a in x_parts for _ in range(3)]
    return pl.pallas_call(
        functools.partial(_hy_in_kernel, len(x_parts), len(c_in)),
        grid=(M // TM,),
        in_specs=x_specs + [
            _norm_spec(layer, 0), _mod_spec(layer, 0), _mod_spec(layer, 1),
            _lay((D, 3 * D), None), _lay((1, 3 * D), j), _lay((3, 3 * D), j), _lay((1, 3 * D), j),
        ] + c_in,
        out_specs=[_tile_spec(D, _joint_tile)] * 2 + c_out,
        out_shape=[jax.ShapeDtypeStruct((M, D), BF16)] * 2 + c_shape,
        scratch_shapes=[pltpu.VMEM((TM + HALO, D), BF16)],
        compiler_params=_cp(("arbitrary",)),
        name="hy_in",
    )(*x_args, nw_all, mods, mods, w_bf, b_in, w_short, b_short, *c_args)


def _dft_mats():
    n = np.arange(CB, dtype=np.float64)
    f = np.arange(CB, dtype=np.float64)
    ang = 2.0 * np.pi * np.outer(f, n) / (2 * CB)
    fwd = np.concatenate([np.cos(ang), -np.sin(ang)], axis=0)
    fwd[CB] = np.cos(np.pi * n)
    scale = np.full((2 * CB, 1), 2.0 / (2 * CB))
    scale[0] = scale[CB] = 1.0 / (2 * CB)
    inv = (fwd * scale).T
    sgn = np.where(np.arange(CB) % 2 == 0, 1.0, -1.0)
    sgn2 = np.concatenate([sgn, sgn])[:, None]
    sgn2[CB] = 1.0
    return (jnp.asarray(fwd, F32).astype(BF16), jnp.asarray(inv, F32).astype(BF16),
            jnp.asarray(sgn2, F32))


def _filter_features(L):
    t = np.linspace(0.0, 1.0, L)[:, None]
    w = 2.0 * np.pi * np.arange(L)[:, None] / L
    bands = np.linspace(1e-4, EMB_BANDS - 1, EMB_BANDS)
    z = np.concatenate([t, np.cos(bands * w), -np.sin(bands * w)], axis=-1)
    return t, z


def _filter_tables():
    zs, ts, ms = [], [], []
    for L in (LP, LS):
        t, z = _filter_features(L)
        z = np.pad(z, ((0, 0), (0, FILTER_HIDDEN - EMB_DIM)))
        idx = np.abs(np.arange(2 * L) - L) % L
        zs.append(z[idx])
        ts.append(t[idx])
        ms.append((np.arange(2 * L) != 0).astype(np.float64)[:, None])
    return tuple(jnp.asarray(np.concatenate(a), F32) for a in (zs, ts, ms))


N_WIN = 1 + 2 * NB_S - 1


def _filt_kernel(z_ref, t_ref, m_ref, w1_ref, b1_ref, fr_ref, w2_ref, b2_ref, w3_ref,
                 ad_ref, f_ref, sg_ref, g_ref, nyq_ref, prev_scr):
    c = pl.program_id(0)

    @pl.when(c == 0)
    def _():
        prev_scr[...] = jnp.zeros_like(prev_scr)

    fr = fr_ref[...]
    hid = jnp.sin(fr * (_dot(z_ref[...].astype(BF16), w1_ref[...].astype(BF16)) + b1_ref[...]))
    hid = jnp.sin(fr * (_dot(hid.astype(BF16), w2_ref[...].astype(BF16)) + b2_ref[...]))
    h = _dot(hid.astype(BF16), w3_ref[...].astype(BF16))
    taps = h * jnp.exp(-t_ref[...] * ad_ref[...]) * m_ref[...]
    a = _dot(f_ref[...], taps.astype(BF16))
    g = a + sg_ref[...] * prev_scr[...]
    prev_scr[...] = a
    g_ref[...] = g
    nyq_ref[...] = g[CB:CB + 1, :]
    g_ref[CB:CB + 1, :] = jnp.zeros((1, D), F32)


def _is_bwd_chunk(c):
    return jnp.logical_or(c == 0, jnp.logical_and(c >= 2, c < 2 + NB_S))


def _window_of_chunk(c):
    return jnp.where(c <= 1, 0, jnp.maximum(c - 2, 1))


def _filter_spectra(tabs, fwd, sgn, j, w1, b1, freq, w2, b2, w3, absd):
    z_tab, t_tab, m_tab = tabs
    small = lambda shape: pl.BlockSpec(shape, lambda c: (0, 0))
    hid = _lay((1, FILTER_HIDDEN), j)
    return pl.pallas_call(
        _filt_kernel,
        grid=(N_CHUNK,),
        in_specs=[
            pl.BlockSpec((CB, FILTER_HIDDEN), lambda c: (c, 0)),
            pl.BlockSpec((CB, 1), lambda c: (c, 0)),
            pl.BlockSpec((CB, 1), lambda c: (c, 0)),
            _lay((FILTER_HIDDEN, FILTER_HIDDEN), j), hid, hid,
            _lay((FILTER_HIDDEN, FILTER_HIDDEN), j), hid,
            pl.BlockSpec((None, FILTER_HIDDEN, D),
                         lambda c: (j, 0, jnp.where(_is_bwd_chunk(c), 1, 0))),
            small((1, D)),
            small((2 * CB, CB)),
            small((2 * CB, 1)),
        ],
        out_specs=[pl.BlockSpec((None, 2 * CB, D), lambda c: (_window_of_chunk(c), 0, 0)),
                   pl.BlockSpec((None, 1, D), lambda c: (_window_of_chunk(c), 0, 0))],
        out_shape=[jax.ShapeDtypeStruct((N_WIN, 2 * CB, D), F32),
                   jax.ShapeDtypeStruct((N_WIN, 1, D), F32)],
        scratch_shapes=[pltpu.VMEM((2 * CB, D), F32)],
        compiler_params=_cp(("arbitrary",)),
        name="hy_filter",
    )(z_tab, t_tab, m_tab, w1, b1, freq, w2, b2, w3, absd, fwd, sgn)


def _conv_kernel(tc, n_cast, z_ref, x0_ref, g_ref, nyq_ref, f_ref, gi_ref, db_ref, *rest):
    o_ref, u_scr = rest[n_cast], rest[-1]
    _cast_blocks(rest[:n_cast], rest[n_cast + 1:-1])
    unit = pl.program_id(1)
    fmat = f_ref[...]
    gmat = gi_ref[...]
    row0 = lax.broadcasted_iota(jnp.int32, (CB, tc), 0) == 0
    db = db_ref[...]

    def emit(blk, yre, yim, ynyq):
        rows = pl.ds(blk * CB, CB)
        yspec = jnp.concatenate([yre, jnp.where(row0, ynyq, yim)], axis=0)
        y = _dot(gmat, yspec.astype(BF16))
        o_ref[rows, :] = (x0_ref[rows, :] * (y + z_ref[rows, :] * db)).astype(BF16)

    @pl.when(unit < MP // LS)
    def _():
        gre, gim, gnyq = g_ref[0, :CB, :], g_ref[0, CB:, :], nyq_ref[0]
        for s in range(LS // LP):
            u = _dot(fmat, z_ref[pl.ds(s * CB, CB), :])
            ure, unyq = u[:CB], u[CB:CB + 1]
            uim = jnp.where(row0, 0.0, u[CB:])
            emit(s, gre * ure - gim * uim, gre * uim + gim * ure, gnyq * unyq)

    @pl.when(unit >= MP // LS)
    def _():
        unyq = []
        for j in range(NB_S):
            u = _dot(fmat, z_ref[pl.ds(j * CB, CB), :])
            unyq.append(u[CB:CB + 1])
            u_scr[j] = u
            u_scr[j, CB:CB + 1, :] = jnp.zeros((1, tc), F32)
        for i in range(NB_S):
            win = [NB_S + i - j for j in range(NB_S)]
            parts_re, parts_im = [], []
            for r in range(0, CB, MAC_ROWS):
                re, im = slice(r, r + MAC_ROWS), slice(CB + r, CB + r + MAC_ROWS)
                yre = yim = None
                for j, w in enumerate(win):
                    gre, gim = g_ref[w, re, :], g_ref[w, im, :]
                    ure, uim = u_scr[j, re, :], u_scr[j, im, :]
                    pre, pim = gre * ure - gim * uim, gre * uim + gim * ure
                    yre = pre if yre is None else yre + pre
                    yim = pim if yim is None else yim + pim
                parts_re.append(yre)
                parts_im.append(yim)
            ynyq = nyq_ref[win[0]] * unyq[0]
            for j in range(1, NB_S):
                ynyq = ynyq + nyq_ref[win[j]] * unyq[j]
            emit(i, jnp.concatenate(parts_re, axis=0), jnp.concatenate(parts_im, axis=0), ynyq)


def _long_conv(z, x0, spectra, fwd, inv, d_bias, j, cast_jobs, tc=256):
    g_win, g_nyq = spectra
    units = M // LS
    assert (D // tc) * units == CAST_STEPS
    c_in, c_out, c_shape, c_args = _cast_plumbing(cast_jobs, lambda c, u: c * units + u)
    return pl.pallas_call(
        functools.partial(_conv_kernel, tc, len(c_in)),
        grid=(D // tc, units),
        in_specs=[
            pl.BlockSpec((LS, tc), lambda c, u: (u, c)),
            pl.BlockSpec((LS, tc), lambda c, u: (u, c)),
            pl.BlockSpec((N_WIN, 2 * CB, tc), lambda c, u: (0, 0, c)),
            pl.BlockSpec((N_WIN, 1, tc), lambda c, u: (0, 0, c)),
            pl.BlockSpec((2 * CB, CB), lambda c, u: (0, 0)),
            pl.BlockSpec((CB, 2 * CB), lambda c, u: (0, 0)),
            pl.BlockSpec((None, 1, tc), lambda c, u: (j, 0, c)),
        ] + c_in,
        out_specs=[pl.BlockSpec((LS, tc), lambda c, u: (u, c))] + c_out,
        out_shape=[jax.ShapeDtypeStruct((M, D), BF16)] + c_shape,
        scratch_shapes=[pltpu.VMEM((NB_S, 2 * CB, tc), F32)],
        compiler_params=_cp(("arbitrary", "arbitrary")),
        name="hy_conv",
    )(z, x0, g_win, g_nyq, fwd, inv, d_bias, *c_args)


def _rope_tables(L):
    rows = L // GRID_W
    r = np.repeat(np.arange(rows, dtype=np.float64), GRID_W)
    cidx = np.tile(np.arange(GRID_W, dtype=np.float64), rows)
    inv = ROPE_THETA ** (-np.arange(ROT_FREQS, dtype=np.float64) / ROT_FREQS)
    ar = r[:, None] * inv
    ac = cidx[:, None] * inv
    cos = np.concatenate([np.cos(ar), np.cos(ar), np.cos(ac), np.cos(ac)] * 2, axis=-1)
    sin = np.concatenate([np.sin(ar), np.sin(ar), np.sin(ac), np.sin(ac)] * 2, axis=-1)
    first_half = (np.arange(V_DIM) % (2 * ROT_FREQS)) < ROT_FREQS
    sin_a = np.where(first_half, -sin, 0.0)
    sin_b = np.where(first_half, 0.0, sin)
    return tuple(jnp.asarray(a, F32) for a in (cos, sin_a, sin_b))


def _rope(x, cos, sin_a, sin_b):
    return (x * cos + pltpu.roll(x, V_DIM - ROT_FREQS, axis=1) * sin_a
            + pltpu.roll(x, ROT_FREQS, axis=1) * sin_b)


HEADS_PER_CHUNK = CK // V_DIM


def _qkv_chunks():
    per_part = D // CK
    return [(slice(c * CK, (c + 1) * CK), c // per_part, (c % per_part) * HEADS_PER_CHUNK)
            for c in range(3 * per_part)]


def _qkv_c_kernel(first, x_ref, nw_ref, sh_ref, sc_ref, w_ref, *rest):
    qkv_ref, nk_ref, nv_ref = rest[-3:]
    if first:
        for ref in (nk_ref, nv_ref):
            ref[:, 1:] = jnp.zeros((TM // LP, N_ATTN - 1, N_HEADS, LP, V_DIM), F32)
    h = _hmod(x_ref[...], nw_ref, sh_ref, sc_ref)
    for cs, part, head0 in _qkv_chunks():
        u = _dot(h, w_ref[:, cs])
        qkv_ref[:, cs] = (u * QSCALE if part == 0 else u).astype(BF16)
        if part > 0:
            cache = nk_ref if part == 1 else nv_ref
            for s in range(TM // LP):
                for hh in range(HEADS_PER_CHUNK):
                    blk = u[s * LP:(s + 1) * LP, hh * V_DIM:(hh + 1) * V_DIM]
                    if first:
                        cache[s, 0, head0 + hh] = blk
                    else:
                        cache[s, head0 + hh] = blk


def _qkv_ctx(x, nw_all, mods, layer, j, w_bf, prev):
    seqs = TM // LP
    in_specs = [_tile_spec(D, _joint_tile), _norm_spec(layer, 0), _mod_spec(layer, 0),
                _mod_spec(layer, 1), _lay((D, 3 * D), None)]
    args = [x, nw_all, mods, mods, w_bf]
    aliases = {}
    if prev is None:
        assert j == 0
        cache = pl.BlockSpec((seqs, N_ATTN, N_HEADS, LP, V_DIM), lambda i: (i, 0, 0, 0, 0))
    else:
        in_specs += [pl.BlockSpec(memory_space=pl.ANY)] * 2
        args += list(prev)
        aliases = {len(args) - 2: 1, len(args) - 1: 2}
        cache = pl.BlockSpec((seqs, None, N_HEADS, LP, V_DIM), lambda i: (i, j, 0, 0, 0))
    cache_shape = jax.ShapeDtypeStruct((BP, N_ATTN, N_HEADS, LP, V_DIM), F32)
    return pl.pallas_call(
        functools.partial(_qkv_c_kernel, prev is None),
        grid=(N_CTX_TILES,),
        in_specs=in_specs,
        out_specs=[_tile_spec(3 * D, _joint_tile), cache, cache],
        out_shape=[jax.ShapeDtypeStruct((MP, 3 * D), BF16), cache_shape, cache_shape],
        input_output_aliases=aliases,
        compiler_params=_cp(("arbitrary",)),
        name="qkv_ctx",
    )(*args)


def _qkv_l_kernel(x_ref, nw_ref, sh_ref, sc_ref, w_ref, cos_ref, sa_ref, sb_ref, qkv_ref):
    h = _hmod(x_ref[...], nw_ref, sh_ref, sc_ref)
    for cs, part, _ in _qkv_chunks():
        u = _dot(h, w_ref[:, cs])
        if part == 2:
            qkv_ref[:, cs] = u.astype(BF16)
            continue
        for hh in range(HEADS_PER_CHUNK):
            r = _rope(u[:, hh * V_DIM:(hh + 1) * V_DIM], cos_ref[...], sa_ref[...], sb_ref[...])
            if part == 0:
                r = r * QSCALE
            qkv_ref[:, cs.start + hh * V_DIM:cs.start + (hh + 1) * V_DIM] = r.astype(BF16)


def _qkv_lat(x, nw_all, mods, layer, j, w_bf, ropes):
    tile = lambda i: i + N_CTX_TILES
    tab = pl.BlockSpec((TM, V_DIM), lambda i: (i % (LS // TM), 0))
    return pl.pallas_call(
        _qkv_l_kernel,
        grid=(N_LAT_TILES,),
        in_specs=[_tile_spec(D, tile), _norm_spec(layer, 0), _mod_spec(layer, 0, tile),
                  _mod_spec(layer, 1, tile), _lay((D, 3 * D), None), tab, tab, tab],
        out_specs=_tile_spec(3 * D, _joint_tile),
        out_shape=jax.ShapeDtypeStruct((MS, 3 * D), BF16),
        compiler_params=_cp(("arbitrary",)),
        name="qkv_lat",
    )(x, nw_all, mods, mods, w_bf, *ropes)


def _lambda(lv, lam_init):
    a = jnp.exp(jnp.sum(lv[0:1, :] * lv[1:2, :], axis=-1, keepdims=True))
    b = jnp.exp(jnp.sum(lv[2:3, :] * lv[3:4, :], axis=-1, keepdims=True))
    return a - b + lam_init


def _diff_attn(q, chunks, lam, lam_init, subln):
    t = q.shape[0]
    lane = lax.broadcasted_iota(jnp.int32, q.shape, 1)
    zero = jnp.zeros_like(q)
    q2 = jnp.concatenate([jnp.where(lane < HEAD_DIM, q, zero),
                          jnp.where(lane < HEAD_DIM, zero, q)], axis=0)
    m = l = acc = None
    for k, v in chunks:
        s = lax.dot_general(q2, k, (((1,), (1,)), ((), ())), preferred_element_type=F32)
        mc = jnp.max(s, axis=-1, keepdims=True)
        m_new = mc if m is None else jnp.maximum(m, mc)
        e = jnp.exp2(s - m_new)
        lc = jnp.sum(e, axis=-1, keepdims=True)
        pv = _dot(e.astype(BF16), v)
        if m is None:
            l, acc = lc, pv
        else:
            alpha = jnp.exp2(m - m_new)
            l = alpha * l + lc
            acc = alpha * acc + pv
        m = m_new
    o = acc[:t] * (1.0 / l[:t]) - acc[t:] * (lam / l[t:])
    return _rms(o, subln) * (1.0 - lam_init)


def _attn_c_kernel(lam_init, q_ref, k_ref, v_ref, lv_ref, sub_ref, o_ref):
    lam = _lambda(lv_ref[...], lam_init)
    t = LP
    probs = [(slice(s * LP, (s + 1) * LP), slice(h * V_DIM, (h + 1) * V_DIM))
             for s in range(CTX_SEQS) for h in range(N_HEADS)]
    lane = lax.broadcasted_iota(jnp.int32, (t, V_DIM), 1)
    zero = jnp.zeros((t, V_DIM), BF16)
    dn = (((1,), (1,)), ((), ()))
    s = []
    for rs, hs in probs:
        q = q_ref[rs, hs]
        q2 = jnp.concatenate([jnp.where(lane < HEAD_DIM, q, zero),
                              jnp.where(lane < HEAD_DIM, zero, q)], axis=0)
        s.append(lax.dot_general(q2, k_ref[rs, hs], dn, preferred_element_type=F32))
    e = [jnp.exp2(x - jnp.max(x, axis=-1, keepdims=True)) for x in s]
    l = [jnp.sum(x, axis=-1, keepdims=True) for x in e]
    pv = [_dot(x.astype(BF16), v_ref[rs, hs]) for x, (rs, hs) in zip(e, probs)]
    for (rs, hs), acc, lh in zip(probs, pv, l):
        o = acc[:t] * (1.0 / lh[:t]) - acc[t:] * (lam / lh[t:])
        o_ref[rs, hs] = (_rms(o, sub_ref[...]) * (1.0 - lam_init)).astype(BF16)


def _attn_ctx(qkv_c, lamv, subln, j, lam_init):
    part = lambda p: pl.BlockSpec((CTX_SEQS * LP, D), lambda b: (b, p))
    return pl.pallas_call(
        functools.partial(_attn_c_kernel, lam_init),
        grid=(BP // CTX_SEQS,),
        in_specs=[part(0), part(1), part(2),
                  pl.BlockSpec((None, 4, HEAD_DIM), lambda b: (j, 0, 0)),
                  pl.BlockSpec((None, 1, V_DIM), lambda b: (j, 0, 0))],
        out_specs=pl.BlockSpec((CTX_SEQS * LP, D), lambda b: (b, 0)),
        out_shape=jax.ShapeDtypeStruct((MP, D), BF16),
        compiler_params=_cp(("arbitrary",)),
        name="attn_ctx",
    )(qkv_c, qkv_c, qkv_c, lamv, subln)


def _attn_l_kernel(lam_init, n_cast, q_ref, k_ref, v_ref, ck_ref, cv_ref, lv_ref, sub_ref, *rest):
    o_ref = rest[n_cast]
    _cast_blocks(rest[:n_cast], rest[n_cast + 1:])
    lam = _lambda(lv_ref[...], lam_init)
    for h in range(ATT_HEADS):
        hs = slice(h * V_DIM, (h + 1) * V_DIM)
        chunks = [(ck_ref[h].astype(BF16), cv_ref[h].astype(BF16))]
        for c in range(LS // KEY_CHUNK):
            rows = pl.ds(c * KEY_CHUNK, KEY_CHUNK)
            chunks.append((k_ref[rows, hs], v_ref[rows, hs]))
        o = _diff_attn(q_ref[:, hs], chunks, lam, lam_init, sub_ref[...])
        o_ref[:, hs] = o.astype(BF16)


def _attn_lat(qkv_l, cache_k, cache_v, lamv, subln, j, lam_init, cast_jobs, tq=512):
    nq = LS // tq
    width = ATT_HEADS * V_DIM
    groups = N_HEADS // ATT_HEADS
    assert BS * groups * nq == CAST_STEPS
    c_in, c_out, c_shape, c_args = _cast_plumbing(
        cast_jobs, lambda b, g, q: (b * groups + g) * nq + q)
    seq = lambda part: pl.BlockSpec((LS, width), lambda b, g, q: (b, part * groups + g))
    ctx = pl.BlockSpec((None, None, ATT_HEADS, PAST, V_DIM), lambda b, g, q: (b, j, g, 0, 0))
    return pl.pallas_call(
        functools.partial(_attn_l_kernel, lam_init, len(c_in)),
        grid=(BS, groups, nq),
        in_specs=[pl.BlockSpec((tq, width), lambda b, g, q: (b * nq + q, g)),
                  seq(1), seq(2), ctx, ctx,
                  pl.BlockSpec((None, 4, HEAD_DIM), lambda b, g, q: (j, 0, 0)),
                  pl.BlockSpec((None, 1, V_DIM), lambda b, g, q: (j, 0, 0))] + c_in,
        out_specs=[pl.BlockSpec((tq, width), lambda b, g, q: (b * nq + q, g))] + c_out,
        out_shape=[jax.ShapeDtypeStruct((MS, D), BF16)] + c_shape,
        compiler_params=_cp(("arbitrary", "arbitrary", "arbitrary")),
        name="attn_lat",
    )(qkv_l, qkv_l, qkv_l, cache_k, cache_v, lamv, subln, *c_args)


def kernel(x_prompt, x_sample, cache_k, cache_v, c, c_ctx, w_ada, b_ada, norm_w, hy_w_in, hy_b_in, hy_w_short, hy_b_short, hy_f_w1, hy_f_b1, hy_f_freq, hy_f_w2, hy_f_b2, hy_f_w3, hy_d_bias, hy_w_out, hy_b_out, at_w_qkv, at_w_out, at_lambda_q1, at_lambda_k1, at_lambda_q2, at_lambda_k2, at_subln, ffn_w_up, ffn_w_dw, ffn_b_dw, ffn_w_down):
    cond8 = jnp.concatenate([c_ctx[None, :], c, jnp.zeros((SUB - 1 - BS, D), F32)], axis=0)
    mods = _ada(cond8, w_ada, b_ada)

    fwd, inv, sgn = _dft_mats()
    tabs = _filter_tables()
    min_decay = math.log(DECAY_TARGET) / DECAY_PCT_LONG
    max_decay = math.log(DECAY_TARGET) / DECAY_PCT_SHORT
    absd = jnp.asarray(np.abs(np.linspace(min_decay, max_decay, D))[None, :], F32)
    ropes = _rope_tables(LS)

    row = lambda a: a.reshape(a.shape[0], 1, a.shape[1])
    nw_all = norm_w.reshape(DEPTH * 4, 1, D)
    w1_pad = jnp.pad(hy_f_w1, ((0, 0), (0, FILTER_HIDDEN - EMB_DIM), (0, 0)))
    lamv = jnp.stack([at_lambda_q1, at_lambda_k1, at_lambda_q2, at_lambda_k2], axis=1)
    subln = row(at_subln)
    ffn = lambda l: [(ffn_w_up, l), (ffn_w_down, l)]
    w_in_bf = hy_w_in[0].astype(BF16)

    x_parts = [x_prompt.reshape(MP, D), x_sample.reshape(MS, D)]
    caches = None
    for i in range(DEPTH):
        j = i // 2
        last = i == DEPTH - 1
        if i % 2 == 0:
            own = ffn(0) + [(hy_w_out, 0)] if i == 0 else []
            x0, z, *cast = _hy_in(x_parts, nw_all, mods, i, j, w_in_bf, row(hy_b_in), hy_w_short,
                                  row(hy_b_short),
                                  own + ffn(i + 1) + [(at_w_qkv, j), (at_w_out, j)])
            if i == 0:
                w_up_bf, w_down_bf, w_out_bf = cast[:3]
            nxt = cast[len(own):]
            spectra = _filter_spectra(tabs, fwd, sgn, j, w1_pad, row(hy_f_b1), row(hy_f_freq),
                                      hy_f_w2, row(hy_f_b2), hy_f_w3, absd)
            a, = _long_conv(z, x0, spectra, fwd, inv, row(hy_d_bias), j, [])
            a_parts, b_out = [a], row(hy_b_out)
        else:
            lam_init = 0.8 - 0.6 * math.exp(-0.3 * i)
            x = x_parts[0]
            qkv_c, new_k, new_v = _qkv_ctx(x, nw_all, mods, i, j, w_qkv_bf, caches)
            caches = (new_k, new_v)
            qkv_l = _qkv_lat(x, nw_all, mods, i, j, w_qkv_bf, ropes)
            jobs = [] if last else ffn(i + 1) + [(hy_w_in, j + 1), (hy_w_out, j + 1)]
            o_lat, *nxt = _attn_lat(qkv_l, cache_k, cache_v, lamv, subln, j, lam_init, jobs)
            a_parts, b_out = [_attn_ctx(qkv_c, lamv, subln, j, lam_init), o_lat], None
        y = _tail(a_parts, x_parts, nw_all, mods, i, j, w_out_bf, b_out, w_up_bf, ffn_w_dw,
                  row(ffn_b_dw), w_down_bf, split_out=last)
        x_parts = list(y) if last else [y]
        if not last:
            w_up_bf, w_down_bf, w_next_in, w_out_bf = nxt
            if i % 2 == 0:
                w_qkv_bf = w_next_in
            else:
                w_in_bf = w_next_in

    return (x_parts[0].reshape(BP, LP, D), x_parts[1].reshape(BS, LS, D), caches[0], caches[1])
```

```python
import functools
import math

import numpy as np
import jax
import jax.numpy as jnp
from jax import lax
from jax.experimental import pallas as pl
from jax.experimental.pallas import tpu as pltpu

D = 1024
BP, LP = 16, 256
BS, LS = 2, 2048
MP = BP * LP
MS = BS * LS
M = MP + MS
DEPTH = 4
N_ATTN = DEPTH // 2
GRID_W = 64
N_HEADS = 8
HEAD_DIM = 64
V_DIM = 128
ROPE_THETA = 10000.0
ROT_FREQS = 16
EMB_BANDS = 16
EMB_DIM = 33
FILTER_HIDDEN = 64
DECAY_TARGET = 1e-2
DECAY_PCT_SHORT = 0.3
DECAY_PCT_LONG = 1.5
D_FF = 2816
EPS = 1e-6
PAST = 256

CB = 256
NB_S = LS // CB
N_CHUNK = 2 + 2 * NB_S
SUB = 8
HB = 16
HALO = 2 * HB
HH = SUB
SLAB = 16
CK = 256
TM = 512
KEY_CHUNK = 2048
ATT_HEADS = 4
CTX_SEQS = 2
MAC_ROWS = 32
VMEM_LIMIT = 56 * 1024 * 1024
QSCALE = HEAD_DIM ** -0.5 * math.log2(math.e)

F32 = jnp.float32
BF16 = jnp.bfloat16


def _dot(a, b):
    return jnp.dot(a, b, preferred_element_type=F32)


def _rms(x, w):
    ms = jnp.mean(x * x, axis=-1, keepdims=True)
    return x * lax.rsqrt(ms + EPS) * w


def _silu(x):
    return x / (1.0 + jnp.exp(-x))


def _cp(sem, vmem=VMEM_LIMIT):
    return pltpu.CompilerParams(dimension_semantics=sem, vmem_limit_bytes=vmem)


def _lay(shape, idx):
    if idx is None:
        return pl.BlockSpec(tuple(shape), lambda i: (0,) * len(shape),
                            pipeline_mode=pl.Buffered(1))
    return pl.BlockSpec((None,) + tuple(shape), lambda i: (idx,) + (0,) * len(shape),
                        pipeline_mode=pl.Buffered(1))


CAST_STEPS = 16


def _cast_plumbing(jobs, step_of):
    in_specs, out_specs, out_shape, args = [], [], [], []
    for w, layer in jobs:
        _, rows, cols = w.shape
        blk = rows // CAST_STEPS
        in_specs.append(pl.BlockSpec((None, blk, cols),
                                     lambda *g, layer=layer: (layer, step_of(*g), 0)))
        out_specs.append(pl.BlockSpec((blk, cols), lambda *g: (step_of(*g), 0)))
        out_shape.append(jax.ShapeDtypeStruct((rows, cols), BF16))
        args.append(w)
    return in_specs, out_specs, out_shape, args


def _cast_blocks(in_refs, out_refs):
    for i_ref, o_ref in zip(in_refs, out_refs):
        o_ref[...] = i_ref[...].astype(BF16)


def _norm_spec(layer, k):
    return _lay((1, D), layer * 4 + k)


N_CTX_TILES = MP // TM
N_LAT_TILES = MS // TM


def _ctx_tile(i):
    return jnp.minimum(i, N_CTX_TILES - 1)


def _lat_tile(i):
    return jnp.maximum(i - N_CTX_TILES, 0)


def _joint_tile(i):
    return i


def _is_ctx():
    return pl.program_id(0) < N_CTX_TILES


def _tile_spec(width, tile_of):
    return pl.BlockSpec((TM, width), lambda i: (tile_of(i), 0))


def _ext_specs(n_rows, tile_of, width=D):
    r = TM // HB
    last_blk = n_rows // HB - 1
    return [
        pl.BlockSpec((TM, width), lambda i: (tile_of(i), 0)),
        pl.BlockSpec((HB, width), lambda i: (jnp.maximum(tile_of(i) * r - 1, 0), 0)),
        pl.BlockSpec((HB, width), lambda i: (jnp.minimum((tile_of(i) + 1) * r, last_blk), 0)),
    ]


def _fill_ext(scr, t_ref, p_ref, n_ref):
    scr[0:TM, :] = t_ref[...]
    scr[TM:TM + HB, :] = p_ref[...]
    scr[TM + HB:TM + HALO, :] = n_ref[...]


def _per_part(n_parts, fn, refs):
    if n_parts == 1:
        fn(*refs)
        return
    k = len(refs) // 2
    pl.when(_is_ctx())(lambda: fn(*refs[:k]))
    pl.when(jnp.logical_not(_is_ctx()))(lambda: fn(*refs[k:]))


def _ada_kernel(c_ref, w_ref, b_ref, o_ref):
    s = _silu(c_ref[...]).astype(BF16)
    o_ref[...] = _dot(s, w_ref[...].astype(BF16)) + b_ref[...]


def _ada(cond8, w_ada, b_ada):
    out = pl.pallas_call(
        _ada_kernel,
        grid=(DEPTH, 6),
        in_specs=[
            pl.BlockSpec((SUB, D), lambda l, k: (0, 0)),
            pl.BlockSpec((None, D, D), lambda l, k: (l, 0, k)),
            pl.BlockSpec((None, 1, D), lambda l, k: (l, 0, k)),
        ],
        out_specs=pl.BlockSpec((None, SUB, D), lambda l, k: (l * 6 + k, 0, 0)),
        out_shape=jax.ShapeDtypeStruct((DEPTH * 6, SUB, D), F32),
        compiler_params=_cp(("arbitrary", "arbitrary")),
        name="ada",
    )(cond8, w_ada, b_ada.reshape(DEPTH, 1, 6 * D))
    return out.reshape(DEPTH * 6 * SUB, 1, D)


def _mod_spec(layer, which, tile_of=_joint_tile):
    base = (layer * 6 + which) * SUB
    per_b = LS // TM

    def imap(i):
        t = tile_of(i)
        r = jnp.where(t < N_CTX_TILES, 0, 1 + (t - N_CTX_TILES) // per_b)
        return (base + r, 0, 0)

    return pl.BlockSpec((None, 1, D), imap)


def _hmod(x, nw_ref, sh_ref, sc_ref):
    return (_rms(x, nw_ref[...]) * (1.0 + sc_ref[...]) + sh_ref[...]).astype(BF16)


def _fill_h(h_scr, nw_ref, sh_ref, sc_ref, x_ref, xp_ref, xn_ref):
    h_scr[0:TM, :] = _hmod(x_ref[...], nw_ref, sh_ref, sc_ref)
    near = jnp.concatenate([xp_ref[HB - HH:HB, :], xn_ref[0:HH, :]], axis=0)
    h_scr[TM:TM + 2 * HH, :] = _hmod(near, nw_ref, sh_ref, sc_ref)


def _tile_flags():
    i = pl.program_id(0)
    is_ctx = _is_ctx()
    lseq = jnp.where(is_ctx, LP, LS)
    starts = ((i * TM) & (lseq - 1)) == 0
    ends = (((i + 1) * TM) & (lseq - 1)) == 0
    return is_ctx, starts, ends


def _edge_slabs():
    return sorted({b for b in range(0, TM, LP)} | {b + LP - SLAB for b in range(0, TM, LP)})


def _conv3_bulk(u, w, b):
    return (pltpu.roll(u, 1, axis=0) * w[0:1, :] + u * w[1:2, :]
            + pltpu.roll(u, TM - 1, axis=0) * w[2:3, :] + b)


def _conv3_slab(u_ext, s, w, b, flags):
    is_ctx, starts, ends = flags
    us = u_ext[s:s + SLAB, :]
    if s == 0:
        prev = jnp.where(starts, 0.0, u_ext[TM + HH - 1:TM + HH, :])
    else:
        prev = u_ext[s - 1:s, :]
        if s % LP == 0:
            prev = jnp.where(is_ctx, 0.0, prev)
    if s + SLAB == TM:
        nxt = jnp.where(ends, 0.0, u_ext[TM + HH:TM + HH + 1, :])
    else:
        nxt = u_ext[s + SLAB:s + SLAB + 1, :]
        if (s + SLAB) % LP == 0:
            nxt = jnp.where(is_ctx, 0.0, nxt)
    rows = lax.broadcasted_iota(jnp.int32, us.shape, 0)
    up = jnp.where(rows == 0, prev, pltpu.roll(us, 1, axis=0))
    dn = jnp.where(rows == SLAB - 1, nxt, pltpu.roll(us, SLAB - 1, axis=0))
    return up * w[0:1, :] + us * w[1:2, :] + dn * w[2:3, :] + b


def _tail_kernel(n_a, n_x, has_bias, split_out, *refs):
    a_refs, x_refs = refs[:3 * n_a], refs[3 * n_a:3 * (n_a + n_x)]
    rest = list(refs[3 * (n_a + n_x):])
    w_ref = rest.pop(0)
    b_ref = rest.pop(0) if has_bias else None
    (nw1_ref, g1_ref, nw2_ref, sh_ref, sc_ref, wup_ref, wdw_ref, bdw_ref, wdn_ref, nw3_ref,
     g2_ref) = rest[:11]
    n_out = 2 if split_out else 1
    out_refs = rest[11:11 + n_out]
    a_scr, x_scr, h_scr, act_scr = rest[11 + n_out:]

    _per_part(n_a, functools.partial(_fill_ext, a_scr), a_refs)
    _per_part(n_x, functools.partial(_fill_ext, x_scr), x_refs)
    y = _dot(a_scr[...], w_ref[...])
    if has_bias:
        y = y + b_ref[...]
    x1 = x_scr[...] + g1_ref[...] * _rms(y, nw1_ref[...])
    x_scr[...] = x1
    h_scr[0:TM, :] = _hmod(x1[0:TM, :], nw2_ref, sh_ref, sc_ref)
    h_scr[TM:, :] = _hmod(x1[TM + HB - HH:TM + HB + HH, :], nw2_ref, sh_ref, sc_ref)

    flags = _tile_flags()
    h = h_scr[...]
    for c in range(D_FF // CK):
        cg = slice(c * CK, (c + 1) * CK)
        cv = slice(D_FF + c * CK, D_FF + (c + 1) * CK)
        g_ext = _dot(h, wup_ref[:, cg])
        v_ext = _dot(h, wup_ref[:, cv])
        wg, bg, wv, bv = wdw_ref[:, cg], bdw_ref[:, cg], wdw_ref[:, cv], bdw_ref[:, cv]
        g = _conv3_bulk(g_ext[0:TM, :], wg, bg)
        val = _conv3_bulk(v_ext[0:TM, :], wv, bv)
        act_scr[:, cg] = (_silu(g) * val).astype(BF16)
        for s in _edge_slabs():
            g = _conv3_slab(g_ext, s, wg, bg, flags)
            val = _conv3_slab(v_ext, s, wv, bv, flags)
            act_scr[s:s + SLAB, cg] = (_silu(g) * val).astype(BF16)
    y = _dot(act_scr[...], wdn_ref[...])
    res = x_scr[0:TM, :] + g2_ref[...] * _rms(y, nw3_ref[...])
    if split_out:
        is_ctx = flags[0]

        @pl.when(is_ctx)
        def _():
            out_refs[0][...] = res

        @pl.when(jnp.logical_not(is_ctx))
        def _():
            out_refs[1][...] = res
    else:
        out_refs[0][...] = res


def _tail(a_parts, x_parts, nw_all, mods, layer, j, w_out_bf, b_out, wup_bf, w_dw, b_dw, wdn_bf,
          split_out):
    k = w_out_bf.shape[0]
    has_bias = b_out is not None

    def specs(parts, width):
        if len(parts) == 1:
            return _ext_specs(M, _joint_tile, width)
        return _ext_specs(MP, _ctx_tile, width) + _ext_specs(MS, _lat_tile, width)

    in_specs = specs(a_parts, k) + specs(x_parts, D) + [_lay((k, D), None)]
    args = [p for p in a_parts for _ in range(3)] + [p for p in x_parts for _ in range(3)]
    args.append(w_out_bf)
    if has_bias:
        in_specs.append(_lay((1, D), j))
        args.append(b_out)
    in_specs += [
        _norm_spec(layer, 1), _mod_spec(layer, 2),
        _norm_spec(layer, 2), _mod_spec(layer, 3), _mod_spec(layer, 4),
        _lay((D, 2 * D_FF), None), _lay((3, 2 * D_FF), layer), _lay((1, 2 * D_FF), layer),
        _lay((D_FF, D), None),
        _norm_spec(layer, 3), _mod_spec(layer, 5),
    ]
    args += [nw_all, mods, nw_all, mods, mods, wup_bf, w_dw, b_dw, wdn_bf, nw_all, mods]
    if split_out:
        out_specs = [_tile_spec(D, _ctx_tile), _tile_spec(D, _lat_tile)]
        out_shape = [jax.ShapeDtypeStruct((MP, D), F32), jax.ShapeDtypeStruct((MS, D), F32)]
    else:
        out_specs = _tile_spec(D, _joint_tile)
        out_shape = jax.ShapeDtypeStruct((M, D), F32)
    return pl.pallas_call(
        functools.partial(_tail_kernel, len(a_parts), len(x_parts), has_bias, split_out),
        grid=(M // TM,),
        in_specs=in_specs,
        out_specs=out_specs,
        out_shape=out_shape,
        scratch_shapes=[pltpu.VMEM((TM + HALO, k), BF16), pltpu.VMEM((TM + HALO, D), F32),
                        pltpu.VMEM((TM + 2 * HH, D), BF16), pltpu.VMEM((TM, D_FF), BF16)],
        compiler_params=_cp(("arbitrary",)),
        name="tail",
    )(*args)


def _hy_in_kernel(n_parts, n_cast, *refs):
    x_refs = refs[:3 * n_parts]
    nw_ref, sh_ref, sc_ref, w_ref, b_ref, ws_ref, bs_ref = refs[3 * n_parts:3 * n_parts + 7]
    rest = refs[3 * n_parts + 7:]
    x0_ref, z_ref, h_scr = rest[n_cast], rest[n_cast + 1], rest[-1]
    _cast_blocks(rest[:n_cast], rest[n_cast + 2:-1])
    _per_part(n_parts, functools.partial(_fill_h, h_scr, nw_ref, sh_ref, sc_ref), x_refs)
    flags = _tile_flags()
    h = h_scr[...]
    for c in range(D // CK):
        cc = slice(c * CK, (c + 1) * CK)
        u_ext, ws, bs = [], [], []
        for s in range(3):
            cs = slice(s * D + c * CK, s * D + (c + 1) * CK)
            u_ext.append(_dot(h, w_ref[:, cs]) + b_ref[:, cs])
            ws.append(ws_ref[:, cs])
            bs.append(bs_ref[:, cs])
        out = [_conv3_bulk(u_ext[s][0:TM, :], ws[s], bs[s]) for s in range(3)]
        x0_ref[:, cc] = out[0].astype(BF16)
        z_ref[:, cc] = (out[2] * out[1]).astype(BF16)
        for r in _edge_slabs():
            out = [_conv3_slab(u_ext[s], r, ws[s], bs[s], flags) for s in range(3)]
            x0_ref[r:r + SLAB, cc] = out[0].astype(BF16)
            z_ref[r:r + SLAB, cc] = (out[2] * out[1]).astype(BF16)


def _hy_in(x_parts, nw_all, mods, layer, j, w_bf, b_in, w_short, b_short, cast_jobs):
    assert M // TM == CAST_STEPS
    c_in, c_out, c_shape, c_args = _cast_plumbing(cast_jobs, lambda i: i)
    if len(x_parts) == 1:
        x_specs = _ext_specs(M, _joint_tile)
    else:
        x_specs = _ext_specs(MP, _ctx_tile) + _ext_specs(MS, _lat_tile)
    x_args = [a for a in x_parts for _ in range(3)]
    return pl.pallas_call(
        functools.partial(_hy_in_kernel, len(x_parts), len(c_in)),
        grid=(M // TM,),
        in_specs=x_specs + [
            _norm_spec(layer, 0), _mod_spec(layer, 0), _mod_spec(layer, 1),
            _lay((D, 3 * D), None), _lay((1, 3 * D), j), _lay((3, 3 * D), j), _lay((1, 3 * D), j),
        ] + c_in,
        out_specs=[_tile_spec(D, _joint_tile)] * 2 + c_out,
        out_shape=[jax.ShapeDtypeStruct((M, D), BF16)] * 2 + c_shape,
        scratch_shapes=[pltpu.VMEM((TM + 2 * HH, D), BF16)],
        compiler_params=_cp(("arbitrary",)),
        name="hy_in",
    )(*x_args, nw_all, mods, mods, w_bf, b_in, w_short, b_short, *c_args)


def _dft_mats():
    n = np.arange(CB, dtype=np.float64)
    f = np.arange(CB, dtype=np.float64)
    ang = 2.0 * np.pi * np.outer(f, n) / (2 * CB)
    fwd = np.concatenate([np.cos(ang), -np.sin(ang)], axis=0)
    fwd[CB] = np.cos(np.pi * n)
    scale = np.full((2 * CB, 1), 2.0 / (2 * CB))
    scale[0] = scale[CB] = 1.0 / (2 * CB)
    inv = (fwd * scale).T
    sgn = np.where(np.arange(CB) % 2 == 0, 1.0, -1.0)
    sgn2 = np.concatenate([sgn, sgn])[:, None]
    sgn2[CB] = 1.0
    return (jnp.asarray(fwd, F32).astype(BF16), jnp.asarray(inv, F32).astype(BF16),
            jnp.asarray(sgn2, F32))


def _filter_features(L):
    t = np.linspace(0.0, 1.0, L)[:, None]
    w = 2.0 * np.pi * np.arange(L)[:, None] / L
    bands = np.linspace(1e-4, EMB_BANDS - 1, EMB_BANDS)
    z = np.concatenate([t, np.cos(bands * w), -np.sin(bands * w)], axis=-1)
    return t, z


def _filter_tables():
    zs, ts, ms = [], [], []
    for L in (LP, LS):
        t, z = _filter_features(L)
        z = np.pad(z, ((0, 0), (0, FILTER_HIDDEN - EMB_DIM)))
        idx = np.abs(np.arange(2 * L) - L) % L
        zs.append(z[idx])
        ts.append(t[idx])
        ms.append((np.arange(2 * L) != 0).astype(np.float64)[:, None])
    return tuple(jnp.asarray(np.concatenate(a), F32) for a in (zs, ts, ms))


N_WIN = 1 + 2 * NB_S - 1


def _filt_kernel(z_ref, t_ref, m_ref, w1_ref, b1_ref, fr_ref, w2_ref, b2_ref, w3_ref,
                 ad_ref, f_ref, sg_ref, g_ref, nyq_ref, prev_scr):
    c = pl.program_id(0)

    @pl.when(c == 0)
    def _():
        prev_scr[...] = jnp.zeros_like(prev_scr)

    fr = fr_ref[...]
    hid = jnp.sin(fr * (_dot(z_ref[...].astype(BF16), w1_ref[...].astype(BF16)) + b1_ref[...]))
    hid = jnp.sin(fr * (_dot(hid.astype(BF16), w2_ref[...].astype(BF16)) + b2_ref[...]))
    h = _dot(hid.astype(BF16), w3_ref[...].astype(BF16))
    taps = h * jnp.exp(-t_ref[...] * ad_ref[...]) * m_ref[...]
    a = _dot(f_ref[...], taps.astype(BF16))
    g = a + sg_ref[...] * prev_scr[...]
    prev_scr[...] = a
    g_ref[...] = g
    nyq_ref[...] = g[CB:CB + 1, :]
    g_ref[CB:CB + 1, :] = jnp.zeros((1, D), F32)


def _is_bwd_chunk(c):
    return jnp.logical_or(c == 0, jnp.logical_and(c >= 2, c < 2 + NB_S))


def _window_of_chunk(c):
    return jnp.where(c <= 1, 0, jnp.maximum(c - 2, 1))


def _filter_spectra(tabs, fwd, sgn, j, w1, b1, freq, w2, b2, w3, absd):
    z_tab, t_tab, m_tab = tabs
    small = lambda shape: pl.BlockSpec(shape, lambda c: (0, 0))
    hid = _lay((1, FILTER_HIDDEN), j)
    return pl.pallas_call(
        _filt_kernel,
        grid=(N_CHUNK,),
        in_specs=[
            pl.BlockSpec((CB, FILTER_HIDDEN), lambda c: (c, 0)),
            pl.BlockSpec((CB, 1), lambda c: (c, 0)),
            pl.BlockSpec((CB, 1), lambda c: (c, 0)),
            _lay((FILTER_HIDDEN, FILTER_HIDDEN), j), hid, hid,
            _lay((FILTER_HIDDEN, FILTER_HIDDEN), j), hid,
            pl.BlockSpec((None, FILTER_HIDDEN, D),
                         lambda c: (j, 0, jnp.where(_is_bwd_chunk(c), 1, 0))),
            small((1, D)),
            small((2 * CB, CB)),
            small((2 * CB, 1)),
        ],
        out_specs=[pl.BlockSpec((None, 2 * CB, D), lambda c: (_window_of_chunk(c), 0, 0)),
                   pl.BlockSpec((None, 1, D), lambda c: (_window_of_chunk(c), 0, 0))],
        out_shape=[jax.ShapeDtypeStruct((N_WIN, 2 * CB, D), F32),
                   jax.ShapeDtypeStruct((N_WIN, 1, D), F32)],
        scratch_shapes=[pltpu.VMEM((2 * CB, D), F32)],
        compiler_params=_cp(("arbitrary",)),
        name="hy_filter",
    )(z_tab, t_tab, m_tab, w1, b1, freq, w2, b2, w3, absd, fwd, sgn)


def _conv_kernel(tc, n_cast, z_ref, x0_ref, g_ref, nyq_ref, f_ref, gi_ref, db_ref, *rest):
    o_ref, u_scr = rest[n_cast], rest[-1]
    _cast_blocks(rest[:n_cast], rest[n_cast + 1:-1])
    unit = pl.program_id(1)
    fmat = f_ref[...]
    gmat = gi_ref[...]
    row0 = lax.broadcasted_iota(jnp.int32, (CB, tc), 0) == 0
    db = db_ref[...]

    def emit(blk, yre, yim, ynyq):
        rows = pl.ds(blk * CB, CB)
        yspec = jnp.concatenate([yre, jnp.where(row0, ynyq, yim)], axis=0)
        y = _dot(gmat, yspec.astype(BF16))
        o_ref[rows, :] = (x0_ref[rows, :] * (y + z_ref[rows, :] * db)).astype(BF16)

    @pl.when(unit < MP // LS)
    def _():
        gre, gim, gnyq = g_ref[0, :CB, :], g_ref[0, CB:, :], nyq_ref[0]
        for s in range(LS // LP):
            u = _dot(fmat, z_ref[pl.ds(s * CB, CB), :])
            ure, unyq = u[:CB], u[CB:CB + 1]
            uim = jnp.where(row0, 0.0, u[CB:])
            emit(s, gre * ure - gim * uim, gre * uim + gim * ure, gnyq * unyq)

    @pl.when(unit >= MP // LS)
    def _():
        unyq = []
        for j in range(NB_S):
            u = _dot(fmat, z_ref[pl.ds(j * CB, CB), :])
            unyq.append(u[CB:CB + 1])
            u_scr[j] = u
            u_scr[j, CB:CB + 1, :] = jnp.zeros((1, tc), F32)
        for i in range(NB_S):
            win = [NB_S + i - j for j in range(NB_S)]
            parts_re, parts_im = [], []
            for r in range(0, CB, MAC_ROWS):
                re, im = slice(r, r + MAC_ROWS), slice(CB + r, CB + r + MAC_ROWS)
                yre = yim = None
                for j, w in enumerate(win):
                    gre, gim = g_ref[w, re, :], g_ref[w, im, :]
                    ure, uim = u_scr[j, re, :], u_scr[j, im, :]
                    pre, pim = gre * ure - gim * uim, gre * uim + gim * ure
                    yre = pre if yre is None else yre + pre
                    yim = pim if yim is None else yim + pim
                parts_re.append(yre)
                parts_im.append(yim)
            ynyq = nyq_ref[win[0]] * unyq[0]
            for j in range(1, NB_S):
                ynyq = ynyq + nyq_ref[win[j]] * unyq[j]
            emit(i, jnp.concatenate(parts_re, axis=0), jnp.concatenate(parts_im, axis=0), ynyq)


def _long_conv(z, x0, spectra, fwd, inv, d_bias, j, cast_jobs, tc=256):
    g_win, g_nyq = spectra
    units = M // LS
    assert (D // tc) * units == CAST_STEPS
    c_in, c_out, c_shape, c_args = _cast_plumbing(cast_jobs, lambda c, u: c * units + u)
    return pl.pallas_call(
        functools.partial(_conv_kernel, tc, len(c_in)),
        grid=(D // tc, units),
        in_specs=[
            pl.BlockSpec((LS, tc), lambda c, u: (u, c)),
            pl.BlockSpec((LS, tc), lambda c, u: (u, c)),
            pl.BlockSpec((N_WIN, 2 * CB, tc), lambda c, u: (0, 0, c)),
            pl.BlockSpec((N_WIN, 1, tc), lambda c, u: (0, 0, c)),
            pl.BlockSpec((2 * CB, CB), lambda c, u: (0, 0)),
            pl.BlockSpec((CB, 2 * CB), lambda c, u: (0, 0)),
            pl.BlockSpec((None, 1, tc), lambda c, u: (j, 0, c)),
        ] + c_in,
        out_specs=[pl.BlockSpec((LS, tc), lambda c, u: (u, c))] + c_out,
        out_shape=[jax.ShapeDtypeStruct((M, D), BF16)] + c_shape,
        scratch_shapes=[pltpu.VMEM((NB_S, 2 * CB, tc), F32)],
        compiler_params=_cp(("arbitrary", "arbitrary")),
        name="hy_conv",
    )(z, x0, g_win, g_nyq, fwd, inv, d_bias, *c_args)


def _rope_tables(L):
    rows = L // GRID_W
    r = np.repeat(np.arange(rows, dtype=np.float64), GRID_W)
    cidx = np.tile(np.arange(GRID_W, dtype=np.float64), rows)
    inv = ROPE_THETA ** (-np.arange(ROT_FREQS, dtype=np.float64) / ROT_FREQS)
    ar = r[:, None] * inv
    ac = cidx[:, None] * inv
    cos = np.concatenate([np.cos(ar), np.cos(ar), np.cos(ac), np.cos(ac)] * 2, axis=-1)
    sin = np.concatenate([np.sin(ar), np.sin(ar), np.sin(ac), np.sin(ac)] * 2, axis=-1)
    first_half = (np.arange(V_DIM) % (2 * ROT_FREQS)) < ROT_FREQS
    sin_a = np.where(first_half, -sin, 0.0)
    sin_b = np.where(first_half, 0.0, sin)
    return tuple(jnp.asarray(a, F32) for a in (cos, sin_a, sin_b))


def _rope(x, cos, sin_a, sin_b):
    return (x * cos + pltpu.roll(x, V_DIM - ROT_FREQS, axis=1) * sin_a
            + pltpu.roll(x, ROT_FREQS, axis=1) * sin_b)


HEADS_PER_CHUNK = CK // V_DIM


def _qkv_chunks():
    per_part = D // CK
    return [(slice(c * CK, (c + 1) * CK), c // per_part, (c % per_part) * HEADS_PER_CHUNK)
            for c in range(3 * per_part)]


def _qkv_c_kernel(first, x_ref, nw_ref, sh_ref, sc_ref, w_ref, *rest):
    qkv_ref, nk_ref, nv_ref = rest[-3:]
    if first:
        for ref in (nk_ref, nv_ref):
            ref[:, 1:] = jnp.zeros((TM // LP, N_ATTN - 1, N_HEADS, LP, V_DIM), F32)
    h = _hmod(x_ref[...], nw_ref, sh_ref, sc_ref)
    for cs, part, head0 in _qkv_chunks():
        u = _dot(h, w_ref[:, cs])
        qkv_ref[:, cs] = (u * QSCALE if part == 0 else u).astype(BF16)
        if part > 0:
            cache = nk_ref if part == 1 else nv_ref
            for s in range(TM // LP):
                for hh in range(HEADS_PER_CHUNK):
                    blk = u[s * LP:(s + 1) * LP, hh * V_DIM:(hh + 1) * V_DIM]
                    if first:
                        cache[s, 0, head0 + hh] = blk
                    else:
                        cache[s, head0 + hh] = blk


def _qkv_ctx(x, nw_all, mods, layer, j, w_bf, prev):
    seqs = TM // LP
    in_specs = [_tile_spec(D, _joint_tile), _norm_spec(layer, 0), _mod_spec(layer, 0),
                _mod_spec(layer, 1), _lay((D, 3 * D), None)]
    args = [x, nw_all, mods, mods, w_bf]
    aliases = {}
    if prev is None:
        assert j == 0
        cache = pl.BlockSpec((seqs, N_ATTN, N_HEADS, LP, V_DIM), lambda i: (i, 0, 0, 0, 0))
    else:
        in_specs += [pl.BlockSpec(memory_space=pl.ANY)] * 2
        args += list(prev)
        aliases = {len(args) - 2: 1, len(args) - 1: 2}
        cache = pl.BlockSpec((seqs, None, N_HEADS, LP, V_DIM), lambda i: (i, j, 0, 0, 0))
    cache_shape = jax.ShapeDtypeStruct((BP, N_ATTN, N_HEADS, LP, V_DIM), F32)
    return pl.pallas_call(
        functools.partial(_qkv_c_kernel, prev is None),
        grid=(N_CTX_TILES,),
        in_specs=in_specs,
        out_specs=[_tile_spec(3 * D, _joint_tile), cache, cache],
        out_shape=[jax.ShapeDtypeStruct((MP, 3 * D), BF16), cache_shape, cache_shape],
        input_output_aliases=aliases,
        compiler_params=_cp(("arbitrary",)),
        name="qkv_ctx",
    )(*args)


def _qkv_l_kernel(x_ref, nw_ref, sh_ref, sc_ref, w_ref, cos_ref, sa_ref, sb_ref, qkv_ref):
    h = _hmod(x_ref[...], nw_ref, sh_ref, sc_ref)
    for cs, part, _ in _qkv_chunks():
        u = _dot(h, w_ref[:, cs])
        if part == 2:
            qkv_ref[:, cs] = u.astype(BF16)
            continue
        for hh in range(HEADS_PER_CHUNK):
            r = _rope(u[:, hh * V_DIM:(hh + 1) * V_DIM], cos_ref[...], sa_ref[...], sb_ref[...])
            if part == 0:
                r = r * QSCALE
            qkv_ref[:, cs.start + hh * V_DIM:cs.start + (hh + 1) * V_DIM] = r.astype(BF16)


def _qkv_lat(x, nw_all, mods, layer, j, w_bf, ropes):
    tile = lambda i: i + N_CTX_TILES
    tab = pl.BlockSpec((TM, V_DIM), lambda i: (i % (LS // TM), 0))
    return pl.pallas_call(
        _qkv_l_kernel,
        grid=(N_LAT_TILES,),
        in_specs=[_tile_spec(D, tile), _norm_spec(layer, 0), _mod_spec(layer, 0, tile),
                  _mod_spec(layer, 1, tile), _lay((D, 3 * D), None), tab, tab, tab],
        out_specs=_tile_spec(3 * D, _joint_tile),
        out_shape=jax.ShapeDtypeStruct((MS, 3 * D), BF16),
        compiler_params=_cp(("arbitrary",)),
        name="qkv_lat",
    )(x, nw_all, mods, mods, w_bf, *ropes)


def _lambda(lv, lam_init):
    a = jnp.exp(jnp.sum(lv[0:1, :] * lv[1:2, :], axis=-1, keepdims=True))
    b = jnp.exp(jnp.sum(lv[2:3, :] * lv[3:4, :], axis=-1, keepdims=True))
    return a - b + lam_init


def _diff_attn(q, chunks, lam, lam_init, subln):
    t = q.shape[0]
    lane = lax.broadcasted_iota(jnp.int32, q.shape, 1)
    zero = jnp.zeros_like(q)
    q2 = jnp.concatenate([jnp.where(lane < HEAD_DIM, q, zero),
                          jnp.where(lane < HEAD_DIM, zero, q)], axis=0)
    m = l = acc = None
    for k, v in chunks:
        s = lax.dot_general(q2, k, (((1,), (1,)), ((), ())), preferred_element_type=F32)
        mc = jnp.max(s, axis=-1, keepdims=True)
        m_new = mc if m is None else jnp.maximum(m, mc)
        e = jnp.exp2(s - m_new)
        lc = jnp.sum(e, axis=-1, keepdims=True)
        pv = _dot(e.astype(BF16), v)
        if m is None:
            l, acc = lc, pv
        else:
            alpha = jnp.exp2(m - m_new)
            l = alpha * l + lc
            acc = alpha * acc + pv
        m = m_new
    o = acc[:t] * (1.0 / l[:t]) - acc[t:] * (lam / l[t:])
    return _rms(o, subln) * (1.0 - lam_init)


def _attn_c_kernel(lam_init, q_ref, k_ref, v_ref, lv_ref, sub_ref, o_ref):
    lam = _lambda(lv_ref[...], lam_init)
    t = LP
    probs = [(slice(s * LP, (s + 1) * LP), slice(h * V_DIM, (h + 1) * V_DIM))
             for s in range(CTX_SEQS) for h in range(N_HEADS)]
    lane = lax.broadcasted_iota(jnp.int32, (t, V_DIM), 1)
    zero = jnp.zeros((t, V_DIM), BF16)
    dn = (((1,), (1,)), ((), ()))
    s = []
    for rs, hs in probs:
        q = q_ref[rs, hs]
        q2 = jnp.concatenate([jnp.where(lane < HEAD_DIM, q, zero),
                              jnp.where(lane < HEAD_DIM, zero, q)], axis=0)
        s.append(lax.dot_general(q2, k_ref[rs, hs], dn, preferred_element_type=F32))
    e = [jnp.exp2(x - jnp.max(x, axis=-1, keepdims=True)) for x in s]
    l = [jnp.sum(x, axis=-1, keepdims=True) for x in e]
    pv = [_dot(x.astype(BF16), v_ref[rs, hs]) for x, (rs, hs) in zip(e, probs)]
    for (rs, hs), acc, lh in zip(probs, pv, l):
        o = acc[:t] * (1.0 / lh[:t]) - acc[t:] * (lam / lh[t:])
        o_ref[rs, hs] = (_rms(o, sub_ref[...]) * (1.0 - lam_init)).astype(BF16)


def _attn_ctx(qkv_c, lamv, subln, j, lam_init):
    part = lambda p: pl.BlockSpec((CTX_SEQS * LP, D), lambda b: (b, p))
    return pl.pallas_call(
        functools.partial(_attn_c_kernel, lam_init),
        grid=(BP // CTX_SEQS,),
        in_specs=[part(0), part(1), part(2),
                  pl.BlockSpec((None, 4, HEAD_DIM), lambda b: (j, 0, 0)),
                  pl.BlockSpec((None, 1, V_DIM), lambda b: (j, 0, 0))],
        out_specs=pl.BlockSpec((CTX_SEQS * LP, D), lambda b: (b, 0)),
        out_shape=jax.ShapeDtypeStruct((MP, D), BF16),
        compiler_params=_cp(("arbitrary",)),
        name="attn_ctx",
    )(qkv_c, qkv_c, qkv_c, lamv, subln)


def _attn_l_kernel(lam_init, n_cast, q_ref, k_ref, v_ref, ck_ref, cv_ref, lv_ref, sub_ref, *rest):
    o_ref = rest[n_cast]
    _cast_blocks(rest[:n_cast], rest[n_cast + 1:])
    lam = _lambda(lv_ref[...], lam_init)
    for h in range(ATT_HEADS):
        hs = slice(h * V_DIM, (h + 1) * V_DIM)
        chunks = [(ck_ref[h].astype(BF16), cv_ref[h].astype(BF16))]
        for c in range(LS // KEY_CHUNK):
            rows = pl.ds(c * KEY_CHUNK, KEY_CHUNK)
            chunks.append((k_ref[rows, hs], v_ref[rows, hs]))
        o = _diff_attn(q_ref[:, hs], chunks, lam, lam_init, sub_ref[...])
        o_ref[:, hs] = o.astype(BF16)


def _attn_lat(qkv_l, cache_k, cache_v, lamv, subln, j, lam_init, cast_jobs, tq=512):
    nq = LS // tq
    width = ATT_HEADS * V_DIM
    groups = N_HEADS // ATT_HEADS
    assert BS * groups * nq == CAST_STEPS
    c_in, c_out, c_shape, c_args = _cast_plumbing(
        cast_jobs, lambda b, g, q: (b * groups + g) * nq + q)
    seq = lambda part: pl.BlockSpec((LS, width), lambda b, g, q: (b, part * groups + g))
    ctx = pl.BlockSpec((None, None, ATT_HEADS, PAST, V_DIM), lambda b, g, q: (b, j, g, 0, 0))
    return pl.pallas_call(
        functools.partial(_attn_l_kernel, lam_init, len(c_in)),
        grid=(BS, groups, nq),
        in_specs=[pl.BlockSpec((tq, width), lambda b, g, q: (b * nq + q, g)),
                  seq(1), seq(2), ctx, ctx,
                  pl.BlockSpec((None, 4, HEAD_DIM), lambda b, g, q: (j, 0, 0)),
                  pl.BlockSpec((None, 1, V_DIM), lambda b, g, q: (j, 0, 0))] + c_in,
        out_specs=[pl.BlockSpec((tq, width), lambda b, g, q: (b * nq + q, g))] + c_out,
        out_shape=[jax.ShapeDtypeStruct((MS, D), BF16)] + c_shape,
        compiler_params=_cp(("arbitrary", "arbitrary", "arbitrary")),
        name="attn_lat",
    )(qkv_l, qkv_l, qkv_l, cache_k, cache_v, lamv, subln, *c_args)


def kernel(x_prompt, x_sample, cache_k, cache_v, c, c_ctx, w_ada, b_ada, norm_w, hy_w_in, hy_b_in, hy_w_short, hy_b_short, hy_f_w1, hy_f_b1, hy_f_freq, hy_f_w2, hy_f_b2, hy_f_w3, hy_d_bias, hy_w_out, hy_b_out, at_w_qkv, at_w_out, at_lambda_q1, at_lambda_k1, at_lambda_q2, at_lambda_k2, at_subln, ffn_w_up, ffn_w_dw, ffn_b_dw, ffn_w_down):
    cond8 = jnp.concatenate([c_ctx[None, :], c, jnp.zeros((SUB - 1 - BS, D), F32)], axis=0)
    mods = _ada(cond8, w_ada, b_ada)

    fwd, inv, sgn = _dft_mats()
    tabs = _filter_tables()
    min_decay = math.log(DECAY_TARGET) / DECAY_PCT_LONG
    max_decay = math.log(DECAY_TARGET) / DECAY_PCT_SHORT
    absd = jnp.asarray(np.abs(np.linspace(min_decay, max_decay, D))[None, :], F32)
    ropes = _rope_tables(LS)

    row = lambda a: a.reshape(a.shape[0], 1, a.shape[1])
    nw_all = norm_w.reshape(DEPTH * 4, 1, D)
    w1_pad = jnp.pad(hy_f_w1, ((0, 0), (0, FILTER_HIDDEN - EMB_DIM), (0, 0)))
    lamv = jnp.stack([at_lambda_q1, at_lambda_k1, at_lambda_q2, at_lambda_k2], axis=1)
    subln = row(at_subln)
    ffn = lambda l: [(ffn_w_up, l), (ffn_w_down, l)]
    w_in_bf = hy_w_in[0].astype(BF16)

    x_parts = [x_prompt.reshape(MP, D), x_sample.reshape(MS, D)]
    caches = None
    for i in range(DEPTH):
        j = i // 2
        last = i == DEPTH - 1
        if i % 2 == 0:
            own = ffn(0) + [(hy_w_out, 0)] if i == 0 else []
            x0, z, *cast = _hy_in(x_parts, nw_all, mods, i, j, w_in_bf, row(hy_b_in), hy_w_short,
                                  row(hy_b_short),
                                  own + ffn(i + 1) + [(at_w_qkv, j), (at_w_out, j)])
            if i == 0:
                w_up_bf, w_down_bf, w_out_bf = cast[:3]
            nxt = cast[len(own):]
            spectra = _filter_spectra(tabs, fwd, sgn, j, w1_pad, row(hy_f_b1), row(hy_f_freq),
                                      hy_f_w2, row(hy_f_b2), hy_f_w3, absd)
            a, = _long_conv(z, x0, spectra, fwd, inv, row(hy_d_bias), j, [])
            a_parts, b_out = [a], row(hy_b_out)
        else:
            lam_init = 0.8 - 0.6 * math.exp(-0.3 * i)
            x = x_parts[0]
            qkv_c, new_k, new_v = _qkv_ctx(x, nw_all, mods, i, j, w_qkv_bf, caches)
            caches = (new_k, new_v)
            qkv_l = _qkv_lat(x, nw_all, mods, i, j, w_qkv_bf, ropes)
            jobs = [] if last else ffn(i + 1) + [(hy_w_in, j + 1), (hy_w_out, j + 1)]
            o_lat, *nxt = _attn_lat(qkv_l, cache_k, cache_v, lamv, subln, j, lam_init, jobs)
            a_parts, b_out = [_attn_ctx(qkv_c, lamv, subln, j, lam_init), o_lat], None
        y = _tail(a_parts, x_parts, nw_all, mods, i, j, w_out_bf, b_out, w_up_bf, ffn_w_dw,
                  row(ffn_b_dw), w_down_bf, split_out=last)
        x_parts = list(y) if last else [y]
        if not last:
            w_up_bf, w_down_bf, w_next_in, w_out_bf = nxt
            if i % 2 == 0:
                w_qkv_bf = w_next_in
            else:
                w_in_bf = w_next_in

    return (x_parts[0].reshape(BP, LP, D), x_parts[1].reshape(BS, LS, D), caches[0], caches[1])
```

```python
import functools
import math

import numpy as np
import jax
import jax.numpy as jnp
from jax import lax
from jax.experimental import pallas as pl
from jax.experimental.pallas import tpu as pltpu

D = 1024
BP, LP = 16, 256
BS, LS = 2, 2048
MP = BP * LP
MS = BS * LS
M = MP + MS
DEPTH = 4
N_ATTN = DEPTH // 2
GRID_W = 64
N_HEADS = 8
HEAD_DIM = 64
V_DIM = 128
ROPE_THETA = 10000.0
ROT_FREQS = 16
EMB_BANDS = 16
EMB_DIM = 33
FILTER_HIDDEN = 64
DECAY_TARGET = 1e-2
DECAY_PCT_SHORT = 0.3
DECAY_PCT_LONG = 1.5
D_FF = 2816
EPS = 1e-6
PAST = 256

CB = 256
NB_S = LS // CB
N_CHUNK = 2 + 2 * NB_S
SUB = 8
HB = 16
HALO = 2 * HB
HH = SUB
SLAB = 16
CK = 256
TM = 512
KEY_CHUNK = 2048
ATT_HEADS = 4
CTX_SEQS = 4
MAC_ROWS = 32
VMEM_LIMIT = 56 * 1024 * 1024
QSCALE = HEAD_DIM ** -0.5 * math.log2(math.e)

F32 = jnp.float32
BF16 = jnp.bfloat16


def _dot(a, b):
    return jnp.dot(a, b, preferred_element_type=F32)


def _rms(x, w):
    ms = jnp.mean(x * x, axis=-1, keepdims=True)
    return x * lax.rsqrt(ms + EPS) * w


def _silu(x):
    return x / (1.0 + jnp.exp(-x))


def _cp(sem, vmem=VMEM_LIMIT):
    return pltpu.CompilerParams(dimension_semantics=sem, vmem_limit_bytes=vmem)


def _lay(shape, idx):
    if idx is None:
        return pl.BlockSpec(tuple(shape), lambda i: (0,) * len(shape),
                            pipeline_mode=pl.Buffered(1))
    return pl.BlockSpec((None,) + tuple(shape), lambda i: (idx,) + (0,) * len(shape),
                        pipeline_mode=pl.Buffered(1))


CAST_STEPS = 16


def _cast_plumbing(jobs, step_of):
    in_specs, out_specs, out_shape, args = [], [], [], []
    for w, layer in jobs:
        _, rows, cols = w.shape
        blk = rows // CAST_STEPS
        in_specs.append(pl.BlockSpec((None, blk, cols),
                                     lambda *g, layer=layer: (layer, step_of(*g), 0)))
        out_specs.append(pl.BlockSpec((blk, cols), lambda *g: (step_of(*g), 0)))
        out_shape.append(jax.ShapeDtypeStruct((rows, cols), BF16))
        args.append(w)
    return in_specs, out_specs, out_shape, args


def _cast_blocks(in_refs, out_refs):
    for i_ref, o_ref in zip(in_refs, out_refs):
        o_ref[...] = i_ref[...].astype(BF16)


def _norm_spec(layer, k):
    return _lay((1, D), layer * 4 + k)


N_CTX_TILES = MP // TM
N_LAT_TILES = MS // TM


def _ctx_tile(i):
    return jnp.minimum(i, N_CTX_TILES - 1)


def _lat_tile(i):
    return jnp.maximum(i - N_CTX_TILES, 0)


def _joint_tile(i):
    return i


def _is_ctx():
    return pl.program_id(0) < N_CTX_TILES


def _tile_spec(width, tile_of):
    return pl.BlockSpec((TM, width), lambda i: (tile_of(i), 0))


def _ext_specs(n_rows, tile_of, width=D):
    r = TM // HB
    last_blk = n_rows // HB - 1
    return [
        pl.BlockSpec((TM, width), lambda i: (tile_of(i), 0)),
        pl.BlockSpec((HB, width), lambda i: (jnp.maximum(tile_of(i) * r - 1, 0), 0)),
        pl.BlockSpec((HB, width), lambda i: (jnp.minimum((tile_of(i) + 1) * r, last_blk), 0)),
    ]


def _fill_ext(scr, t_ref, p_ref, n_ref):
    scr[0:TM, :] = t_ref[...]
    scr[TM:TM + HB, :] = p_ref[...]
    scr[TM + HB:TM + HALO, :] = n_ref[...]


def _per_part(n_parts, fn, refs):
    if n_parts == 1:
        fn(*refs)
        return
    k = len(refs) // 2
    pl.when(_is_ctx())(lambda: fn(*refs[:k]))
    pl.when(jnp.logical_not(_is_ctx()))(lambda: fn(*refs[k:]))


ADA_GROUP = 3


def _ada_kernel(c_ref, w_ref, b_ref, o_ref):
    s = _silu(c_ref[...]).astype(BF16)
    for t in range(ADA_GROUP):
        cols = slice(t * D, (t + 1) * D)
        o_ref[t] = _dot(s, w_ref[:, cols].astype(BF16)) + b_ref[:, cols]


def _ada(cond8, w_ada, b_ada):
    wide = ADA_GROUP * D
    out = pl.pallas_call(
        _ada_kernel,
        grid=(DEPTH, 6 // ADA_GROUP),
        in_specs=[
            pl.BlockSpec((SUB, D), lambda l, k: (0, 0)),
            pl.BlockSpec((None, D, wide), lambda l, k: (l, 0, k)),
            pl.BlockSpec((None, 1, wide), lambda l, k: (l, 0, k)),
        ],
        out_specs=pl.BlockSpec((ADA_GROUP, SUB, D),
                               lambda l, k: (l * (6 // ADA_GROUP) + k, 0, 0)),
        out_shape=jax.ShapeDtypeStruct((DEPTH * 6, SUB, D), F32),
        compiler_params=_cp(("arbitrary", "arbitrary")),
        name="ada",
    )(cond8, w_ada, b_ada.reshape(DEPTH, 1, 6 * D))
    return out.reshape(DEPTH * 6 * SUB, 1, D)


def _mod_spec(layer, which, tile_of=_joint_tile):
    base = (layer * 6 + which) * SUB
    per_b = LS // TM

    def imap(i):
        t = tile_of(i)
        r = jnp.where(t < N_CTX_TILES, 0, 1 + (t - N_CTX_TILES) // per_b)
        return (base + r, 0, 0)

    return pl.BlockSpec((None, 1, D), imap)


def _hmod(x, nw_ref, sh_ref, sc_ref):
    return (_rms(x, nw_ref[...]) * (1.0 + sc_ref[...]) + sh_ref[...]).astype(BF16)


def _fill_h(h_scr, nw_ref, sh_ref, sc_ref, x_ref, xp_ref, xn_ref):
    h_scr[0:TM, :] = _hmod(x_ref[...], nw_ref, sh_ref, sc_ref)
    near = jnp.concatenate([xp_ref[HB - HH:HB, :], xn_ref[0:HH, :]], axis=0)
    h_scr[TM:TM + 2 * HH, :] = _hmod(near, nw_ref, sh_ref, sc_ref)


def _tile_flags():
    i = pl.program_id(0)
    is_ctx = _is_ctx()
    lseq = jnp.where(is_ctx, LP, LS)
    starts = ((i * TM) & (lseq - 1)) == 0
    ends = (((i + 1) * TM) & (lseq - 1)) == 0
    return is_ctx, starts, ends


def _edge_slabs():
    return sorted({b for b in range(0, TM, LP)} | {b + LP - SLAB for b in range(0, TM, LP)})


def _conv3_bulk(u, w, b):
    return (pltpu.roll(u, 1, axis=0) * w[0:1, :] + u * w[1:2, :]
            + pltpu.roll(u, TM - 1, axis=0) * w[2:3, :] + b)


def _conv3_slab(u_ext, s, w, b, flags):
    is_ctx, starts, ends = flags
    us = u_ext[s:s + SLAB, :]
    if s == 0:
        prev = jnp.where(starts, 0.0, u_ext[TM + HH - 1:TM + HH, :])
    else:
        prev = u_ext[s - 1:s, :]
        if s % LP == 0:
            prev = jnp.where(is_ctx, 0.0, prev)
    if s + SLAB == TM:
        nxt = jnp.where(ends, 0.0, u_ext[TM + HH:TM + HH + 1, :])
    else:
        nxt = u_ext[s + SLAB:s + SLAB + 1, :]
        if (s + SLAB) % LP == 0:
            nxt = jnp.where(is_ctx, 0.0, nxt)
    rows = lax.broadcasted_iota(jnp.int32, us.shape, 0)
    up = jnp.where(rows == 0, prev, pltpu.roll(us, 1, axis=0))
    dn = jnp.where(rows == SLAB - 1, nxt, pltpu.roll(us, SLAB - 1, axis=0))
    return up * w[0:1, :] + us * w[1:2, :] + dn * w[2:3, :] + b


def _tail_kernel(n_a, n_x, has_bias, split_out, *refs):
    a_refs, x_refs = refs[:3 * n_a], refs[3 * n_a:3 * (n_a + n_x)]
    rest = list(refs[3 * (n_a + n_x):])
    w_ref = rest.pop(0)
    b_ref = rest.pop(0) if has_bias else None
    (nw1_ref, g1_ref, nw2_ref, sh_ref, sc_ref, wup_ref, wdw_ref, bdw_ref, wdn_ref, nw3_ref,
     g2_ref) = rest[:11]
    n_out = 2 if split_out else 1
    out_refs = rest[11:11 + n_out]
    a_scr, x_scr, h_scr, act_scr = rest[11 + n_out:]

    _per_part(n_a, functools.partial(_fill_ext, a_scr), a_refs)
    _per_part(n_x, functools.partial(_fill_ext, x_scr), x_refs)
    y = _dot(a_scr[...], w_ref[...])
    if has_bias:
        y = y + b_ref[...]
    x1 = x_scr[...] + g1_ref[...] * _rms(y, nw1_ref[...])
    x_scr[...] = x1
    h_scr[0:TM, :] = _hmod(x1[0:TM, :], nw2_ref, sh_ref, sc_ref)
    h_scr[TM:, :] = _hmod(x1[TM + HB - HH:TM + HB + HH, :], nw2_ref, sh_ref, sc_ref)

    flags = _tile_flags()
    h = h_scr[...]
    for c in range(D_FF // CK):
        cg = slice(c * CK, (c + 1) * CK)
        cv = slice(D_FF + c * CK, D_FF + (c + 1) * CK)
        g_ext = _dot(h, wup_ref[:, cg])
        v_ext = _dot(h, wup_ref[:, cv])
        wg, bg, wv, bv = wdw_ref[:, cg], bdw_ref[:, cg], wdw_ref[:, cv], bdw_ref[:, cv]
        g = _conv3_bulk(g_ext[0:TM, :], wg, bg)
        val = _conv3_bulk(v_ext[0:TM, :], wv, bv)
        act_scr[:, cg] = (_silu(g) * val).astype(BF16)
        for s in _edge_slabs():
            g = _conv3_slab(g_ext, s, wg, bg, flags)
            val = _conv3_slab(v_ext, s, wv, bv, flags)
            act_scr[s:s + SLAB, cg] = (_silu(g) * val).astype(BF16)
    y = _dot(act_scr[...], wdn_ref[...])
    res = x_scr[0:TM, :] + g2_ref[...] * _rms(y, nw3_ref[...])
    if split_out:
        is_ctx = flags[0]

        @pl.when(is_ctx)
        def _():
            out_refs[0][...] = res

        @pl.when(jnp.logical_not(is_ctx))
        def _():
            out_refs[1][...] = res
    else:
        out_refs[0][...] = res


def _tail(a_parts, x_parts, nw_all, mods, layer, j, w_out_bf, b_out, wup_bf, w_dw, b_dw, wdn_bf,
          split_out):
    k = w_out_bf.shape[0]
    has_bias = b_out is not None

    def specs(parts, width):
        if len(parts) == 1:
            return _ext_specs(M, _joint_tile, width)
        return _ext_specs(MP, _ctx_tile, width) + _ext_specs(MS, _lat_tile, width)

    in_specs = specs(a_parts, k) + specs(x_parts, D) + [_lay((k, D), None)]
    args = [p for p in a_parts for _ in range(3)] + [p for p in x_parts for _ in range(3)]
    args.append(w_out_bf)
    if has_bias:
        in_specs.append(_lay((1, D), j))
        args.append(b_out)
    in_specs += [
        _norm_spec(layer, 1), _mod_spec(layer, 2),
        _norm_spec(layer, 2), _mod_spec(layer, 3), _mod_spec(layer, 4),
        _lay((D, 2 * D_FF), None), _lay((3, 2 * D_FF), layer), _lay((1, 2 * D_FF), layer),
        _lay((D_FF, D), None),
        _norm_spec(layer, 3), _mod_spec(layer, 5),
    ]
    args += [nw_all, mods, nw_all, mods, mods, wup_bf, w_dw, b_dw, wdn_bf, nw_all, mods]
    if split_out:
        out_specs = [_tile_spec(D, _ctx_tile), _tile_spec(D, _lat_tile)]
        out_shape = [jax.ShapeDtypeStruct((MP, D), F32), jax.ShapeDtypeStruct((MS, D), F32)]
    else:
        out_specs = _tile_spec(D, _joint_tile)
        out_shape = jax.ShapeDtypeStruct((M, D), F32)
    return pl.pallas_call(
        functools.partial(_tail_kernel, len(a_parts), len(x_parts), has_bias, split_out),
        grid=(M // TM,),
        in_specs=in_specs,
        out_specs=out_specs,
        out_shape=out_shape,
        scratch_shapes=[pltpu.VMEM((TM + HALO, k), BF16), pltpu.VMEM((TM + HALO, D), F32),
                        pltpu.VMEM((TM + 2 * HH, D), BF16), pltpu.VMEM((TM, D_FF), BF16)],
        compiler_params=_cp(("arbitrary",)),
        name="tail",
    )(*args)


def _hy_in_kernel(n_parts, n_cast, *refs):
    x_refs = refs[:3 * n_parts]
    nw_ref, sh_ref, sc_ref, w_ref, b_ref, ws_ref, bs_ref = refs[3 * n_parts:3 * n_parts + 7]
    rest = refs[3 * n_parts + 7:]
    x0_ref, z_ref, h_scr = rest[n_cast], rest[n_cast + 1], rest[-1]
    _cast_blocks(rest[:n_cast], rest[n_cast + 2:-1])
    _per_part(n_parts, functools.partial(_fill_h, h_scr, nw_ref, sh_ref, sc_ref), x_refs)
    flags = _tile_flags()
    h = h_scr[...]
    for c in range(D // CK):
        cc = slice(c * CK, (c + 1) * CK)
        u_ext, ws, bs = [], [], []
        for s in range(3):
            cs = slice(s * D + c * CK, s * D + (c + 1) * CK)
            u_ext.append(_dot(h, w_ref[:, cs]) + b_ref[:, cs])
            ws.append(ws_ref[:, cs])
            bs.append(bs_ref[:, cs])
        out = [_conv3_bulk(u_ext[s][0:TM, :], ws[s], bs[s]) for s in range(3)]
        x0_ref[:, cc] = out[0].astype(BF16)
        z_ref[:, cc] = (out[2] * out[1]).astype(BF16)
        for r in _edge_slabs():
            out = [_conv3_slab(u_ext[s], r, ws[s], bs[s], flags) for s in range(3)]
            x0_ref[r:r + SLAB, cc] = out[0].astype(BF16)
            z_ref[r:r + SLAB, cc] = (out[2] * out[1]).astype(BF16)


def _hy_in(x_parts, nw_all, mods, layer, j, w_bf, b_in, w_short, b_short, cast_jobs):
    assert M // TM == CAST_STEPS
    c_in, c_out, c_shape, c_args = _cast_plumbing(cast_jobs, lambda i: i)
    if len(x_parts) == 1:
        x_specs = _ext_specs(M, _joint_tile)
    else:
        x_specs = _ext_specs(MP, _ctx_tile) + _ext_specs(MS, _lat_tile)
    x_args = [a for a in x_parts for _ in range(3)]
    return pl.pallas_call(
        functools.partial(_hy_in_kernel, len(x_parts), len(c_in)),
        grid=(M // TM,),
        in_specs=x_specs + [
            _norm_spec(layer, 0), _mod_spec(layer, 0), _mod_spec(layer, 1),
            _lay((D, 3 * D), None), _lay((1, 3 * D), j), _lay((3, 3 * D), j), _lay((1, 3 * D), j),
        ] + c_in,
        out_specs=[_tile_spec(D, _joint_tile)] * 2 + c_out,
        out_shape=[jax.ShapeDtypeStruct((M, D), BF16)] * 2 + c_shape,
        scratch_shapes=[pltpu.VMEM((TM + 2 * HH, D), BF16)],
        compiler_params=_cp(("arbitrary",)),
        name="hy_in",
    )(*x_args, nw_all, mods, mods, w_bf, b_in, w_short, b_short, *c_args)


def _dft_mats():
    n = np.arange(CB, dtype=np.float64)
    f = np.arange(CB, dtype=np.float64)
    ang = 2.0 * np.pi * np.outer(f, n) / (2 * CB)
    fwd = np.concatenate([np.cos(ang), -np.sin(ang)], axis=0)
    fwd[CB] = np.cos(np.pi * n)
    scale = np.full((2 * CB, 1), 2.0 / (2 * CB))
    scale[0] = scale[CB] = 1.0 / (2 * CB)
    inv = (fwd * scale).T
    sgn = np.where(np.arange(CB) % 2 == 0, 1.0, -1.0)
    sgn2 = np.concatenate([sgn, sgn])[:, None]
    sgn2[CB] = 1.0
    return (jnp.asarray(fwd, F32).astype(BF16), jnp.asarray(inv, F32).astype(BF16),
            jnp.asarray(sgn2, F32))


def _filter_features(L):
    t = np.linspace(0.0, 1.0, L)[:, None]
    w = 2.0 * np.pi * np.arange(L)[:, None] / L
    bands = np.linspace(1e-4, EMB_BANDS - 1, EMB_BANDS)
    z = np.concatenate([t, np.cos(bands * w), -np.sin(bands * w)], axis=-1)
    return t, z


def _filter_tables():
    zs, ts, ms = [], [], []
    for L in (LP, LS):
        t, z = _filter_features(L)
        z = np.pad(z, ((0, 0), (0, FILTER_HIDDEN - EMB_DIM)))
        idx = np.abs(np.arange(2 * L) - L) % L
        zs.append(z[idx])
        ts.append(t[idx])
        ms.append((np.arange(2 * L) != 0).astype(np.float64)[:, None])
    return tuple(jnp.asarray(np.concatenate(a), F32) for a in (zs, ts, ms))


N_WIN = 1 + 2 * NB_S - 1


def _filt_kernel(z_ref, t_ref, m_ref, w1_ref, b1_ref, fr_ref, w2_ref, b2_ref, w3_ref,
                 ad_ref, f_ref, sg_ref, g_ref, nyq_ref, prev_scr):
    c = pl.program_id(0)

    @pl.when(c == 0)
    def _():
        prev_scr[...] = jnp.zeros_like(prev_scr)

    fr = fr_ref[...]
    hid = jnp.sin(fr * (_dot(z_ref[...].astype(BF16), w1_ref[...].astype(BF16)) + b1_ref[...]))
    hid = jnp.sin(fr * (_dot(hid.astype(BF16), w2_ref[...].astype(BF16)) + b2_ref[...]))
    h = _dot(hid.astype(BF16), w3_ref[...].astype(BF16))
    taps = h * jnp.exp(-t_ref[...] * ad_ref[...]) * m_ref[...]
    a = _dot(f_ref[...], taps.astype(BF16))
    g = a + sg_ref[...] * prev_scr[...]
    prev_scr[...] = a
    g_ref[...] = g
    nyq_ref[...] = g[CB:CB + 1, :]
    g_ref[CB:CB + 1, :] = jnp.zeros((1, D), F32)


def _is_bwd_chunk(c):
    return jnp.logical_or(c == 0, jnp.logical_and(c >= 2, c < 2 + NB_S))


def _window_of_chunk(c):
    return jnp.where(c <= 1, 0, jnp.maximum(c - 2, 1))


def _filter_spectra(tabs, fwd, sgn, j, w1, b1, freq, w2, b2, w3, absd):
    z_tab, t_tab, m_tab = tabs
    small = lambda shape: pl.BlockSpec(shape, lambda c: (0, 0))
    hid = _lay((1, FILTER_HIDDEN), j)
    return pl.pallas_call(
        _filt_kernel,
        grid=(N_CHUNK,),
        in_specs=[
            pl.BlockSpec((CB, FILTER_HIDDEN), lambda c: (c, 0)),
            pl.BlockSpec((CB, 1), lambda c: (c, 0)),
            pl.BlockSpec((CB, 1), lambda c: (c, 0)),
            _lay((FILTER_HIDDEN, FILTER_HIDDEN), j), hid, hid,
            _lay((FILTER_HIDDEN, FILTER_HIDDEN), j), hid,
            pl.BlockSpec((None, FILTER_HIDDEN, D),
                         lambda c: (j, 0, jnp.where(_is_bwd_chunk(c), 1, 0))),
            small((1, D)),
            small((2 * CB, CB)),
            small((2 * CB, 1)),
        ],
        out_specs=[pl.BlockSpec((None, 2 * CB, D), lambda c: (_window_of_chunk(c), 0, 0)),
                   pl.BlockSpec((None, 1, D), lambda c: (_window_of_chunk(c), 0, 0))],
        out_shape=[jax.ShapeDtypeStruct((N_WIN, 2 * CB, D), F32),
                   jax.ShapeDtypeStruct((N_WIN, 1, D), F32)],
        scratch_shapes=[pltpu.VMEM((2 * CB, D), F32)],
        compiler_params=_cp(("arbitrary",)),
        name="hy_filter",
    )(z_tab, t_tab, m_tab, w1, b1, freq, w2, b2, w3, absd, fwd, sgn)


def _conv_kernel(tc, n_cast, z_ref, x0_ref, g_ref, nyq_ref, f_ref, gi_ref, db_ref, *rest):
    o_ref, u_scr = rest[n_cast], rest[-1]
    _cast_blocks(rest[:n_cast], rest[n_cast + 1:-1])
    unit = pl.program_id(1)
    fmat = f_ref[...]
    gmat = gi_ref[...]
    row0 = lax.broadcasted_iota(jnp.int32, (CB, tc), 0) == 0
    db = db_ref[...]

    def emit(blk, yre, yim, ynyq):
        rows = pl.ds(blk * CB, CB)
        yspec = jnp.concatenate([yre, jnp.where(row0, ynyq, yim)], axis=0)
        y = _dot(gmat, yspec.astype(BF16))
        o_ref[rows, :] = (x0_ref[rows, :] * (y + z_ref[rows, :] * db)).astype(BF16)

    @pl.when(unit < MP // LS)
    def _():
        gre, gim, gnyq = g_ref[0, :CB, :], g_ref[0, CB:, :], nyq_ref[0]
        for s in range(LS // LP):
            u = _dot(fmat, z_ref[pl.ds(s * CB, CB), :])
            ure, unyq = u[:CB], u[CB:CB + 1]
            uim = jnp.where(row0, 0.0, u[CB:])
            emit(s, gre * ure - gim * uim, gre * uim + gim * ure, gnyq * unyq)

    @pl.when(unit >= MP // LS)
    def _():
        unyq = []
        for j in range(NB_S):
            u = _dot(fmat, z_ref[pl.ds(j * CB, CB), :])
            unyq.append(u[CB:CB + 1])
            u_scr[j] = u
            u_scr[j, CB:CB + 1, :] = jnp.zeros((1, tc), F32)
        for i in range(NB_S):
            win = [NB_S + i - j for j in range(NB_S)]
            parts_re, parts_im = [], []
            for r in range(0, CB, MAC_ROWS):
                re, im = slice(r, r + MAC_ROWS), slice(CB + r, CB + r + MAC_ROWS)
                yre = yim = None
                for j, w in enumerate(win):
                    gre, gim = g_ref[w, re, :], g_ref[w, im, :]
                    ure, uim = u_scr[j, re, :], u_scr[j, im, :]
                    pre, pim = gre * ure - gim * uim, gre * uim + gim * ure
                    yre = pre if yre is None else yre + pre
                    yim = pim if yim is None else yim + pim
                parts_re.append(yre)
                parts_im.append(yim)
            ynyq = nyq_ref[win[0]] * unyq[0]
            for j in range(1, NB_S):
                ynyq = ynyq + nyq_ref[win[j]] * unyq[j]
            emit(i, jnp.concatenate(parts_re, axis=0), jnp.concatenate(parts_im, axis=0), ynyq)


def _long_conv(z, x0, spectra, fwd, inv, d_bias, j, cast_jobs, tc=256):
    g_win, g_nyq = spectra
    units = M // LS
    assert (D // tc) * units == CAST_STEPS
    c_in, c_out, c_shape, c_args = _cast_plumbing(cast_jobs, lambda c, u: c * units + u)
    return pl.pallas_call(
        functools.partial(_conv_kernel, tc, len(c_in)),
        grid=(D // tc, units),
        in_specs=[
            pl.BlockSpec((LS, tc), lambda c, u: (u, c)),
            pl.BlockSpec((LS, tc), lambda c, u: (u, c)),
            pl.BlockSpec((N_WIN, 2 * CB, tc), lambda c, u: (0, 0, c)),
            pl.BlockSpec((N_WIN, 1, tc), lambda c, u: (0, 0, c)),
            pl.BlockSpec((2 * CB, CB), lambda c, u: (0, 0)),
            pl.BlockSpec((CB, 2 * CB), lambda c, u: (0, 0)),
            pl.BlockSpec((None, 1, tc), lambda c, u: (j, 0, c)),
        ] + c_in,
        out_specs=[pl.BlockSpec((LS, tc), lambda c, u: (u, c))] + c_out,
        out_shape=[jax.ShapeDtypeStruct((M, D), BF16)] + c_shape,
        scratch_shapes=[pltpu.VMEM((NB_S, 2 * CB, tc), F32)],
        compiler_params=_cp(("arbitrary", "arbitrary")),
        name="hy_conv",
    )(z, x0, g_win, g_nyq, fwd, inv, d_bias, *c_args)


def _rope_tables(L):
    rows = L // GRID_W
    r = np.repeat(np.arange(rows, dtype=np.float64), GRID_W)
    cidx = np.tile(np.arange(GRID_W, dtype=np.float64), rows)
    inv = ROPE_THETA ** (-np.arange(ROT_FREQS, dtype=np.float64) / ROT_FREQS)
    ar = r[:, None] * inv
    ac = cidx[:, None] * inv
    cos = np.concatenate([np.cos(ar), np.cos(ar), np.cos(ac), np.cos(ac)] * 2, axis=-1)
    sin = np.concatenate([np.sin(ar), np.sin(ar), np.sin(ac), np.sin(ac)] * 2, axis=-1)
    first_half = (np.arange(V_DIM) % (2 * ROT_FREQS)) < ROT_FREQS
    sin_a = np.where(first_half, -sin, 0.0)
    sin_b = np.where(first_half, 0.0, sin)
    return tuple(jnp.asarray(a, F32) for a in (cos, sin_a, sin_b))


def _rope(x, cos, sin_a, sin_b):
    return (x * cos + pltpu.roll(x, V_DIM - ROT_FREQS, axis=1) * sin_a
            + pltpu.roll(x, ROT_FREQS, axis=1) * sin_b)


HEADS_PER_CHUNK = CK // V_DIM


def _qkv_chunks():
    per_part = D // CK
    return [(slice(c * CK, (c + 1) * CK), c // per_part, (c % per_part) * HEADS_PER_CHUNK)
            for c in range(3 * per_part)]


def _qkv_c_kernel(first, x_ref, nw_ref, sh_ref, sc_ref, w_ref, *rest):
    qkv_ref, nk_ref, nv_ref = rest[-3:]
    if first:
        for ref in (nk_ref, nv_ref):
            ref[:, 1:] = jnp.zeros((TM // LP, N_ATTN - 1, N_HEADS, LP, V_DIM), F32)
    h = _hmod(x_ref[...], nw_ref, sh_ref, sc_ref)
    for cs, part, head0 in _qkv_chunks():
        u = _dot(h, w_ref[:, cs])
        qkv_ref[:, cs] = (u * QSCALE if part == 0 else u).astype(BF16)
        if part > 0:
            cache = nk_ref if part == 1 else nv_ref
            for s in range(TM // LP):
                for hh in range(HEADS_PER_CHUNK):
                    blk = u[s * LP:(s + 1) * LP, hh * V_DIM:(hh + 1) * V_DIM]
                    if first:
                        cache[s, 0, head0 + hh] = blk
                    else:
                        cache[s, head0 + hh] = blk


def _qkv_ctx(x, nw_all, mods, layer, j, w_bf, prev):
    seqs = TM // LP
    in_specs = [_tile_spec(D, _joint_tile), _norm_spec(layer, 0), _mod_spec(layer, 0),
                _mod_spec(layer, 1), _lay((D, 3 * D), None)]
    args = [x, nw_all, mods, mods, w_bf]
    aliases = {}
    if prev is None:
        assert j == 0
        cache = pl.BlockSpec((seqs, N_ATTN, N_HEADS, LP, V_DIM), lambda i: (i, 0, 0, 0, 0))
    else:
        in_specs += [pl.BlockSpec(memory_space=pl.ANY)] * 2
        args += list(prev)
        aliases = {len(args) - 2: 1, len(args) - 1: 2}
        cache = pl.BlockSpec((seqs, None, N_HEADS, LP, V_DIM), lambda i: (i, j, 0, 0, 0))
    cache_shape = jax.ShapeDtypeStruct((BP, N_ATTN, N_HEADS, LP, V_DIM), F32)
    return pl.pallas_call(
        functools.partial(_qkv_c_kernel, prev is None),
        grid=(N_CTX_TILES,),
        in_specs=in_specs,
        out_specs=[_tile_spec(3 * D, _joint_tile), cache, cache],
        out_shape=[jax.ShapeDtypeStruct((MP, 3 * D), BF16), cache_shape, cache_shape],
        input_output_aliases=aliases,
        compiler_params=_cp(("arbitrary",)),
        name="qkv_ctx",
    )(*args)


def _qkv_l_kernel(x_ref, nw_ref, sh_ref, sc_ref, w_ref, cos_ref, sa_ref, sb_ref, qkv_ref):
    h = _hmod(x_ref[...], nw_ref, sh_ref, sc_ref)
    for cs, part, _ in _qkv_chunks():
        u = _dot(h, w_ref[:, cs])
        if part == 2:
            qkv_ref[:, cs] = u.astype(BF16)
            continue
        for hh in range(HEADS_PER_CHUNK):
            r = _rope(u[:, hh * V_DIM:(hh + 1) * V_DIM], cos_ref[...], sa_ref[...], sb_ref[...])
            if part == 0:
                r = r * QSCALE
            qkv_ref[:, cs.start + hh * V_DIM:cs.start + (hh + 1) * V_DIM] = r.astype(BF16)


def _qkv_lat(x, nw_all, mods, layer, j, w_bf, ropes):
    tile = lambda i: i + N_CTX_TILES
    tab = pl.BlockSpec((TM, V_DIM), lambda i: (i % (LS // TM), 0))
    return pl.pallas_call(
        _qkv_l_kernel,
        grid=(N_LAT_TILES,),
        in_specs=[_tile_spec(D, tile), _norm_spec(layer, 0), _mod_spec(layer, 0, tile),
                  _mod_spec(layer, 1, tile), _lay((D, 3 * D), None), tab, tab, tab],
        out_specs=_tile_spec(3 * D, _joint_tile),
        out_shape=jax.ShapeDtypeStruct((MS, 3 * D), BF16),
        compiler_params=_cp(("arbitrary",)),
        name="qkv_lat",
    )(x, nw_all, mods, mods, w_bf, *ropes)


def _lambda(lv, lam_init):
    a = jnp.exp(jnp.sum(lv[0:1, :] * lv[1:2, :], axis=-1, keepdims=True))
    b = jnp.exp(jnp.sum(lv[2:3, :] * lv[3:4, :], axis=-1, keepdims=True))
    return a - b + lam_init


def _diff_attn(q, chunks, lam, lam_init, subln):
    t = q.shape[0]
    lane = lax.broadcasted_iota(jnp.int32, q.shape, 1)
    zero = jnp.zeros_like(q)
    q2 = jnp.concatenate([jnp.where(lane < HEAD_DIM, q, zero),
                          jnp.where(lane < HEAD_DIM, zero, q)], axis=0)
    m = l = acc = None
    for k, v in chunks:
        s = lax.dot_general(q2, k, (((1,), (1,)), ((), ())), preferred_element_type=F32)
        mc = jnp.max(s, axis=-1, keepdims=True)
        m_new = mc if m is None else jnp.maximum(m, mc)
        e = jnp.exp2(s - m_new)
        lc = jnp.sum(e, axis=-1, keepdims=True)
        pv = _dot(e.astype(BF16), v)
        if m is None:
            l, acc = lc, pv
        else:
            alpha = jnp.exp2(m - m_new)
            l = alpha * l + lc
            acc = alpha * acc + pv
        m = m_new
    o = acc[:t] * (1.0 / l[:t]) - acc[t:] * (lam / l[t:])
    return _rms(o, subln) * (1.0 - lam_init)


def _attn_c_kernel(lam_init, q_ref, k_ref, v_ref, lv_ref, sub_ref, o_ref):
    lam = _lambda(lv_ref[...], lam_init)
    t = LP
    probs = [(slice(s * LP, (s + 1) * LP), slice(h * V_DIM, (h + 1) * V_DIM))
             for s in range(CTX_SEQS) for h in range(N_HEADS)]
    lane = lax.broadcasted_iota(jnp.int32, (t, V_DIM), 1)
    zero = jnp.zeros((t, V_DIM), BF16)
    dn = (((1,), (1,)), ((), ()))
    s = []
    for rs, hs in probs:
        q = q_ref[rs, hs]
        q2 = jnp.concatenate([jnp.where(lane < HEAD_DIM, q, zero),
                              jnp.where(lane < HEAD_DIM, zero, q)], axis=0)
        s.append(lax.dot_general(q2, k_ref[rs, hs], dn, preferred_element_type=F32))
    e = [jnp.exp2(x - jnp.max(x, axis=-1, keepdims=True)) for x in s]
    l = [jnp.sum(x, axis=-1, keepdims=True) for x in e]
    pv = [_dot(x.astype(BF16), v_ref[rs, hs]) for x, (rs, hs) in zip(e, probs)]
    for (rs, hs), acc, lh in zip(probs, pv, l):
        o = acc[:t] * (1.0 / lh[:t]) - acc[t:] * (lam / lh[t:])
        o_ref[rs, hs] = (_rms(o, sub_ref[...]) * (1.0 - lam_init)).astype(BF16)


def _attn_ctx(qkv_c, lamv, subln, j, lam_init):
    part = lambda p: pl.BlockSpec((CTX_SEQS * LP, D), lambda b: (b, p))
    return pl.pallas_call(
        functools.partial(_attn_c_kernel, lam_init),
        grid=(BP // CTX_SEQS,),
        in_specs=[part(0), part(1), part(2),
                  pl.BlockSpec((None, 4, HEAD_DIM), lambda b: (j, 0, 0)),
                  pl.BlockSpec((None, 1, V_DIM), lambda b: (j, 0, 0))],
        out_specs=pl.BlockSpec((CTX_SEQS * LP, D), lambda b: (b, 0)),
        out_shape=jax.ShapeDtypeStruct((MP, D), BF16),
        compiler_params=_cp(("arbitrary",)),
        name="attn_ctx",
    )(qkv_c, qkv_c, qkv_c, lamv, subln)


def _attn_l_kernel(lam_init, n_cast, q_ref, k_ref, v_ref, ck_ref, cv_ref, lv_ref, sub_ref, *rest):
    o_ref = rest[n_cast]
    _cast_blocks(rest[:n_cast], rest[n_cast + 1:])
    lam = _lambda(lv_ref[...], lam_init)
    for h in range(ATT_HEADS):
        hs = slice(h * V_DIM, (h + 1) * V_DIM)
        chunks = [(ck_ref[h].astype(BF16), cv_ref[h].astype(BF16))]
        for c in range(LS // KEY_CHUNK):
            rows = pl.ds(c * KEY_CHUNK, KEY_CHUNK)
            chunks.append((k_ref[rows, hs], v_ref[rows, hs]))
        o = _diff_attn(q_ref[:, hs], chunks, lam, lam_init, sub_ref[...])
        o_ref[:, hs] = o.astype(BF16)


def _attn_lat(qkv_l, cache_k, cache_v, lamv, subln, j, lam_init, cast_jobs, tq=512):
    nq = LS // tq
    width = ATT_HEADS * V_DIM
    groups = N_HEADS // ATT_HEADS
    assert BS * groups * nq == CAST_STEPS
    c_in, c_out, c_shape, c_args = _cast_plumbing(
        cast_jobs, lambda b, g, q: (b * groups + g) * nq + q)
    seq = lambda part: pl.BlockSpec((LS, width), lambda b, g, q: (b, part * groups + g))
    ctx = pl.BlockSpec((None, None, ATT_HEADS, PAST, V_DIM), lambda b, g, q: (b, j, g, 0, 0))
    return pl.pallas_call(
        functools.partial(_attn_l_kernel, lam_init, len(c_in)),
        grid=(BS, groups, nq),
        in_specs=[pl.BlockSpec((tq, width), lambda b, g, q: (b * nq + q, g)),
                  seq(1), seq(2), ctx, ctx,
                  pl.BlockSpec((None, 4, HEAD_DIM), lambda b, g, q: (j, 0, 0)),
                  pl.BlockSpec((None, 1, V_DIM), lambda b, g, q: (j, 0, 0))] + c_in,
        out_specs=[pl.BlockSpec((tq, width), lambda b, g, q: (b * nq + q, g))] + c_out,
        out_shape=[jax.ShapeDtypeStruct((MS, D), BF16)] + c_shape,
        compiler_params=_cp(("arbitrary", "arbitrary", "arbitrary")),
        name="attn_lat",
    )(qkv_l, qkv_l, qkv_l, cache_k, cache_v, lamv, subln, *c_args)


def kernel(x_prompt, x_sample, cache_k, cache_v, c, c_ctx, w_ada, b_ada, norm_w, hy_w_in, hy_b_in, hy_w_short, hy_b_short, hy_f_w1, hy_f_b1, hy_f_freq, hy_f_w2, hy_f_b2, hy_f_w3, hy_d_bias, hy_w_out, hy_b_out, at_w_qkv, at_w_out, at_lambda_q1, at_lambda_k1, at_lambda_q2, at_lambda_k2, at_subln, ffn_w_up, ffn_w_dw, ffn_b_dw, ffn_w_down):
    cond8 = jnp.concatenate([c_ctx[None, :], c, jnp.zeros((SUB - 1 - BS, D), F32)], axis=0)
    mods = _ada(cond8, w_ada, b_ada)

    fwd, inv, sgn = _dft_mats()
    tabs = _filter_tables()
    min_decay = math.log(DECAY_TARGET) / DECAY_PCT_LONG
    max_decay = math.log(DECAY_TARGET) / DECAY_PCT_SHORT
    absd = jnp.asarray(np.abs(np.linspace(min_decay, max_decay, D))[None, :], F32)
    ropes = _rope_tables(LS)

    row = lambda a: a.reshape(a.shape[0], 1, a.shape[1])
    nw_all = norm_w.reshape(DEPTH * 4, 1, D)
    w1_pad = jnp.pad(hy_f_w1, ((0, 0), (0, FILTER_HIDDEN - EMB_DIM), (0, 0)))
    lamv = jnp.stack([at_lambda_q1, at_lambda_k1, at_lambda_q2, at_lambda_k2], axis=1)
    subln = row(at_subln)
    ffn = lambda l: [(ffn_w_up, l), (ffn_w_down, l)]
    w_in_bf = hy_w_in[0].astype(BF16)

    x_parts = [x_prompt.reshape(MP, D), x_sample.reshape(MS, D)]
    caches = None
    for i in range(DEPTH):
        j = i // 2
        last = i == DEPTH - 1
        if i % 2 == 0:
            own = ffn(0) + [(hy_w_out, 0)] if i == 0 else []
            x0, z, *cast = _hy_in(x_parts, nw_all, mods, i, j, w_in_bf, row(hy_b_in), hy_w_short,
                                  row(hy_b_short),
                                  own + ffn(i + 1) + [(at_w_qkv, j), (at_w_out, j)])
            if i == 0:
                w_up_bf, w_down_bf, w_out_bf = cast[:3]
            nxt = cast[len(own):]
            spectra = _filter_spectra(tabs, fwd, sgn, j, w1_pad, row(hy_f_b1), row(hy_f_freq),
                                      hy_f_w2, row(hy_f_b2), hy_f_w3, absd)
            a, = _long_conv(z, x0, spectra, fwd, inv, row(hy_d_bias), j, [])
            a_parts, b_out = [a], row(hy_b_out)
        else:
            lam_init = 0.8 - 0.6 * math.exp(-0.3 * i)
            x = x_parts[0]
            qkv_c, new_k, new_v = _qkv_ctx(x, nw_all, mods, i, j, w_qkv_bf, caches)
            caches = (new_k, new_v)
            qkv_l = _qkv_lat(x, nw_all, mods, i, j, w_qkv_bf, ropes)
            jobs = [] if last else ffn(i + 1) + [(hy_w_in, j + 1), (hy_w_out, j + 1)]
            o_lat, *nxt = _attn_lat(qkv_l, cache_k, cache_v, lamv, subln, j, lam_init, jobs)
            a_parts, b_out = [_attn_ctx(qkv_c, lamv, subln, j, lam_init), o_lat], None
        y = _tail(a_parts, x_parts, nw_all, mods, i, j, w_out_bf, b_out, w_up_bf, ffn_w_dw,
                  row(ffn_b_dw), w_down_bf, split_out=last)
        x_parts = list(y) if last else [y]
        if not last:
            w_up_bf, w_down_bf, w_next_in, w_out_bf = nxt
            if i % 2 == 0:
                w_qkv_bf = w_next_in
            else:
                w_in_bf = w_next_in

    return (x_parts[0].reshape(BP, LP, D), x_parts[1].reshape(BS, LS, D), caches[0], caches[1])
```

```python
import functools
import math

import numpy as np
import jax
import jax.numpy as jnp
from jax import lax
from jax.experimental import pallas as pl
from jax.experimental.pallas import tpu as pltpu

D = 1024
BP, LP = 16, 256
BS, LS = 2, 2048
MP = BP * LP
MS = BS * LS
M = MP + MS
DEPTH = 4
N_ATTN = DEPTH // 2
GRID_W = 64
N_HEADS = 8
HEAD_DIM = 64
V_DIM = 128
ROPE_THETA = 10000.0
ROT_FREQS = 16
EMB_BANDS = 16
EMB_DIM = 33
FILTER_HIDDEN = 64
DECAY_TARGET = 1e-2
DECAY_PCT_SHORT = 0.3
DECAY_PCT_LONG = 1.5
D_FF = 2816
EPS = 1e-6
PAST = 256

CB = 256
NB_S = LS // CB
N_CHUNK = 2 + 2 * NB_S
SUB = 8
HB = 16
HALO = 2 * HB
HH = SUB
SLAB = 16
CK = 256
TM = 512
KEY_CHUNK = 2048
ATT_HEADS = 4
CTX_SEQS = 4
MAC_ROWS = 32
VMEM_LIMIT = 56 * 1024 * 1024
QSCALE = HEAD_DIM ** -0.5 * math.log2(math.e)

F32 = jnp.float32
BF16 = jnp.bfloat16


def _dot(a, b):
    return jnp.dot(a, b, preferred_element_type=F32)


def _rms(x, w):
    ms = jnp.mean(x * x, axis=-1, keepdims=True)
    return x * lax.rsqrt(ms + EPS) * w


def _silu(x):
    return x / (1.0 + jnp.exp(-x))


def _cp(sem, vmem=VMEM_LIMIT, fuse=None):
    return pltpu.CompilerParams(dimension_semantics=sem, vmem_limit_bytes=vmem,
                                allow_input_fusion=fuse)


def _lay(shape, idx):
    if idx is None:
        return pl.BlockSpec(tuple(shape), lambda i: (0,) * len(shape),
                            pipeline_mode=pl.Buffered(1))
    return pl.BlockSpec((None,) + tuple(shape), lambda i: (idx,) + (0,) * len(shape),
                        pipeline_mode=pl.Buffered(1))


CAST_STEPS = 16


def _cast_plumbing(jobs, step_of):
    in_specs, out_specs, out_shape, args = [], [], [], []
    for w, layer in jobs:
        _, rows, cols = w.shape
        blk = rows // CAST_STEPS
        in_specs.append(pl.BlockSpec((None, blk, cols),
                                     lambda *g, layer=layer: (layer, step_of(*g), 0)))
        out_specs.append(pl.BlockSpec((blk, cols), lambda *g: (step_of(*g), 0)))
        out_shape.append(jax.ShapeDtypeStruct((rows, cols), BF16))
        args.append(w)
    return in_specs, out_specs, out_shape, args


def _cast_blocks(in_refs, out_refs):
    for i_ref, o_ref in zip(in_refs, out_refs):
        o_ref[...] = i_ref[...].astype(BF16)


def _norm_spec(layer, k):
    return _lay((1, D), layer * 4 + k)


N_CTX_TILES = MP // TM
N_LAT_TILES = MS // TM


def _ctx_tile(i):
    return jnp.minimum(i, N_CTX_TILES - 1)


def _lat_tile(i):
    return jnp.maximum(i - N_CTX_TILES, 0)


def _joint_tile(i):
    return i


def _is_ctx():
    return pl.program_id(0) < N_CTX_TILES


def _tile_spec(width, tile_of):
    return pl.BlockSpec((TM, width), lambda i: (tile_of(i), 0))


def _ext_specs(n_rows, tile_of, width=D):
    r = TM // HB
    last_blk = n_rows // HB - 1
    return [
        pl.BlockSpec((TM, width), lambda i: (tile_of(i), 0)),
        pl.BlockSpec((HB, width), lambda i: (jnp.maximum(tile_of(i) * r - 1, 0), 0)),
        pl.BlockSpec((HB, width), lambda i: (jnp.minimum((tile_of(i) + 1) * r, last_blk), 0)),
    ]


def _fill_ext(scr, t_ref, p_ref, n_ref):
    scr[0:TM, :] = t_ref[...]
    scr[TM:TM + HB, :] = p_ref[...]
    scr[TM + HB:TM + HALO, :] = n_ref[...]


def _per_part(n_parts, fn, refs):
    if n_parts == 1:
        fn(*refs)
        return
    k = len(refs) // 2
    pl.when(_is_ctx())(lambda: fn(*refs[:k]))
    pl.when(jnp.logical_not(_is_ctx()))(lambda: fn(*refs[k:]))


ADA_GROUP = 3


def _ada_kernel(c_ref, w_ref, b_ref, o_ref):
    s = _silu(c_ref[...]).astype(BF16)
    for t in range(ADA_GROUP):
        cols = slice(t * D, (t + 1) * D)
        o_ref[t] = _dot(s, w_ref[:, cols].astype(BF16)) + b_ref[:, cols]


def _ada(cond8, w_ada, b_ada):
    wide = ADA_GROUP * D
    out = pl.pallas_call(
        _ada_kernel,
        grid=(DEPTH, 6 // ADA_GROUP),
        in_specs=[
            pl.BlockSpec((SUB, D), lambda l, k: (0, 0)),
            pl.BlockSpec((None, D, wide), lambda l, k: (l, 0, k)),
            pl.BlockSpec((None, 1, wide), lambda l, k: (l, 0, k)),
        ],
        out_specs=pl.BlockSpec((ADA_GROUP, SUB, D),
                               lambda l, k: (l * (6 // ADA_GROUP) + k, 0, 0)),
        out_shape=jax.ShapeDtypeStruct((DEPTH * 6, SUB, D), F32),
        compiler_params=_cp(("arbitrary", "arbitrary")),
        name="ada",
    )(cond8, w_ada, b_ada.reshape(DEPTH, 1, 6 * D))
    return out.reshape(DEPTH * 6 * SUB, 1, D)


def _mod_spec(layer, which, tile_of=_joint_tile):
    base = (layer * 6 + which) * SUB
    per_b = LS // TM

    def imap(i):
        t = tile_of(i)
        r = jnp.where(t < N_CTX_TILES, 0, 1 + (t - N_CTX_TILES) // per_b)
        return (base + r, 0, 0)

    return pl.BlockSpec((None, 1, D), imap)


def _hmod(x, nw_ref, sh_ref, sc_ref):
    return (_rms(x, nw_ref[...]) * (1.0 + sc_ref[...]) + sh_ref[...]).astype(BF16)


def _fill_h(h_scr, nw_ref, sh_ref, sc_ref, x_ref, xp_ref, xn_ref):
    h_scr[0:TM, :] = _hmod(x_ref[...], nw_ref, sh_ref, sc_ref)
    near = jnp.concatenate([xp_ref[HB - HH:HB, :], xn_ref[0:HH, :]], axis=0)
    h_scr[TM:TM + 2 * HH, :] = _hmod(near, nw_ref, sh_ref, sc_ref)


def _tile_flags():
    i = pl.program_id(0)
    is_ctx = _is_ctx()
    lseq = jnp.where(is_ctx, LP, LS)
    starts = ((i * TM) & (lseq - 1)) == 0
    ends = (((i + 1) * TM) & (lseq - 1)) == 0
    return is_ctx, starts, ends


def _edge_slabs():
    return sorted({b for b in range(0, TM, LP)} | {b + LP - SLAB for b in range(0, TM, LP)})


def _conv3_bulk(u, w, b):
    return (pltpu.roll(u, 1, axis=0) * w[0:1, :] + u * w[1:2, :]
            + pltpu.roll(u, TM - 1, axis=0) * w[2:3, :] + b)


def _conv3_slab(u_ext, s, w, b, flags):
    is_ctx, starts, ends = flags
    us = u_ext[s:s + SLAB, :]
    if s == 0:
        prev = jnp.where(starts, 0.0, u_ext[TM + HH - 1:TM + HH, :])
    else:
        prev = u_ext[s - 1:s, :]
        if s % LP == 0:
            prev = jnp.where(is_ctx, 0.0, prev)
    if s + SLAB == TM:
        nxt = jnp.where(ends, 0.0, u_ext[TM + HH:TM + HH + 1, :])
    else:
        nxt = u_ext[s + SLAB:s + SLAB + 1, :]
        if (s + SLAB) % LP == 0:
            nxt = jnp.where(is_ctx, 0.0, nxt)
    rows = lax.broadcasted_iota(jnp.int32, us.shape, 0)
    up = jnp.where(rows == 0, prev, pltpu.roll(us, 1, axis=0))
    dn = jnp.where(rows == SLAB - 1, nxt, pltpu.roll(us, SLAB - 1, axis=0))
    return up * w[0:1, :] + us * w[1:2, :] + dn * w[2:3, :] + b


def _tail_kernel(n_a, n_x, has_bias, split_out, *refs):
    a_refs, x_refs = refs[:3 * n_a], refs[3 * n_a:3 * (n_a + n_x)]
    rest = list(refs[3 * (n_a + n_x):])
    w_ref = rest.pop(0)
    b_ref = rest.pop(0) if has_bias else None
    (nw1_ref, g1_ref, nw2_ref, sh_ref, sc_ref, wup_ref, wdw_ref, bdw_ref, wdn_ref, nw3_ref,
     g2_ref) = rest[:11]
    n_out = 2 if split_out else 1
    out_refs = rest[11:11 + n_out]
    a_scr, x_scr, h_scr, act_scr = rest[11 + n_out:]

    _per_part(n_a, functools.partial(_fill_ext, a_scr), a_refs)
    _per_part(n_x, functools.partial(_fill_ext, x_scr), x_refs)
    y = _dot(a_scr[...], w_ref[...])
    if has_bias:
        y = y + b_ref[...]
    x1 = x_scr[...] + g1_ref[...] * _rms(y, nw1_ref[...])
    x_scr[...] = x1
    h_scr[0:TM, :] = _hmod(x1[0:TM, :], nw2_ref, sh_ref, sc_ref)
    h_scr[TM:, :] = _hmod(x1[TM + HB - HH:TM + HB + HH, :], nw2_ref, sh_ref, sc_ref)

    flags = _tile_flags()
    h = h_scr[...]
    for c in range(D_FF // CK):
        cg = slice(c * CK, (c + 1) * CK)
        cv = slice(D_FF + c * CK, D_FF + (c + 1) * CK)
        g_ext = _dot(h, wup_ref[:, cg])
        v_ext = _dot(h, wup_ref[:, cv])
        wg, bg, wv, bv = wdw_ref[:, cg], bdw_ref[:, cg], wdw_ref[:, cv], bdw_ref[:, cv]
        g = _conv3_bulk(g_ext[0:TM, :], wg, bg)
        val = _conv3_bulk(v_ext[0:TM, :], wv, bv)
        act_scr[:, cg] = (_silu(g) * val).astype(BF16)
        for s in _edge_slabs():
            g = _conv3_slab(g_ext, s, wg, bg, flags)
            val = _conv3_slab(v_ext, s, wv, bv, flags)
            act_scr[s:s + SLAB, cg] = (_silu(g) * val).astype(BF16)
    y = _dot(act_scr[...], wdn_ref[...])
    res = x_scr[0:TM, :] + g2_ref[...] * _rms(y, nw3_ref[...])
    if split_out:
        is_ctx = flags[0]

        @pl.when(is_ctx)
        def _():
            out_refs[0][...] = res

        @pl.when(jnp.logical_not(is_ctx))
        def _():
            out_refs[1][...] = res
    else:
        out_refs[0][...] = res


def _tail(a_parts, x_parts, nw_all, mods, layer, j, w_out_bf, b_out, wup_bf, w_dw, b_dw, wdn_bf,
          split_out):
    k = w_out_bf.shape[0]
    has_bias = b_out is not None

    def specs(parts, width):
        if len(parts) == 1:
            return _ext_specs(M, _joint_tile, width)
        return _ext_specs(MP, _ctx_tile, width) + _ext_specs(MS, _lat_tile, width)

    in_specs = specs(a_parts, k) + specs(x_parts, D) + [_lay((k, D), None)]
    args = [p for p in a_parts for _ in range(3)] + [p for p in x_parts for _ in range(3)]
    args.append(w_out_bf)
    if has_bias:
        in_specs.append(_lay((1, D), j))
        args.append(b_out)
    in_specs += [
        _norm_spec(layer, 1), _mod_spec(layer, 2),
        _norm_spec(layer, 2), _mod_spec(layer, 3), _mod_spec(layer, 4),
        _lay((D, 2 * D_FF), None), _lay((3, 2 * D_FF), layer), _lay((1, 2 * D_FF), layer),
        _lay((D_FF, D), None),
        _norm_spec(layer, 3), _mod_spec(layer, 5),
    ]
    args += [nw_all, mods, nw_all, mods, mods, wup_bf, w_dw, b_dw, wdn_bf, nw_all, mods]
    if split_out:
        out_specs = [_tile_spec(D, _ctx_tile), _tile_spec(D, _lat_tile)]
        out_shape = [jax.ShapeDtypeStruct((MP, D), F32), jax.ShapeDtypeStruct((MS, D), F32)]
    else:
        out_specs = _tile_spec(D, _joint_tile)
        out_shape = jax.ShapeDtypeStruct((M, D), F32)
    return pl.pallas_call(
        functools.partial(_tail_kernel, len(a_parts), len(x_parts), has_bias, split_out),
        grid=(M // TM,),
        in_specs=in_specs,
        out_specs=out_specs,
        out_shape=out_shape,
        scratch_shapes=[pltpu.VMEM((TM + HALO, k), BF16), pltpu.VMEM((TM + HALO, D), F32),
                        pltpu.VMEM((TM + 2 * HH, D), BF16), pltpu.VMEM((TM, D_FF), BF16)],
        compiler_params=_cp(("arbitrary",)),
        name="tail",
    )(*args)


def _hy_in_kernel(n_parts, n_cast, *refs):
    x_refs = refs[:3 * n_parts]
    nw_ref, sh_ref, sc_ref, w_ref, b_ref, ws_ref, bs_ref = refs[3 * n_parts:3 * n_parts + 7]
    rest = refs[3 * n_parts + 7:]
    x0_ref, z_ref, h_scr = rest[n_cast], rest[n_cast + 1], rest[-1]
    _cast_blocks(rest[:n_cast], rest[n_cast + 2:-1])
    _per_part(n_parts, functools.partial(_fill_h, h_scr, nw_ref, sh_ref, sc_ref), x_refs)
    flags = _tile_flags()
    h = h_scr[...]
    for c in range(D // CK):
        cc = slice(c * CK, (c + 1) * CK)
        u_ext, ws, bs = [], [], []
        for s in range(3):
            cs = slice(s * D + c * CK, s * D + (c + 1) * CK)
            u_ext.append(_dot(h, w_ref[:, cs]) + b_ref[:, cs])
            ws.append(ws_ref[:, cs])
            bs.append(bs_ref[:, cs])
        out = [_conv3_bulk(u_ext[s][0:TM, :], ws[s], bs[s]) for s in range(3)]
        x0_ref[:, cc] = out[0].astype(BF16)
        z_ref[:, cc] = (out[2] * out[1]).astype(BF16)
        for r in _edge_slabs():
            out = [_conv3_slab(u_ext[s], r, ws[s], bs[s], flags) for s in range(3)]
            x0_ref[r:r + SLAB, cc] = out[0].astype(BF16)
            z_ref[r:r + SLAB, cc] = (out[2] * out[1]).astype(BF16)


def _hy_in(x_parts, nw_all, mods, layer, j, w_bf, b_in, w_short, b_short, cast_jobs):
    assert M // TM == CAST_STEPS
    c_in, c_out, c_shape, c_args = _cast_plumbing(cast_jobs, lambda i: i)
    if len(x_parts) == 1:
        x_specs = _ext_specs(M, _joint_tile)
    else:
        x_specs = _ext_specs(MP, _ctx_tile) + _ext_specs(MS, _lat_tile)
    x_args = [a for a in x_parts for _ in range(3)]
    n_in = len(x_args) + 7 + len(c_args)
    return pl.pallas_call(
        functools.partial(_hy_in_kernel, len(x_parts), len(c_in)),
        grid=(M // TM,),
        in_specs=x_specs + [
            _norm_spec(layer, 0), _mod_spec(layer, 0), _mod_spec(layer, 1),
            _lay((D, 3 * D), None), _lay((1, 3 * D), j), _lay((3, 3 * D), j), _lay((1, 3 * D), j),
        ] + c_in,
        out_specs=[_tile_spec(D, _joint_tile)] * 2 + c_out,
        out_shape=[jax.ShapeDtypeStruct((M, D), BF16)] * 2 + c_shape,
        scratch_shapes=[pltpu.VMEM((TM + 2 * HH, D), BF16)],
        compiler_params=_cp(("arbitrary",), fuse=[k == len(x_args) + 3 for k in range(n_in)]),
        name="hy_in",
    )(*x_args, nw_all, mods, mods, w_bf, b_in, w_short, b_short, *c_args)


def _dft_mats():
    n = np.arange(CB, dtype=np.float64)
    f = np.arange(CB, dtype=np.float64)
    ang = 2.0 * np.pi * np.outer(f, n) / (2 * CB)
    fwd = np.concatenate([np.cos(ang), -np.sin(ang)], axis=0)
    fwd[CB] = np.cos(np.pi * n)
    scale = np.full((2 * CB, 1), 2.0 / (2 * CB))
    scale[0] = scale[CB] = 1.0 / (2 * CB)
    inv = (fwd * scale).T
    sgn = np.where(np.arange(CB) % 2 == 0, 1.0, -1.0)
    sgn2 = np.concatenate([sgn, sgn])[:, None]
    sgn2[CB] = 1.0
    return (jnp.asarray(fwd, F32).astype(BF16), jnp.asarray(inv, F32).astype(BF16),
            jnp.asarray(sgn2, F32))


def _filter_features(L):
    t = np.linspace(0.0, 1.0, L)[:, None]
    w = 2.0 * np.pi * np.arange(L)[:, None] / L
    bands = np.linspace(1e-4, EMB_BANDS - 1, EMB_BANDS)
    z = np.concatenate([t, np.cos(bands * w), -np.sin(bands * w)], axis=-1)
    return t, z


def _filter_tables():
    zs, ts, ms = [], [], []
    for L in (LP, LS):
        t, z = _filter_features(L)
        z = np.pad(z, ((0, 0), (0, FILTER_HIDDEN - EMB_DIM)))
        idx = np.abs(np.arange(2 * L) - L) % L
        zs.append(z[idx])
        ts.append(t[idx])
        ms.append((np.arange(2 * L) != 0).astype(np.float64)[:, None])
    return tuple(jnp.asarray(np.concatenate(a), F32) for a in (zs, ts, ms))


N_WIN = 1 + 2 * NB_S - 1


def _filt_kernel(z_ref, t_ref, m_ref, w1_ref, b1_ref, fr_ref, w2_ref, b2_ref, w3_ref,
                 ad_ref, f_ref, sg_ref, g_ref, nyq_ref, prev_scr):
    c = pl.program_id(0)

    @pl.when(c == 0)
    def _():
        prev_scr[...] = jnp.zeros_like(prev_scr)

    fr = fr_ref[...]
    hid = jnp.sin(fr * (_dot(z_ref[...].astype(BF16), w1_ref[...].astype(BF16)) + b1_ref[...]))
    hid = jnp.sin(fr * (_dot(hid.astype(BF16), w2_ref[...].astype(BF16)) + b2_ref[...]))
    h = _dot(hid.astype(BF16), w3_ref[...].astype(BF16))
    taps = h * jnp.exp(-t_ref[...] * ad_ref[...]) * m_ref[...]
    a = _dot(f_ref[...], taps.astype(BF16))
    g = a + sg_ref[...] * prev_scr[...]
    prev_scr[...] = a
    g_ref[...] = g
    nyq_ref[...] = g[CB:CB + 1, :]
    g_ref[CB:CB + 1, :] = jnp.zeros((1, D), F32)


def _is_bwd_chunk(c):
    return jnp.logical_or(c == 0, jnp.logical_and(c >= 2, c < 2 + NB_S))


def _window_of_chunk(c):
    return jnp.where(c <= 1, 0, jnp.maximum(c - 2, 1))


def _filter_spectra(tabs, fwd, sgn, j, w1, b1, freq, w2, b2, w3, absd):
    z_tab, t_tab, m_tab = tabs
    small = lambda shape: pl.BlockSpec(shape, lambda c: (0, 0))
    hid = _lay((1, FILTER_HIDDEN), j)
    return pl.pallas_call(
        _filt_kernel,
        grid=(N_CHUNK,),
        in_specs=[
            pl.BlockSpec((CB, FILTER_HIDDEN), lambda c: (c, 0)),
            pl.BlockSpec((CB, 1), lambda c: (c, 0)),
            pl.BlockSpec((CB, 1), lambda c: (c, 0)),
            _lay((FILTER_HIDDEN, FILTER_HIDDEN), j), hid, hid,
            _lay((FILTER_HIDDEN, FILTER_HIDDEN), j), hid,
            pl.BlockSpec((None, FILTER_HIDDEN, D),
                         lambda c: (j, 0, jnp.where(_is_bwd_chunk(c), 1, 0))),
            small((1, D)),
            small((2 * CB, CB)),
            small((2 * CB, 1)),
        ],
        out_specs=[pl.BlockSpec((None, 2 * CB, D), lambda c: (_window_of_chunk(c), 0, 0)),
                   pl.BlockSpec((None, 1, D), lambda c: (_window_of_chunk(c), 0, 0))],
        out_shape=[jax.ShapeDtypeStruct((N_WIN, 2 * CB, D), F32),
                   jax.ShapeDtypeStruct((N_WIN, 1, D), F32)],
        scratch_shapes=[pltpu.VMEM((2 * CB, D), F32)],
        compiler_params=_cp(("arbitrary",)),
        name="hy_filter",
    )(z_tab, t_tab, m_tab, w1, b1, freq, w2, b2, w3, absd, fwd, sgn)


def _conv_kernel(tc, n_cast, z_ref, x0_ref, g_ref, nyq_ref, f_ref, gi_ref, db_ref, *rest):
    o_ref, u_scr = rest[n_cast], rest[-1]
    _cast_blocks(rest[:n_cast], rest[n_cast + 1:-1])
    unit = pl.program_id(1)
    fmat = f_ref[...]
    gmat = gi_ref[...]
    row0 = lax.broadcasted_iota(jnp.int32, (CB, tc), 0) == 0
    db = db_ref[...]

    def emit(blk, yre, yim, ynyq):
        rows = pl.ds(blk * CB, CB)
        yspec = jnp.concatenate([yre, jnp.where(row0, ynyq, yim)], axis=0)
        y = _dot(gmat, yspec.astype(BF16))
        o_ref[rows, :] = (x0_ref[rows, :] * (y + z_ref[rows, :] * db)).astype(BF16)

    @pl.when(unit < MP // LS)
    def _():
        gre, gim, gnyq = g_ref[0, :CB, :], g_ref[0, CB:, :], nyq_ref[0]
        for s in range(LS // LP):
            u = _dot(fmat, z_ref[pl.ds(s * CB, CB), :])
            ure, unyq = u[:CB], u[CB:CB + 1]
            uim = jnp.where(row0, 0.0, u[CB:])
            emit(s, gre * ure - gim * uim, gre * uim + gim * ure, gnyq * unyq)

    @pl.when(unit >= MP // LS)
    def _():
        unyq = []
        for j in range(NB_S):
            u = _dot(fmat, z_ref[pl.ds(j * CB, CB), :])
            unyq.append(u[CB:CB + 1])
            u_scr[j] = u
            u_scr[j, CB:CB + 1, :] = jnp.zeros((1, tc), F32)
        for i in range(NB_S):
            win = [NB_S + i - j for j in range(NB_S)]
            parts_re, parts_im = [], []
            for r in range(0, CB, MAC_ROWS):
                re, im = slice(r, r + MAC_ROWS), slice(CB + r, CB + r + MAC_ROWS)
                yre = yim = None
                for j, w in enumerate(win):
                    gre, gim = g_ref[w, re, :], g_ref[w, im, :]
                    ure, uim = u_scr[j, re, :], u_scr[j, im, :]
                    pre, pim = gre * ure - gim * uim, gre * uim + gim * ure
                    yre = pre if yre is None else yre + pre
                    yim = pim if yim is None else yim + pim
                parts_re.append(yre)
                parts_im.append(yim)
            ynyq = nyq_ref[win[0]] * unyq[0]
            for j in range(1, NB_S):
                ynyq = ynyq + nyq_ref[win[j]] * unyq[j]
            emit(i, jnp.concatenate(parts_re, axis=0), jnp.concatenate(parts_im, axis=0), ynyq)


def _long_conv(z, x0, spectra, fwd, inv, d_bias, j, cast_jobs, tc=256):
    g_win, g_nyq = spectra
    units = M // LS
    assert (D // tc) * units == CAST_STEPS
    c_in, c_out, c_shape, c_args = _cast_plumbing(cast_jobs, lambda c, u: c * units + u)
    return pl.pallas_call(
        functools.partial(_conv_kernel, tc, len(c_in)),
        grid=(D // tc, units),
        in_specs=[
            pl.BlockSpec((LS, tc), lambda c, u: (u, c)),
            pl.BlockSpec((LS, tc), lambda c, u: (u, c)),
            pl.BlockSpec((N_WIN, 2 * CB, tc), lambda c, u: (0, 0, c)),
            pl.BlockSpec((N_WIN, 1, tc), lambda c, u: (0, 0, c)),
            pl.BlockSpec((2 * CB, CB), lambda c, u: (0, 0)),
            pl.BlockSpec((CB, 2 * CB), lambda c, u: (0, 0)),
            pl.BlockSpec((None, 1, tc), lambda c, u: (j, 0, c)),
        ] + c_in,
        out_specs=[pl.BlockSpec((LS, tc), lambda c, u: (u, c))] + c_out,
        out_shape=[jax.ShapeDtypeStruct((M, D), BF16)] + c_shape,
        scratch_shapes=[pltpu.VMEM((NB_S, 2 * CB, tc), F32)],
        compiler_params=_cp(("arbitrary", "arbitrary")),
        name="hy_conv",
    )(z, x0, g_win, g_nyq, fwd, inv, d_bias, *c_args)


def _rope_tables(L):
    rows = L // GRID_W
    r = np.repeat(np.arange(rows, dtype=np.float64), GRID_W)
    cidx = np.tile(np.arange(GRID_W, dtype=np.float64), rows)
    inv = ROPE_THETA ** (-np.arange(ROT_FREQS, dtype=np.float64) / ROT_FREQS)
    ar = r[:, None] * inv
    ac = cidx[:, None] * inv
    cos = np.concatenate([np.cos(ar), np.cos(ar), np.cos(ac), np.cos(ac)] * 2, axis=-1)
    sin = np.concatenate([np.sin(ar), np.sin(ar), np.sin(ac), np.sin(ac)] * 2, axis=-1)
    first_half = (np.arange(V_DIM) % (2 * ROT_FREQS)) < ROT_FREQS
    sin_a = np.where(first_half, -sin, 0.0)
    sin_b = np.where(first_half, 0.0, sin)
    return tuple(jnp.asarray(a, F32) for a in (cos, sin_a, sin_b))


def _rope(x, cos, sin_a, sin_b):
    return (x * cos + pltpu.roll(x, V_DIM - ROT_FREQS, axis=1) * sin_a
            + pltpu.roll(x, ROT_FREQS, axis=1) * sin_b)


HEADS_PER_CHUNK = CK // V_DIM


def _qkv_chunks():
    per_part = D // CK
    return [(slice(c * CK, (c + 1) * CK), c // per_part, (c % per_part) * HEADS_PER_CHUNK)
            for c in range(3 * per_part)]


def _qkv_c_kernel(first, x_ref, nw_ref, sh_ref, sc_ref, w_ref, *rest):
    qkv_ref, nk_ref, nv_ref = rest[-3:]
    if first:
        for ref in (nk_ref, nv_ref):
            ref[:, 1:] = jnp.zeros((TM // LP, N_ATTN - 1, N_HEADS, LP, V_DIM), F32)
    h = _hmod(x_ref[...], nw_ref, sh_ref, sc_ref)
    for cs, part, head0 in _qkv_chunks():
        u = _dot(h, w_ref[:, cs])
        qkv_ref[:, cs] = (u * QSCALE if part == 0 else u).astype(BF16)
        if part > 0:
            cache = nk_ref if part == 1 else nv_ref
            for s in range(TM // LP):
                for hh in range(HEADS_PER_CHUNK):
                    blk = u[s * LP:(s + 1) * LP, hh * V_DIM:(hh + 1) * V_DIM]
                    if first:
                        cache[s, 0, head0 + hh] = blk
                    else:
                        cache[s, head0 + hh] = blk


def _qkv_ctx(x, nw_all, mods, layer, j, w_bf, prev):
    seqs = TM // LP
    in_specs = [_tile_spec(D, _joint_tile), _norm_spec(layer, 0), _mod_spec(layer, 0),
                _mod_spec(layer, 1), _lay((D, 3 * D), None)]
    args = [x, nw_all, mods, mods, w_bf]
    aliases = {}
    if prev is None:
        assert j == 0
        cache = pl.BlockSpec((seqs, N_ATTN, N_HEADS, LP, V_DIM), lambda i: (i, 0, 0, 0, 0))
    else:
        in_specs += [pl.BlockSpec(memory_space=pl.ANY)] * 2
        args += list(prev)
        aliases = {len(args) - 2: 1, len(args) - 1: 2}
        cache = pl.BlockSpec((seqs, None, N_HEADS, LP, V_DIM), lambda i: (i, j, 0, 0, 0))
    cache_shape = jax.ShapeDtypeStruct((BP, N_ATTN, N_HEADS, LP, V_DIM), F32)
    return pl.pallas_call(
        functools.partial(_qkv_c_kernel, prev is None),
        grid=(N_CTX_TILES,),
        in_specs=in_specs,
        out_specs=[_tile_spec(3 * D, _joint_tile), cache, cache],
        out_shape=[jax.ShapeDtypeStruct((MP, 3 * D), BF16), cache_shape, cache_shape],
        input_output_aliases=aliases,
        compiler_params=_cp(("arbitrary",)),
        name="qkv_ctx",
    )(*args)


def _qkv_l_kernel(x_ref, nw_ref, sh_ref, sc_ref, w_ref, cos_ref, sa_ref, sb_ref, qkv_ref):
    h = _hmod(x_ref[...], nw_ref, sh_ref, sc_ref)
    for cs, part, _ in _qkv_chunks():
        u = _dot(h, w_ref[:, cs])
        if part == 2:
            qkv_ref[:, cs] = u.astype(BF16)
            continue
        for hh in range(HEADS_PER_CHUNK):
            r = _rope(u[:, hh * V_DIM:(hh + 1) * V_DIM], cos_ref[...], sa_ref[...], sb_ref[...])
            if part == 0:
                r = r * QSCALE
            qkv_ref[:, cs.start + hh * V_DIM:cs.start + (hh + 1) * V_DIM] = r.astype(BF16)


def _qkv_lat(x, nw_all, mods, layer, j, w_bf, ropes):
    tile = lambda i: i + N_CTX_TILES
    tab = pl.BlockSpec((TM, V_DIM), lambda i: (i % (LS // TM), 0))
    return pl.pallas_call(
        _qkv_l_kernel,
        grid=(N_LAT_TILES,),
        in_specs=[_tile_spec(D, tile), _norm_spec(layer, 0), _mod_spec(layer, 0, tile),
                  _mod_spec(layer, 1, tile), _lay((D, 3 * D), None), tab, tab, tab],
        out_specs=_tile_spec(3 * D, _joint_tile),
        out_shape=jax.ShapeDtypeStruct((MS, 3 * D), BF16),
        compiler_params=_cp(("arbitrary",)),
        name="qkv_lat",
    )(x, nw_all, mods, mods, w_bf, *ropes)


def _lambda(lv, lam_init):
    a = jnp.exp(jnp.sum(lv[0:1, :] * lv[1:2, :], axis=-1, keepdims=True))
    b = jnp.exp(jnp.sum(lv[2:3, :] * lv[3:4, :], axis=-1, keepdims=True))
    return a - b + lam_init


def _diff_attn(q, chunks, lam, lam_init, subln):
    t = q.shape[0]
    lane = lax.broadcasted_iota(jnp.int32, q.shape, 1)
    zero = jnp.zeros_like(q)
    q2 = jnp.concatenate([jnp.where(lane < HEAD_DIM, q, zero),
                          jnp.where(lane < HEAD_DIM, zero, q)], axis=0)
    m = l = acc = None
    for k, v in chunks:
        s = lax.dot_general(q2, k, (((1,), (1,)), ((), ())), preferred_element_type=F32)
        mc = jnp.max(s, axis=-1, keepdims=True)
        m_new = mc if m is None else jnp.maximum(m, mc)
        e = jnp.exp2(s - m_new)
        lc = jnp.sum(e, axis=-1, keepdims=True)
        pv = _dot(e.astype(BF16), v)
        if m is None:
            l, acc = lc, pv
        else:
            alpha = jnp.exp2(m - m_new)
            l = alpha * l + lc
            acc = alpha * acc + pv
        m = m_new
    o = acc[:t] * (1.0 / l[:t]) - acc[t:] * (lam / l[t:])
    return _rms(o, subln) * (1.0 - lam_init)


def _attn_c_kernel(lam_init, q_ref, k_ref, v_ref, lv_ref, sub_ref, o_ref):
    lam = _lambda(lv_ref[...], lam_init)
    t = LP
    probs = [(slice(s * LP, (s + 1) * LP), slice(h * V_DIM, (h + 1) * V_DIM))
             for s in range(CTX_SEQS) for h in range(N_HEADS)]
    lane = lax.broadcasted_iota(jnp.int32, (t, V_DIM), 1)
    zero = jnp.zeros((t, V_DIM), BF16)
    dn = (((1,), (1,)), ((), ()))
    s = []
    for rs, hs in probs:
        q = q_ref[rs, hs]
        q2 = jnp.concatenate([jnp.where(lane < HEAD_DIM, q, zero),
                              jnp.where(lane < HEAD_DIM, zero, q)], axis=0)
        s.append(lax.dot_general(q2, k_ref[rs, hs], dn, preferred_element_type=F32))
    e = [jnp.exp2(x - jnp.max(x, axis=-1, keepdims=True)) for x in s]
    l = [jnp.sum(x, axis=-1, keepdims=True) for x in e]
    pv = [_dot(x.astype(BF16), v_ref[rs, hs]) for x, (rs, hs) in zip(e, probs)]
    for (rs, hs), acc, lh in zip(probs, pv, l):
        o = acc[:t] * (1.0 / lh[:t]) - acc[t:] * (lam / lh[t:])
        o_ref[rs, hs] = (_rms(o, sub_ref[...]) * (1.0 - lam_init)).astype(BF16)


def _attn_ctx(qkv_c, lamv, subln, j, lam_init):
    part = lambda p: pl.BlockSpec((CTX_SEQS * LP, D), lambda b: (b, p))
    return pl.pallas_call(
        functools.partial(_attn_c_kernel, lam_init),
        grid=(BP // CTX_SEQS,),
        in_specs=[part(0), part(1), part(2),
                  pl.BlockSpec((None, 4, HEAD_DIM), lambda b: (j, 0, 0)),
                  pl.BlockSpec((None, 1, V_DIM), lambda b: (j, 0, 0))],
        out_specs=pl.BlockSpec((CTX_SEQS * LP, D), lambda b: (b, 0)),
        out_shape=jax.ShapeDtypeStruct((MP, D), BF16),
        compiler_params=_cp(("arbitrary",)),
        name="attn_ctx",
    )(qkv_c, qkv_c, qkv_c, lamv, subln)


def _attn_l_kernel(lam_init, n_cast, q_ref, k_ref, v_ref, ck_ref, cv_ref, lv_ref, sub_ref, *rest):
    o_ref = rest[n_cast]
    _cast_blocks(rest[:n_cast], rest[n_cast + 1:])
    lam = _lambda(lv_ref[...], lam_init)
    for h in range(ATT_HEADS):
        hs = slice(h * V_DIM, (h + 1) * V_DIM)
        chunks = [(ck_ref[h].astype(BF16), cv_ref[h].astype(BF16))]
        for c in range(LS // KEY_CHUNK):
            rows = pl.ds(c * KEY_CHUNK, KEY_CHUNK)
            chunks.append((k_ref[rows, hs], v_ref[rows, hs]))
        o = _diff_attn(q_ref[:, hs], chunks, lam, lam_init, sub_ref[...])
        o_ref[:, hs] = o.astype(BF16)


def _attn_lat(qkv_l, cache_k, cache_v, lamv, subln, j, lam_init, cast_jobs, tq=512):
    nq = LS // tq
    width = ATT_HEADS * V_DIM
    groups = N_HEADS // ATT_HEADS
    assert BS * groups * nq == CAST_STEPS
    c_in, c_out, c_shape, c_args = _cast_plumbing(
        cast_jobs, lambda b, g, q: (b * groups + g) * nq + q)
    seq = lambda part: pl.BlockSpec((LS, width), lambda b, g, q: (b, part * groups + g))
    ctx = pl.BlockSpec((None, None, ATT_HEADS, PAST, V_DIM), lambda b, g, q: (b, j, g, 0, 0))
    return pl.pallas_call(
        functools.partial(_attn_l_kernel, lam_init, len(c_in)),
        grid=(BS, groups, nq),
        in_specs=[pl.BlockSpec((tq, width), lambda b, g, q: (b * nq + q, g)),
                  seq(1), seq(2), ctx, ctx,
                  pl.BlockSpec((None, 4, HEAD_DIM), lambda b, g, q: (j, 0, 0)),
                  pl.BlockSpec((None, 1, V_DIM), lambda b, g, q: (j, 0, 0))] + c_in,
        out_specs=[pl.BlockSpec((tq, width), lambda b, g, q: (b * nq + q, g))] + c_out,
        out_shape=[jax.ShapeDtypeStruct((MS, D), BF16)] + c_shape,
        compiler_params=_cp(("arbitrary", "arbitrary", "arbitrary")),
        name="attn_lat",
    )(qkv_l, qkv_l, qkv_l, cache_k, cache_v, lamv, subln, *c_args)


def kernel(x_prompt, x_sample, cache_k, cache_v, c, c_ctx, w_ada, b_ada, norm_w, hy_w_in, hy_b_in, hy_w_short, hy_b_short, hy_f_w1, hy_f_b1, hy_f_freq, hy_f_w2, hy_f_b2, hy_f_w3, hy_d_bias, hy_w_out, hy_b_out, at_w_qkv, at_w_out, at_lambda_q1, at_lambda_k1, at_lambda_q2, at_lambda_k2, at_subln, ffn_w_up, ffn_w_dw, ffn_b_dw, ffn_w_down):
    cond8 = jnp.concatenate([c_ctx[None, :], c, jnp.zeros((SUB - 1 - BS, D), F32)], axis=0)
    mods = _ada(cond8, w_ada, b_ada)

    fwd, inv, sgn = _dft_mats()
    tabs = _filter_tables()
    min_decay = math.log(DECAY_TARGET) / DECAY_PCT_LONG
    max_decay = math.log(DECAY_TARGET) / DECAY_PCT_SHORT
    absd = jnp.asarray(np.abs(np.linspace(min_decay, max_decay, D))[None, :], F32)
    ropes = _rope_tables(LS)

    row = lambda a: a.reshape(a.shape[0], 1, a.shape[1])
    nw_all = norm_w.reshape(DEPTH * 4, 1, D)
    w1_pad = jnp.pad(hy_f_w1, ((0, 0), (0, FILTER_HIDDEN - EMB_DIM), (0, 0)))
    lamv = jnp.stack([at_lambda_q1, at_lambda_k1, at_lambda_q2, at_lambda_k2], axis=1)
    subln = row(at_subln)
    ffn = lambda l: [(ffn_w_up, l), (ffn_w_down, l)]
    w_in_bf = hy_w_in[0].astype(BF16)

    x_parts = [x_prompt.reshape(MP, D), x_sample.reshape(MS, D)]
    caches = None
    for i in range(DEPTH):
        j = i // 2
        last = i == DEPTH - 1
        if i % 2 == 0:
            own = ffn(0) + [(hy_w_out, 0)] if i == 0 else []
            x0, z, *cast = _hy_in(x_parts, nw_all, mods, i, j, w_in_bf, row(hy_b_in), hy_w_short,
                                  row(hy_b_short),
                                  own + ffn(i + 1) + [(at_w_qkv, j), (at_w_out, j)])
            if i == 0:
                w_up_bf, w_down_bf, w_out_bf = cast[:3]
            nxt = cast[len(own):]
            spectra = _filter_spectra(tabs, fwd, sgn, j, w1_pad, row(hy_f_b1), row(hy_f_freq),
                                      hy_f_w2, row(hy_f_b2), hy_f_w3, absd)
            a, = _long_conv(z, x0, spectra, fwd, inv, row(hy_d_bias), j, [])
            a_parts, b_out = [a], row(hy_b_out)
        else:
            lam_init = 0.8 - 0.6 * math.exp(-0.3 * i)
            x = x_parts[0]
            qkv_c, new_k, new_v = _qkv_ctx(x, nw_all, mods, i, j, w_qkv_bf, caches)
            caches = (new_k, new_v)
            qkv_l = _qkv_lat(x, nw_all, mods, i, j, w_qkv_bf, ropes)
            jobs = [] if last else ffn(i + 1) + [(hy_w_in, j + 1), (hy_w_out, j + 1)]
            o_lat, *nxt = _attn_lat(qkv_l, cache_k, cache_v, lamv, subln, j, lam_init, jobs)
            a_parts, b_out = [_attn_ctx(qkv_c, lamv, subln, j, lam_init), o_lat], None
        y = _tail(a_parts, x_parts, nw_all, mods, i, j, w_out_bf, b_out, w_up_bf, ffn_w_dw,
                  row(ffn_b_dw), w_down_bf, split_out=last)
        x_parts = list(y) if last else [y]
        if not last:
            w_up_bf, w_down_bf, w_next_in, w_out_bf = nxt
            if i % 2 == 0:
                w_qkv_bf = w_next_in
            else:
                w_in_bf = w_next_in

    return (x_parts[0].reshape(BP, LP, D), x_parts[1].reshape(BS, LS, D), caches[0], caches[1])
```
